```python
import math
import jax, jax.numpy as jnp
from jax import lax
import numpy as np

D_MODEL = 1024
BATCH = 8
SEQ = 4096
DEPTH = 2

GRID_W = 64
CTX_LEN = 256

D_MIX = D_MODEL
N_HEADS_MLA = 8
QK_NOPE_DIM = 64
QK_ROPE_DIM = 32
QK_HEAD_DIM = QK_NOPE_DIM + QK_ROPE_DIM
V_HEAD_DIM = 64
Q_LORA_RANK = 384
KV_LORA_RANK = 256
D_ATTN = N_HEADS_MLA * V_HEAD_DIM
D_CONV = D_MIX // 4
CONV_WIDTH = 31
D_POOL = D_MIX // 4
POOL_WINDOWS = (2, 4, 8, 16)
POOL_GROUP_DIM = D_POOL // len(POOL_WINDOWS)
ROPE_THETA = 10000.0
Q_BLOCK = 128

OFF_Q = 0
OFF_KV = OFF_Q + Q_LORA_RANK
OFF_KR = OFF_KV + KV_LORA_RANK
OFF_CONV = OFF_KR + QK_ROPE_DIM
OFF_POOL = OFF_CONV + 2 * D_CONV
D_IN = OFF_POOL + D_POOL

N_GROUPS = 4
EXPERTS_PER_GROUP = 8
N_EXPERTS = N_GROUPS * EXPERTS_PER_GROUP
TOP_K_IN_GROUP = 2
D_EXPERT = D_MODEL // 4

LN_EPS = 1e-5
RMS_EPS = 1e-6
DEEPNORM_ALPHA = (2 * DEPTH) ** 0.25
DEEPNORM_BETA = (8 * DEPTH) ** -0.25

kernel_name = "hymba_mla_conformer_pool_hmoe_diffusion_trunk"


def layer_norm(x, g, b, eps=LN_EPS):
    xf = x.astype(jnp.float32)
    mu = jnp.mean(xf, -1, keepdims=True)
    var = jnp.mean(jnp.square(xf - mu), -1, keepdims=True)
    y = (xf - mu) * lax.rsqrt(var + eps)
    return (y * g.astype(jnp.float32) + b.astype(jnp.float32)).astype(x.dtype)


def rms_norm(x, g, eps=RMS_EPS):
    xf = x.astype(jnp.float32)
    y = xf * lax.rsqrt(jnp.mean(jnp.square(xf), -1, keepdims=True) + eps)
    return (y * g.astype(jnp.float32)).astype(x.dtype)


def modulate(x, shift, scale):
    return x * (1 + scale) + shift


def axial_rope_tables(n_tokens):
    rows = n_tokens // GRID_W
    row = jnp.repeat(jnp.arange(rows), GRID_W)
    col = jnp.tile(jnp.arange(GRID_W), rows)
    d_axis = QK_ROPE_DIM // 2
    inv_freq = jnp.power(ROPE_THETA, -jnp.arange(0, d_axis, 2, dtype=jnp.float32) / d_axis)

    def axis_angles(p):
        a = p.astype(jnp.float32)[:, None] * inv_freq[None, :]
        return jnp.concatenate([a, a], -1)

    ang = jnp.concatenate([axis_angles(row), axis_angles(col)], -1)
    return jnp.cos(ang), jnp.sin(ang)


def rotate_axial_half(x):
    xs = x.reshape(x.shape[:-1] + (2, 2, QK_ROPE_DIM // 4))
    rot = jnp.stack([-xs[..., 1, :], xs[..., 0, :]], axis=-2)
    return rot.reshape(x.shape)


def apply_rope(x, cos, sin):
    y = x.astype(jnp.float32) * cos + rotate_axial_half(x).astype(jnp.float32) * sin
    return y.astype(x.dtype)


def mla_query(p_q, g_q, w_uq):
    lead = p_q.shape[:-1]
    q = rms_norm(p_q, g_q) @ w_uq
    return q.reshape(lead + (N_HEADS_MLA, QK_HEAD_DIM))


def mla_key_value(p_kvr, g_kv, w_ukv):
    lead = p_kvr.shape[:-1]
    c_kv, k_rope = p_kvr[..., :KV_LORA_RANK], p_kvr[..., KV_LORA_RANK:]
    kv = (rms_norm(c_kv, g_kv) @ w_ukv).reshape(lead + (N_HEADS_MLA, QK_NOPE_DIM + V_HEAD_DIM))
    return kv[..., :QK_NOPE_DIM], k_rope, kv[..., QK_NOPE_DIM:]


def assemble_keys(k_nope, k_rope):
    k_rope_h = jnp.broadcast_to(k_rope[..., None, :], k_nope.shape[:-1] + (QK_ROPE_DIM,))
    return jnp.concatenate([k_nope, k_rope_h], -1)


def latent_attention(q, k_lat, v_lat, k_ctx, v_ctx):
    k = jnp.concatenate([k_ctx, k_lat], 1)
    v = jnp.concatenate([v_ctx, v_lat], 1)
    B, S, H, Dq = q.shape
    n_blk = S // Q_BLOCK
    scale = 1.0 / math.sqrt(Dq)
    qb = q.reshape(B, n_blk, Q_BLOCK, H, Dq).transpose(1, 0, 2, 3, 4)

    def one_block(q_blk):
        s = jnp.einsum('bqhd,bkhd->bhqk', q_blk, k, preferred_element_type=jnp.float32) * scale
        p = jax.nn.softmax(s, axis=-1).astype(v.dtype)
        return jnp.einsum('bhqk,bkhd->bqhd', p, v)

    o = lax.map(one_block, qb)
    return o.transpose(1, 0, 2, 3, 4).reshape(B, S, H * V_HEAD_DIM)


def context_attention(q, k, v):
    B, N, H, Dq = q.shape
    s = jnp.einsum('bqhd,bkhd->bhqk', q, k, preferred_element_type=jnp.float32) * (1.0 / math.sqrt(Dq))
    p = jax.nn.softmax(s, axis=-1).astype(v.dtype)
    return jnp.einsum('bhqk,bkhd->bqhd', p, v).reshape(B, N, H * V_HEAD_DIM)


def conformer_conv(p_conv, conv_w, conv_b, ln_g, ln_b):
    a, gt = p_conv[..., :D_CONV], p_conv[..., D_CONV:]
    u = a * jax.nn.sigmoid(gt)
    pad = CONV_WIDTH // 2
    u = lax.conv_general_dilated(u, conv_w[:, None, :], window_strides=(1,), padding=[(pad, pad)],
                                 dimension_numbers=('NWC', 'WIO', 'NWC'), feature_group_count=D_CONV)
    u = layer_norm(u + conv_b, ln_g, ln_b)
    return jax.nn.silu(u)


def multiscale_pool(u, pool_w, pool_scale):
    B, N, _ = u.shape
    t = jnp.arange(N)
    outs = []
    for gi, w in enumerate(POOL_WINDOWS):
        ug = u[..., gi * POOL_GROUP_DIM:(gi + 1) * POOL_GROUP_DIM].astype(jnp.float32)
        cs = jnp.concatenate([jnp.zeros((B, 1, POOL_GROUP_DIM), jnp.float32), jnp.cumsum(ug, axis=1)], 1)
        lo = jnp.clip(t - w // 2, 0, N - 1)
        hi = jnp.clip(t + w // 2 - 1, 0, N - 1)
        cnt = (hi - lo + 1).astype(jnp.float32)[None, :, None]
        mixed = ((cs[:, hi + 1] - cs[:, lo]) / cnt - ug).astype(u.dtype)
        outs.append(mixed @ pool_w[gi])
    return jnp.concatenate(outs, -1) * pool_scale


def hierarchical_moe(h, w_rg, b_rg, w_re, b_re, w_gate, w_up, w_down):
    T = h.shape[0]
    tok = jnp.arange(T)
    g_logits = (h @ w_rg + b_rg).astype(jnp.float32)
    g_prob = jax.nn.softmax(g_logits, -1)
    g_sel = jnp.argmax(g_logits, -1)
    p_sel = g_prob[tok, g_sel]
    e_logits = (h @ w_re + b_re).astype(jnp.float32).reshape(T, N_GROUPS, EXPERTS_PER_GROUP)
    e_logits_g = e_logits[tok, g_sel]
    top_v, top_i = lax.top_k(e_logits_g, TOP_K_IN_GROUP)
    top_w = jax.nn.softmax(top_v, -1) * p_sel[:, None]
    expert_id = g_sel[:, None] * EXPERTS_PER_GROUP + top_i
    combine = jnp.sum(jax.nn.one_hot(expert_id, N_EXPERTS, dtype=jnp.float32) * top_w[..., None], 1)
    combine = combine.astype(h.dtype)
    y = jnp.zeros_like(h)
    for e in range(N_EXPERTS):
        hid = jax.nn.silu(h @ w_gate[e]) * (h @ w_up[e])
        y = y + combine[:, e:e + 1] * (hid @ w_down[e])
    return y


def post_norm(x, y, gate, g, b):
    return layer_norm(DEEPNORM_ALPHA * x + gate * y, g, b)


def setup_inputs(seed: int = 0) -> dict:
    key = jax.random.key(seed)
    ks = iter(jax.random.split(key, 40))

    def nrm(shape, scale):
        return jax.random.normal(next(ks), shape, jnp.float32) * scale

    L, D = DEPTH, D_MODEL
    return {
        "x": nrm((BATCH, SEQ, D), 1.0),
        "c": nrm((BATCH, D), 1.0),
        "ctx": nrm((BATCH, CTX_LEN, D), 1.0),
        "c_ctx": nrm((D,), 1.0),
        "w_ada": nrm((L, D, 6 * D), 0.5 * D ** -0.5),
        "b_ada": nrm((L, 6 * D), 0.01),
        "w_in": nrm((L, D, D_IN), D ** -0.5),
        "g_q": 1.0 + nrm((L, Q_LORA_RANK), 0.01),
        "w_uq": nrm((L, Q_LORA_RANK, N_HEADS_MLA * QK_HEAD_DIM), Q_LORA_RANK ** -0.5),
        "g_kv": 1.0 + nrm((L, KV_LORA_RANK), 0.01),
        "w_ukv": nrm((L, KV_LORA_RANK, N_HEADS_MLA * (QK_NOPE_DIM + V_HEAD_DIM)), KV_LORA_RANK ** -0.5),
        "conv_w": nrm((L, CONV_WIDTH, D_CONV), CONV_WIDTH ** -0.5),
        "conv_b": nrm((L, D_CONV), 0.01),
        "conv_ln_g": 1.0 + nrm((L, D_CONV), 0.01),
        "conv_ln_b": nrm((L, D_CONV), 0.01),
        "pool_w": nrm((L, len(POOL_WINDOWS), POOL_GROUP_DIM, POOL_GROUP_DIM), POOL_GROUP_DIM ** -0.5),
        "pool_scale": 1.0 + nrm((L, D_POOL), 0.01),
        "w_out": nrm((L, D_MIX, D), DEEPNORM_BETA * D_MIX ** -0.5),
        "ln1_g": 1.0 + nrm((L, D), 0.01),
        "ln1_b": nrm((L, D), 0.01),
        "w_router_group": nrm((L, D, N_GROUPS), D ** -0.5),
        "b_router_group": nrm((L, N_GROUPS), 0.01),
        "w_router_expert": nrm((L, D, N_EXPERTS), D ** -0.5),
        "b_router_expert": nrm((L, N_EXPERTS), 0.01),
        "w_gate": nrm((L, N_EXPERTS, D, D_EXPERT), D ** -0.5),
        "w_up": nrm((L, N_EXPERTS, D, D_EXPERT), D ** -0.5),
        "w_down": nrm((L, N_EXPERTS, D_EXPERT, D), DEEPNORM_BETA * D_EXPERT ** -0.5),
        "ln2_g": 1.0 + nrm((L, D), 0.01),
        "ln2_b": nrm((L, D), 0.01),
    }


def reference(x, c, ctx, c_ctx, w_ada, b_ada, w_in, g_q, w_uq, g_kv, w_ukv, conv_w, conv_b, conv_ln_g,
              conv_ln_b, pool_w, pool_scale, w_out, ln1_g, ln1_b, w_router_group, b_router_group,
              w_router_expert, b_router_expert, w_gate, w_up, w_down, ln2_g, ln2_b):
    B, S, D = x.shape
    cos, sin = axial_rope_tables(S)
    xc = ctx
    for l in range(DEPTH):
        last = l == DEPTH - 1
        ada = jax.nn.silu(c) @ w_ada[l] + b_ada[l]
        sh1, sc1, g1, sh2, sc2, g2 = [a[:, None, :] for a in jnp.split(ada, 6, axis=-1)]
        ada_c = jax.nn.silu(c_ctx) @ w_ada[l] + b_ada[l]
        csh1, csc1, cg1, csh2, csc2, cg2 = jnp.split(ada_c, 6, axis=-1)

        h = modulate(x, sh1, sc1)
        hc = modulate(xc, csh1, csc1)
        p = h @ w_in[l]
        if last:
            pc_kvr = hc @ w_in[l][:, OFF_KV:OFF_CONV]
        else:
            pc = hc @ w_in[l]
            pc_kvr = pc[..., OFF_KV:OFF_CONV]

        kc_nope, kc_rope, v_ctx = mla_key_value(pc_kvr, g_kv[l], w_ukv[l])
        k_ctx = assemble_keys(kc_nope, kc_rope)

        q = mla_query(p[..., OFF_Q:OFF_KV], g_q[l], w_uq[l])
        q = jnp.concatenate([q[..., :QK_NOPE_DIM],
                             apply_rope(q[..., QK_NOPE_DIM:], cos[:, None, :], sin[:, None, :])], -1)
        k_nope, k_rope, v_lat = mla_key_value(p[..., OFF_KV:OFF_CONV], g_kv[l], w_ukv[l])
        k_lat = assemble_keys(k_nope, apply_rope(k_rope, cos, sin))
        attn = latent_attention(q, k_lat, v_lat, k_ctx, v_ctx)
        conv = conformer_conv(p[..., OFF_CONV:OFF_POOL], conv_w[l], conv_b[l], conv_ln_g[l], conv_ln_b[l])
        pool = multiscale_pool(p[..., OFF_POOL:], pool_w[l], pool_scale[l])
        mix = jnp.concatenate([attn, conv, pool], -1) @ w_out[l]
        x = post_norm(x, mix, g1, ln1_g[l], ln1_b[l])

        if not last:
            qc = mla_query(pc[..., OFF_Q:OFF_KV], g_q[l], w_uq[l])
            attn_c = context_attention(qc, k_ctx, v_ctx)
            conv_c = conformer_conv(pc[..., OFF_CONV:OFF_POOL], conv_w[l], conv_b[l], conv_ln_g[l], conv_ln_b[l])
            pool_c = multiscale_pool(pc[..., OFF_POOL:], pool_w[l], pool_scale[l])
            mix_c = jnp.concatenate([attn_c, conv_c, pool_c], -1) @ w_out[l]
            xc = post_norm(xc, mix_c, cg1, ln1_g[l], ln1_b[l])

        h2 = modulate(x, sh2, sc2).reshape(B * S, D)
        if last:
            tokens = h2
        else:
            hc2 = modulate(xc, csh2, csc2).reshape(-1, D)
            tokens = jnp.concatenate([h2, hc2], 0)
        ffn = hierarchical_moe(tokens, w_router_group[l], b_router_group[l], w_router_expert[l],
                               b_router_expert[l], w_gate[l], w_up[l], w_down[l])
        x = post_norm(x, ffn[:B * S].reshape(B, S, D), g2, ln2_g[l], ln2_b[l])
        if not last:
            xc = post_norm(xc, ffn[B * S:].reshape(xc.shape), cg2, ln2_g[l], ln2_b[l])
    return x
```

```python
import functools
import math

import jax
import jax.numpy as jnp
from jax import lax
from jax.experimental import pallas as pl
from jax.experimental.pallas import tpu as pltpu

D_MODEL = 1024
SEQ = 4096
CTX_LEN = 256
SEQ_ALL = CTX_LEN + SEQ
GRID_W = 64
N_HEADS = 8
QK_NOPE = 64
QK_ROPE = 32
QK_HEAD = QK_NOPE + QK_ROPE
V_HEAD = 64
Q_RANK = 384
KV_RANK = 256
D_CONV = 256
CONV_WIDTH = 31
D_POOL = 256
POOL_WINDOWS = (2, 4, 8, 16)
ROPE_THETA = 10000.0
OFF_KV = Q_RANK
OFF_KR = OFF_KV + KV_RANK
OFF_CONV = OFF_KR + QK_ROPE
OFF_POOL = OFF_CONV + 2 * D_CONV
N_GROUPS = 4
EXPERTS_PER_GROUP = 8
N_EXPERTS = 32
D_EXPERT = 256
LN_EPS = 1e-5
RMS_EPS = 1e-6
DEPTH = 2
DEEPNORM_ALPHA = (2 * DEPTH) ** 0.25

LANE = 128
TILE = 256
HALO = 16
N_TILES = SEQ_ALL // TILE
HEAD_PAD = 128
VMEM_LIMIT = 56 * 1024 * 1024

XOFF_Q = 0
XOFF_KV = 384
XOFF_KR = 640
XOFF_KRR = 768
XOFF_CONV = 896
XOFF_POOL = 1408
D_IN_EXT = 1664

BF16 = jnp.bfloat16
F32 = jnp.float32


def _dot(a, b):
    return jnp.dot(a, b, preferred_element_type=F32)


def _silu(x):
    return x * jax.nn.sigmoid(x)


def _layer_norm_rows(x, g, b):
    mu = jnp.mean(x, axis=-1, keepdims=True)
    xc = x - mu
    var = jnp.mean(xc * xc, axis=-1, keepdims=True)
    return xc * lax.rsqrt(var + LN_EPS) * g + b


def _ada_kernel(cc_ref, w_ref, b_ref, o_ref):
    a = _silu(cc_ref[...])
    o_ref[0] = jnp.dot(a, w_ref[0], preferred_element_type=F32,
                       precision=lax.Precision.HIGHEST) + b_ref[0]


def _ada(cc, w_ada, b_ada):
    L = w_ada.shape[0]
    n = w_ada.shape[2]
    bn = 512
    return pl.pallas_call(
        _ada_kernel,
        out_shape=jax.ShapeDtypeStruct((L, cc.shape[0], n), F32),
        grid=(L, n // bn),
        in_specs=[
            pl.BlockSpec(cc.shape, lambda l, j: (0, 0)),
            pl.BlockSpec((1, D_MODEL, bn), lambda l, j: (l, 0, j)),
            pl.BlockSpec((1, 1, bn), lambda l, j: (l, 0, j)),
        ],
        out_specs=pl.BlockSpec((1, cc.shape[0], bn), lambda l, j: (l, 0, j)),
        compiler_params=pltpu.CompilerParams(
            dimension_semantics=("parallel", "parallel"), vmem_limit_bytes=VMEM_LIMIT),
        name="ada",
    )(cc, w_ada, b_ada.reshape(L, 1, n))


def _inproj_kernel(x_ref, mod_ref, cq_ref, sq_ref, ck_ref, sk_ref, win_ref, gq_ref, wuq_ref,
                   gkv_ref, wukv_ref, q_ref, k_ref, v_ref, u_ref, pool_ref):
    x = x_ref[0]
    sh = mod_ref[0, 0:1, :]
    sc = mod_ref[0, 1:2, :]
    h = (x * (1.0 + sc) + sh).astype(BF16)
    p = _dot(h, win_ref[...])

    pq = p[:, XOFF_Q:XOFF_KV]
    qn = pq * lax.rsqrt(jnp.mean(pq * pq, axis=-1, keepdims=True) + RMS_EPS) * gq_ref[...]
    q2 = _dot(qn.astype(BF16), wuq_ref[...])
    cq = cq_ref[...]
    sq = sq_ref[...]
    for hd in range(N_HEADS):
        lo = hd * HEAD_PAD
        qh = q2[:, lo:lo + HEAD_PAD] * cq + q2[:, N_HEADS * HEAD_PAD + lo:N_HEADS * HEAD_PAD + lo + HEAD_PAD] * sq
        q_ref[0, :, lo:lo + HEAD_PAD] = qh.astype(BF16)

    pkv = p[:, XOFF_KV:XOFF_KR]
    kvn = pkv * lax.rsqrt(jnp.mean(pkv * pkv, axis=-1, keepdims=True) + RMS_EPS) * gkv_ref[...]
    kv = _dot(kvn.astype(BF16), wukv_ref[...])
    kr = p[:, XOFF_KR:XOFF_KRR] * ck_ref[...] + p[:, XOFF_KRR:XOFF_CONV] * sk_ref[...]
    for hd in range(N_HEADS):
        lo = hd * HEAD_PAD
        k_ref[0, :, lo:lo + HEAD_PAD] = (kv[:, lo:lo + HEAD_PAD] + kr).astype(BF16)
    v_ref[0] = kv[:, N_HEADS * HEAD_PAD:].astype(BF16)

    a = p[:, XOFF_CONV:XOFF_CONV + D_CONV]
    gt = p[:, XOFF_CONV + D_CONV:XOFF_POOL]
    u_ref[0] = (a * jax.nn.sigmoid(gt)).astype(BF16)
    pool_ref[0] = p[:, XOFF_POOL:]


def _inproj(xa, mods, tabs, win_ext, gq, wuq_ext, gkv, wukv_ext):
    B = xa.shape[0]
    nb = mods.shape[0] - 1

    def mod_idx(b, i):
        return (jnp.where(i == 0, nb, b), 0, 0)

    tab_spec = pl.BlockSpec((TILE, LANE), lambda b, i: (i, 0))
    full = lambda a: pl.BlockSpec(a.shape, lambda b, i: (0,) * a.ndim)
    tok = lambda w: pl.BlockSpec((1, TILE, w), lambda b, i: (b, i, 0))
    return pl.pallas_call(
        _inproj_kernel,
        out_shape=(
            jax.ShapeDtypeStruct((B, SEQ_ALL, N_HEADS * HEAD_PAD), BF16),
            jax.ShapeDtypeStruct((B, SEQ_ALL, N_HEADS * HEAD_PAD), BF16),
            jax.ShapeDtypeStruct((B, SEQ_ALL, N_HEADS * V_HEAD), BF16),
            jax.ShapeDtypeStruct((B, SEQ_ALL, D_CONV), BF16),
            jax.ShapeDtypeStruct((B, SEQ_ALL, D_POOL), F32),
        ),
        grid=(B, N_TILES),
        in_specs=[tok(D_MODEL), pl.BlockSpec((1, 8, D_MODEL), mod_idx),
                  tab_spec, tab_spec, tab_spec, tab_spec,
                  full(win_ext), full(gq), full(wuq_ext), full(gkv), full(wukv_ext)],
        out_specs=(tok(N_HEADS * HEAD_PAD), tok(N_HEADS * HEAD_PAD), tok(N_HEADS * V_HEAD),
                   tok(D_CONV), tok(D_POOL)),
        compiler_params=pltpu.CompilerParams(
            dimension_semantics=("parallel", "parallel"), vmem_limit_bytes=VMEM_LIMIT),
        name="inproj",
    )(xa, mods, *tabs, win_ext, gq, wuq_ext, gkv, wukv_ext)


def _attend(q, k_ref, v_ref, n_keys):
    lane = lax.broadcasted_iota(jnp.int32, (n_keys, 2 * V_HEAD), 1)
    vv = v_ref[0, 0:n_keys, :]
    acc = jnp.zeros((TILE, 2 * V_HEAD), F32)
    for hd in range(2):
        qh = q[:, hd * HEAD_PAD:(hd + 1) * HEAD_PAD]
        kh = k_ref[0, 0:n_keys, hd * HEAD_PAD:(hd + 1) * HEAD_PAD]
        s = lax.dot_general(qh, kh, (((1,), (1,)), ((), ())), preferred_element_type=F32)
        m = jnp.max(s, axis=-1, keepdims=True)
        p = jnp.exp(s - m)
        l = jnp.sum(p, axis=-1, keepdims=True)
        in_head = (lane >= hd * V_HEAD) & (lane < (hd + 1) * V_HEAD)
        vh = jnp.where(in_head, vv, jnp.zeros_like(vv))
        acc = acc + _dot(p.astype(BF16), vh) / l
    return acc


def _attn_kernel(q_ref, k_ref, v_ref, o_ref, *, first_tile):
    q = q_ref[0]
    if first_tile == 0:
        i = pl.program_id(2)

        @pl.when(i == 0)
        def _():
            o_ref[0] = _attend(q, k_ref, v_ref, CTX_LEN).astype(BF16)

        @pl.when(i > 0)
        def _():
            o_ref[0] = _attend(q, k_ref, v_ref, SEQ_ALL).astype(BF16)
    else:
        o_ref[0] = _attend(q, k_ref, v_ref, SEQ_ALL).astype(BF16)


def _attention(q, k, v, first_tile):
    B = q.shape[0]
    n_pairs = N_HEADS // 2
    nq = N_TILES - first_tile
    return pl.pallas_call(
        functools.partial(_attn_kernel, first_tile=first_tile),
        out_shape=jax.ShapeDtypeStruct((B, nq * TILE, N_HEADS * V_HEAD), BF16),
        grid=(B, n_pairs, nq),
        in_specs=[
            pl.BlockSpec((1, TILE, 2 * HEAD_PAD), lambda b, hp, i: (b, i + first_tile, hp)),
            pl.BlockSpec((1, SEQ_ALL, 2 * HEAD_PAD), lambda b, hp, i: (b, 0, hp)),
            pl.BlockSpec((1, SEQ_ALL, 2 * V_HEAD), lambda b, hp, i: (b, 0, hp)),
        ],
        out_specs=pl.BlockSpec((1, TILE, 2 * V_HEAD), lambda b, hp, i: (b, i, hp)),
        compiler_params=pltpu.CompilerParams(
            dimension_semantics=("parallel", "parallel", "parallel"), vmem_limit_bytes=VMEM_LIMIT),
        name="attn",
    )(q, k, v)


def _mixout_kernel(attn_ref, u_ref, ul_ref, ur_ref, pm_ref, pl_ref, pr_ref, x_ref, mod_ref,
                   cw_ref, cb_ref, cg_ref, cbeta_ref, pw_ref, ps_ref, wout_ref, g1_ref, b1_ref,
                   wr_ref, br_ref, tri_ref,
                   x1_ref, h2_ref, rinfo_ref, tot_ref,
                   winu, winp, cnt, *, first_tile):
    b = pl.program_id(0)
    i = pl.program_id(1) + first_tile
    left_ok = i >= 2
    right_ok = (i >= 1) & (i <= N_TILES - 2)

    @pl.when((pl.program_id(0) == 0) & (pl.program_id(1) == 0))
    def _():
        cnt[...] = jnp.zeros_like(cnt)

    zu = jnp.zeros((HALO, D_CONV), F32)
    winu[0:HALO, :] = jnp.where(left_ok, ul_ref[0].astype(F32), zu)
    winu[HALO:HALO + TILE, :] = u_ref[0].astype(F32)
    winu[HALO + TILE:, :] = jnp.where(right_ok, ur_ref[0].astype(F32), zu)
    winp[0:HALO, :] = jnp.where(left_ok, pl_ref[0], zu)
    winp[HALO:HALO + TILE, :] = pm_ref[0]
    winp[HALO + TILE:, :] = jnp.where(right_ok, pr_ref[0], zu)

    acc = jnp.zeros((TILE, D_CONV), F32)
    for j in range(CONV_WIDTH):
        off = HALO - CONV_WIDTH // 2 + j
        acc = acc + winu[off:off + TILE, :] * cw_ref[j:j + 1, :]
    conv = _silu(_layer_norm_rows(acc + cb_ref[...], cg_ref[...], cbeta_ref[...]))

    def pw(o):
        return winp[HALO + o:HALO + o + TILE, :]

    centre = pw(0)
    s2 = pw(-1) + centre
    s4 = s2 + pw(-2) + pw(1)
    s8 = s4 + pw(-4) + pw(-3) + pw(2) + pw(3)
    s16 = s8 + pw(-8) + pw(-7) + pw(-6) + pw(-5) + pw(4) + pw(5) + pw(6) + pw(7)
    row = lax.broadcasted_iota(jnp.int32, (TILE, 1), 0)
    seg_len = jnp.where(i == 0, CTX_LEN, SEQ)
    t = jnp.where(i == 0, 0, (i - 1) * TILE) + row

    def mean(sw, w):
        hi = jnp.minimum(t + (w // 2 - 1), seg_len - 1)
        lo = jnp.maximum(t - w // 2, 0)
        return sw / (hi - lo + 1).astype(F32)

    lane_p = lax.broadcasted_iota(jnp.int32, (1, D_POOL), 1)
    gdim = D_POOL // len(POOL_WINDOWS)
    mixed = jnp.where(lane_p < gdim, mean(s2, 2),
                      jnp.where(lane_p < 2 * gdim, mean(s4, 4),
                                jnp.where(lane_p < 3 * gdim, mean(s8, 8), mean(s16, 16)))) - centre
    pool = _dot(mixed.astype(BF16), pw_ref[...]) * ps_ref[...]

    d_attn = N_HEADS * V_HEAD
    mix = (_dot(attn_ref[0], wout_ref[0:d_attn, :])
           + _dot(conv.astype(BF16), wout_ref[d_attn:d_attn + D_CONV, :])
           + _dot(pool.astype(BF16), wout_ref[d_attn + D_CONV:, :]))
    gate1 = mod_ref[0, 2:3, :]
    x1 = _layer_norm_rows(DEEPNORM_ALPHA * x_ref[0] + gate1 * mix, g1_ref[...], b1_ref[...])
    x1_ref[0] = x1

    h2 = x1 * (1.0 + mod_ref[0, 4:5, :]) + mod_ref[0, 3:4, :]
    h2_ref[...] = h2
    logits = _dot(h2.astype(BF16), wr_ref[...]) + br_ref[...]
    lane = lax.broadcasted_iota(jnp.int32, (TILE, LANE), 1)
    neg = jnp.float32(-jnp.inf)
    big = jnp.int32(1 << 20)
    is_g = (lane >= N_EXPERTS) & (lane < N_EXPERTS + N_GROUPS)
    gl = jnp.where(is_g, logits, neg)
    gmax = jnp.max(gl, axis=-1, keepdims=True)
    g_sel = jnp.min(jnp.where(gl == gmax, lane - N_EXPERTS, big), axis=-1, keepdims=True)
    p_sel = 1.0 / jnp.sum(jnp.exp(gl - gmax), axis=-1, keepdims=True)
    in_grp = (lane >= g_sel * EXPERTS_PER_GROUP) & (lane < (g_sel + 1) * EXPERTS_PER_GROUP)
    el = jnp.where(in_grp, logits, neg)
    v1 = jnp.max(el, axis=-1, keepdims=True)
    i1 = jnp.min(jnp.where(el == v1, lane, big), axis=-1, keepdims=True)
    el2 = jnp.where(lane == i1, neg, el)
    v2 = jnp.max(el2, axis=-1, keepdims=True)
    i2 = jnp.min(jnp.where(el2 == v2, lane, big), axis=-1, keepdims=True)
    e2 = jnp.exp(v2 - v1)
    w1 = p_sel / (1.0 + e2)
    w2 = p_sel * e2 / (1.0 + e2)

    oh1 = (lane == i1).astype(F32)
    oh2 = (lane == i2).astype(F32)
    both = oh1 + oh2
    before = _dot(tri_ref[...], both.astype(BF16)) + cnt[0:1, :]
    r1 = jnp.sum(oh1 * before, axis=-1, keepdims=True)
    r2 = jnp.sum(oh2 * before, axis=-1, keepdims=True)
    cnt[...] = cnt[...] + jnp.sum(both, axis=0, keepdims=True)
    tot_ref[...] = cnt[...]
    fields = (w1, w2, i1.astype(F32), i2.astype(F32), r1, r2)
    rinfo = jnp.zeros((TILE, LANE), F32)
    for n, f in enumerate(fields):
        rinfo = jnp.where(lane == n, f, rinfo)
    rinfo_ref[...] = rinfo


def _mixout(attn, u, poolin, xa, mods, cw, cb, cg, cbeta, pw, ps, wout, g1, b1, wr, br, tri,
            first_tile):
    B = xa.shape[0]
    nt = N_TILES - first_tile
    nb = mods.shape[0] - 1
    hb = TILE // HALO
    n_hb = SEQ_ALL // HALO

    def mod_idx(b, i):
        return (jnp.where(i + first_tile == 0, nb, b), 0, 0)

    tok = lambda w: pl.BlockSpec((1, TILE, w), lambda b, i: (b, i + first_tile, 0))
    own = lambda w: pl.BlockSpec((1, TILE, w), lambda b, i: (b, i, 0))
    left = lambda w: pl.BlockSpec(
        (1, HALO, w), lambda b, i: (b, jnp.maximum((i + first_tile) * hb - 1, 0), 0))
    right = lambda w: pl.BlockSpec(
        (1, HALO, w), lambda b, i: (b, jnp.minimum((i + first_tile + 1) * hb, n_hb - 1), 0))
    full = lambda a: pl.BlockSpec(a.shape, lambda b, i: (0,) * a.ndim)
    flat = pl.BlockSpec((TILE, D_MODEL), lambda b, i: (b * nt + i, 0))
    n_tok = B * nt * TILE
    return pl.pallas_call(
        functools.partial(_mixout_kernel, first_tile=first_tile),
        out_shape=(
            jax.ShapeDtypeStruct((B, nt * TILE, D_MODEL), F32),
            jax.ShapeDtypeStruct((n_tok, D_MODEL), F32),
            jax.ShapeDtypeStruct((n_tok, LANE), F32),
            jax.ShapeDtypeStruct((8, LANE), F32),
        ),
        grid=(B, nt),
        in_specs=[own(N_HEADS * V_HEAD), tok(D_CONV), left(D_CONV), right(D_CONV),
                  tok(D_POOL), left(D_POOL), right(D_POOL), tok(D_MODEL),
                  pl.BlockSpec((1, 8, D_MODEL), mod_idx),
                  full(cw), full(cb), full(cg), full(cbeta), full(pw), full(ps), full(wout),
                  full(g1), full(b1), full(wr), full(br), full(tri)],
        out_specs=(own(D_MODEL), flat,
                   pl.BlockSpec((TILE, LANE), lambda b, i: (b * nt + i, 0)),
                   pl.BlockSpec((8, LANE), lambda b, i: (0, 0))),
        scratch_shapes=[pltpu.VMEM((TILE + 2 * HALO, D_CONV), F32),
                        pltpu.VMEM((TILE + 2 * HALO, D_POOL), F32),
                        pltpu.VMEM((8, LANE), F32)],
        compiler_params=pltpu.CompilerParams(
            dimension_semantics=("arbitrary", "arbitrary"), vmem_limit_bytes=VMEM_LIMIT),
        name="mixout",
    )(attn, u, u, u, poolin, poolin, poolin, xa, mods, cw, cb, cg, cbeta, pw, ps, wout, g1, b1,
      wr, br, tri)


def _row_copy(src, src_row, dst, dst_row, sem):
    return pltpu.make_async_copy(src.at[pl.ds(src_row, 1)], dst.at[pl.ds(dst_row, 1)], sem)


def _dispatch_kernel(pos_ref, h2_ref, xs_in_ref, xs_ref, sem):
    del xs_in_ref
    base = pl.program_id(0) * TILE

    def issue(j, c):
        _row_copy(h2_ref, base + lax.rem(j, TILE), xs_ref, pos_ref[0, 0, j], sem).start()
        return c

    lax.fori_loop(0, 2 * TILE, issue, 0)

    def drain(j, c):
        _row_copy(h2_ref, base, xs_ref, 0, sem).wait()
        return c

    lax.fori_loop(0, 2 * TILE, drain, 0)


def _dispatch(pos, h2, n_rows):
    n_tok = h2.shape[0]
    xs0 = jnp.zeros((n_rows, D_MODEL), F32)
    return pl.pallas_call(
        _dispatch_kernel,
        out_shape=jax.ShapeDtypeStruct((n_rows, D_MODEL), F32),
        grid=(n_tok // TILE,),
        in_specs=[pl.BlockSpec((1, 1, 2 * TILE), lambda i: (i, 0, 0), memory_space=pltpu.SMEM),
                  pl.BlockSpec(memory_space=pl.ANY),
                  pl.BlockSpec(memory_space=pl.ANY)],
        out_specs=pl.BlockSpec(memory_space=pl.ANY),
        scratch_shapes=[pltpu.SemaphoreType.DMA],
        input_output_aliases={2: 0},
        compiler_params=pltpu.CompilerParams(
            dimension_semantics=("arbitrary",), has_side_effects=True),
        name="dispatch",
    )(pos, h2, xs0)


def _experts_kernel(te_ref, nv_ref, xs_ref, wg_ref, wu_ref, wd_ref, ys_ref):
    i = pl.program_id(0)

    @pl.when(i < nv_ref[0])
    def _():
        x = xs_ref[...].astype(BF16)
        g = _dot(x, wg_ref[0, 0].astype(BF16))
        u = _dot(x, wu_ref[0, 0].astype(BF16))
        hid = (_silu(g) * u).astype(BF16)
        ys_ref[...] = _dot(hid, wd_ref[0, 0].astype(BF16))

    @pl.when(i >= nv_ref[0])
    def _():
        ys_ref[...] = jnp.zeros_like(ys_ref)


def _experts(tile_expert, n_valid, xs, w_gate, w_up, w_down, layer):
    n_tiles = xs.shape[0] // TILE

    def x_idx(i, te, nv):
        return (jnp.minimum(i, nv[0] - 1), 0)

    def w_idx(i, te, nv):
        return (layer, te[i], 0, 0)

    return pl.pallas_call(
        _experts_kernel,
        out_shape=jax.ShapeDtypeStruct((n_tiles * TILE, D_MODEL), F32),
        grid_spec=pltpu.PrefetchScalarGridSpec(
            num_scalar_prefetch=2,
            grid=(n_tiles,),
            in_specs=[pl.BlockSpec((TILE, D_MODEL), x_idx),
                      pl.BlockSpec((1, 1, D_MODEL, D_EXPERT), w_idx),
                      pl.BlockSpec((1, 1, D_MODEL, D_EXPERT), w_idx),
                      pl.BlockSpec((1, 1, D_EXPERT, D_MODEL), w_idx)],
            out_specs=pl.BlockSpec((TILE, D_MODEL), lambda i, te, nv: (i, 0)),
        ),
        compiler_params=pltpu.CompilerParams(
            dimension_semantics=("arbitrary",), vmem_limit_bytes=VMEM_LIMIT),
        name="experts",
    )(tile_expert, n_valid, xs, w_gate, w_up, w_down)


def _combine_kernel(pos_ref, ys_ref, rinfo_ref, x_ref, mod_ref, g2_ref, b2_ref, o_ref, ybuf, sem):
    def issue(j, c):
        _row_copy(ys_ref, pos_ref[0, 0, j], ybuf, j, sem).start()
        return c

    lax.fori_loop(0, 2 * TILE, issue, 0)

    def drain(j, c):
        _row_copy(ys_ref, 0, ybuf, 0, sem).wait()
        return c

    lax.fori_loop(0, 2 * TILE, drain, 0)

    w1 = rinfo_ref[:, 0:1]
    w2 = rinfo_ref[:, 1:2]
    y = w1 * ybuf[0:TILE, :] + w2 * ybuf[TILE:, :]
    gate2 = mod_ref[0, 5:6, :]
    o_ref[0] = _layer_norm_rows(DEEPNORM_ALPHA * x_ref[0] + gate2 * y, g2_ref[...], b2_ref[...])


def _combine(pos, ys, rinfo, x1, mods, g2, b2, first_tile, out_first_tile):
    B = x1.shape[0]
    nt = N_TILES - first_tile
    nb = mods.shape[0] - 1
    out_len = SEQ_ALL - out_first_tile * TILE

    def mod_idx(b, i):
        return (jnp.where(i + first_tile == 0, nb, b), 0, 0)

    full = lambda a: pl.BlockSpec(a.shape, lambda b, i: (0,) * a.ndim)
    return pl.pallas_call(
        _combine_kernel,
        out_shape=jax.ShapeDtypeStruct((B, out_len, D_MODEL), F32),
        grid=(B, nt),
        in_specs=[pl.BlockSpec((1, 1, 2 * TILE), lambda b, i: (b * nt + i, 0, 0),
                               memory_space=pltpu.SMEM),
                  pl.BlockSpec(memory_space=pl.ANY),
                  pl.BlockSpec((TILE, LANE), lambda b, i: (b * nt + i, 0)),
                  pl.BlockSpec((1, TILE, D_MODEL), lambda b, i: (b, i, 0)),
                  pl.BlockSpec((1, 8, D_MODEL), mod_idx),
                  full(g2), full(b2)],
        out_specs=pl.BlockSpec((1, TILE, D_MODEL),
                               lambda b, i: (b, i + first_tile - out_first_tile, 0)),
        scratch_shapes=[pltpu.VMEM((2 * TILE, D_MODEL), F32), pltpu.SemaphoreType.DMA],
        compiler_params=pltpu.CompilerParams(
            dimension_semantics=("arbitrary", "arbitrary"), vmem_limit_bytes=VMEM_LIMIT),
        name="combine",
    )(pos, ys, rinfo, x1, mods, g2, b2)


def _rot_cols(w):
    w4 = w.reshape(w.shape[:-1] + (2, 2, QK_ROPE // 4))
    return jnp.stack([-w4[..., 1, :], w4[..., 0, :]], axis=-2).reshape(w.shape)


def _rope_tables():
    rows = SEQ // GRID_W
    row = jnp.repeat(jnp.arange(rows), GRID_W)
    col = jnp.tile(jnp.arange(GRID_W), rows)
    d_axis = QK_ROPE // 2
    inv_freq = jnp.power(ROPE_THETA, -jnp.arange(0, d_axis, 2, dtype=F32) / d_axis)

    def axis_angles(p):
        a = p.astype(F32)[:, None] * inv_freq[None, :]
        return jnp.concatenate([a, a], -1)

    ang = jnp.concatenate([axis_angles(row), axis_angles(col)], -1)
    cos = jnp.concatenate([jnp.ones((CTX_LEN, QK_ROPE), F32), jnp.cos(ang)], 0)
    sin = jnp.concatenate([jnp.zeros((CTX_LEN, QK_ROPE), F32), jnp.sin(ang)], 0)
    scale = 1.0 / math.sqrt(QK_HEAD)
    z64 = jnp.zeros((SEQ_ALL, QK_NOPE), F32)
    z32 = jnp.zeros((SEQ_ALL, HEAD_PAD - QK_HEAD), F32)
    cq = jnp.concatenate([jnp.full((SEQ_ALL, QK_NOPE), scale, F32), scale * cos, z32], -1)
    sq = jnp.concatenate([z64, scale * sin, z32], -1)
    ck = jnp.concatenate([z64, cos, z32], -1)
    sk = jnp.concatenate([z64, sin, z32], -1)
    return cq, sq, ck, sk


def _prep_layer(w_in, w_uq, w_ukv, conv_w, pool_w, w_out, w_rg, b_rg, w_re, b_re):
    zc = lambda r, c: jnp.zeros((r, c), F32)
    kr = w_in[:, OFF_KR:OFF_CONV]
    pad_l, pad_r = QK_NOPE, HEAD_PAD - QK_HEAD
    kr_arr = jnp.concatenate([zc(D_MODEL, pad_l), kr, zc(D_MODEL, pad_r)], -1)
    krr_arr = jnp.concatenate([zc(D_MODEL, pad_l), _rot_cols(kr), zc(D_MODEL, pad_r)], -1)
    win_ext = jnp.concatenate(
        [w_in[:, :OFF_KR], kr_arr, krr_arr, w_in[:, OFF_CONV:]], -1).astype(BF16)

    wq3 = w_uq.reshape(Q_RANK, N_HEADS, QK_HEAD)
    nope, rope = wq3[..., :QK_NOPE], wq3[..., QK_NOPE:]
    z3 = lambda c: jnp.zeros((Q_RANK, N_HEADS, c), F32)
    q_arr = jnp.concatenate([nope, rope, z3(pad_r)], -1).reshape(Q_RANK, N_HEADS * HEAD_PAD)
    q_rot = jnp.concatenate([z3(pad_l), _rot_cols(rope), z3(pad_r)], -1).reshape(Q_RANK, N_HEADS * HEAD_PAD)
    wuq_ext = jnp.concatenate([q_arr, q_rot], -1).astype(BF16)

    wkv3 = w_ukv.reshape(KV_RANK, N_HEADS, QK_NOPE + V_HEAD)
    k_arr = jnp.concatenate([wkv3[..., :QK_NOPE], jnp.zeros((KV_RANK, N_HEADS, HEAD_PAD - QK_NOPE), F32)],
                            -1).reshape(KV_RANK, N_HEADS * HEAD_PAD)
    v_arr = wkv3[..., QK_NOPE:].reshape(KV_RANK, N_HEADS * V_HEAD)
    wukv_ext = jnp.concatenate([k_arr, v_arr], -1).astype(BF16)

    cw = jnp.concatenate([conv_w, jnp.zeros((1, D_CONV), F32)], 0)
    gdim = D_POOL // len(POOL_WINDOWS)
    pw = jnp.zeros((D_POOL, D_POOL), F32)
    for g in range(len(POOL_WINDOWS)):
        pw = pw.at[g * gdim:(g + 1) * gdim, g * gdim:(g + 1) * gdim].set(pool_w[g])
    wr = jnp.concatenate([w_re, w_rg, zc(D_MODEL, LANE - N_EXPERTS - N_GROUPS)], -1).astype(BF16)
    br = jnp.concatenate([b_re, b_rg, jnp.zeros((LANE - N_EXPERTS - N_GROUPS,), F32)])[None, :]
    return win_ext, wuq_ext, wukv_ext, cw, pw.astype(BF16), w_out.astype(BF16), wr, br


def _routing_tables(rinfo, totals, n_tiles_max, nt_tok):
    counts = totals[0, :N_EXPERTS].astype(jnp.int32)
    padded = ((counts + TILE - 1) // TILE) * TILE
    ends = jnp.cumsum(padded)
    starts = ends - padded
    e = rinfo[:, 2:4].astype(jnp.int32)
    pos = rinfo[:, 4:6].astype(jnp.int32) + starts[e]
    pos = pos.reshape(nt_tok, TILE, 2).transpose(0, 2, 1).reshape(nt_tok, 1, 2 * TILE)
    n_valid = ends[-1] // TILE
    tile_start = jnp.minimum(jnp.arange(n_tiles_max, dtype=jnp.int32), n_valid - 1) * TILE
    tile_expert = jnp.sum((tile_start[:, None] >= ends[None, :]).astype(jnp.int32), axis=-1)
    return pos, tile_expert.astype(jnp.int32), n_valid.astype(jnp.int32).reshape(1)


def kernel(x, c, ctx, c_ctx, w_ada, b_ada, w_in, g_q, w_uq, g_kv, w_ukv, conv_w, conv_b, conv_ln_g,
           conv_ln_b, pool_w, pool_scale, w_out, ln1_g, ln1_b, w_router_group, b_router_group,
           w_router_expert, b_router_expert, w_gate, w_up, w_down, ln2_g, ln2_b):
    B = x.shape[0]
    L = w_ada.shape[0]
    tabs = _rope_tables()
    tri = jnp.tril(jnp.ones((TILE, TILE), F32), -1).astype(BF16)

    cc = jnp.concatenate([c, c_ctx[None, :], jnp.zeros((16 - B - 1, D_MODEL), F32)], 0)
    ada = _ada(cc, w_ada, b_ada)
    mods_all = ada[:, :B + 1].reshape(L, B + 1, 6, D_MODEL)
    mods_all = jnp.concatenate([mods_all, jnp.zeros((L, B + 1, 2, D_MODEL), F32)], 2)

    xa = jnp.concatenate([ctx, x], axis=1)
    row = lambda a: a[None, :]
    for l in range(L):
        last = l == L - 1
        first_tile = 1 if last else 0
        mods = mods_all[l]
        win_ext, wuq_ext, wukv_ext, cw, pw, wout, wr, br = _prep_layer(
            w_in[l], w_uq[l], w_ukv[l], conv_w[l], pool_w[l], w_out[l], w_router_group[l],
            b_router_group[l], w_router_expert[l], b_router_expert[l])

        q, k, v, u, poolin = _inproj(xa, mods, tabs, win_ext, row(g_q[l]), wuq_ext, row(g_kv[l]),
                                     wukv_ext)
        attn = _attention(q, k, v, first_tile)
        x1, h2, rinfo, totals = _mixout(
            attn, u, poolin, xa, mods, cw, row(conv_b[l]), row(conv_ln_g[l]), row(conv_ln_b[l]),
            pw, row(pool_scale[l]), wout, row(ln1_g[l]), row(ln1_b[l]), wr, br, tri, first_tile)

        nt_tok = h2.shape[0] // TILE
        n_tiles_max = 2 * nt_tok + N_EXPERTS
        pos, tile_expert, n_valid = _routing_tables(rinfo, totals, n_tiles_max, nt_tok)
        xs = _dispatch(pos, h2, n_tiles_max * TILE)
        ys = _experts(tile_expert, n_valid, xs, w_gate, w_up, w_down, l)
        xa = _combine(pos, ys, rinfo, x1, mods, row(ln2_g[l]), row(ln2_b[l]), first_tile,
                      1 if last else 0)
    return xa
```

```python
import functools
import math

import jax
import jax.numpy as jnp
from jax import lax
from jax.experimental import pallas as pl
from jax.experimental.pallas import tpu as pltpu

D_MODEL = 1024
SEQ = 4096
CTX_LEN = 256
SEQ_ALL = CTX_LEN + SEQ
GRID_W = 64
N_HEADS = 8
QK_NOPE = 64
QK_ROPE = 32
QK_HEAD = QK_NOPE + QK_ROPE
V_HEAD = 64
Q_RANK = 384
KV_RANK = 256
D_CONV = 256
CONV_WIDTH = 31
D_POOL = 256
POOL_WINDOWS = (2, 4, 8, 16)
ROPE_THETA = 10000.0
OFF_KV = Q_RANK
OFF_KR = OFF_KV + KV_RANK
OFF_CONV = OFF_KR + QK_ROPE
OFF_POOL = OFF_CONV + 2 * D_CONV
N_GROUPS = 4
EXPERTS_PER_GROUP = 8
N_EXPERTS = 32
D_EXPERT = 256
LN_EPS = 1e-5
RMS_EPS = 1e-6
DEPTH = 2
DEEPNORM_ALPHA = (2 * DEPTH) ** 0.25

LANE = 128
TILE = 256
HALO = 16
N_TILES = SEQ_ALL // TILE
HEAD_PAD = 128
VMEM_LIMIT = 56 * 1024 * 1024

XOFF_Q = 0
XOFF_KV = 384
XOFF_KR = 640
XOFF_KRR = 768
XOFF_CONV = 896
XOFF_POOL = 1408
D_IN_EXT = 1664

BF16 = jnp.bfloat16
F32 = jnp.float32


def _dot(a, b):
    return jnp.dot(a, b, preferred_element_type=F32)


def _silu(x):
    return x * jax.nn.sigmoid(x)


def _layer_norm_rows(x, g, b):
    mu = jnp.mean(x, axis=-1, keepdims=True)
    xc = x - mu
    var = jnp.mean(xc * xc, axis=-1, keepdims=True)
    return xc * lax.rsqrt(var + LN_EPS) * g + b


def _ada_kernel(cc_ref, w_ref, b_ref, o_ref):
    a = _silu(cc_ref[...])
    o_ref[0] = jnp.dot(a, w_ref[0], preferred_element_type=F32,
                       precision=lax.Precision.HIGHEST) + b_ref[0]


def _ada(cc, w_ada, b_ada):
    L = w_ada.shape[0]
    n = w_ada.shape[2]
    bn = 512
    return pl.pallas_call(
        _ada_kernel,
        out_shape=jax.ShapeDtypeStruct((L, cc.shape[0], n), F32),
        grid=(L, n // bn),
        in_specs=[
            pl.BlockSpec(cc.shape, lambda l, j: (0, 0)),
            pl.BlockSpec((1, D_MODEL, bn), lambda l, j: (l, 0, j)),
            pl.BlockSpec((1, 1, bn), lambda l, j: (l, 0, j)),
        ],
        out_specs=pl.BlockSpec((1, cc.shape[0], bn), lambda l, j: (l, 0, j)),
        compiler_params=pltpu.CompilerParams(
            dimension_semantics=("parallel", "parallel"), vmem_limit_bytes=VMEM_LIMIT),
        name="ada",
    )(cc, w_ada, b_ada.reshape(L, 1, n))


def _inproj_kernel(x_ref, mod_ref, cq_ref, sq_ref, ck_ref, sk_ref, win_ref, gq_ref, wuq_ref,
                   gkv_ref, wukv_ref, q_ref, k_ref, v_ref, u_ref, pool_ref):
    x = x_ref[0]
    sh = mod_ref[0, 0:1, :]
    sc = mod_ref[0, 1:2, :]
    h = (x * (1.0 + sc) + sh).astype(BF16)
    p = _dot(h, win_ref[...])

    pq = p[:, XOFF_Q:XOFF_KV]
    qn = pq * lax.rsqrt(jnp.mean(pq * pq, axis=-1, keepdims=True) + RMS_EPS) * gq_ref[...]
    q2 = _dot(qn.astype(BF16), wuq_ref[...])
    cq = cq_ref[...]
    sq = sq_ref[...]
    for hd in range(N_HEADS):
        lo = hd * HEAD_PAD
        qh = q2[:, lo:lo + HEAD_PAD] * cq + q2[:, N_HEADS * HEAD_PAD + lo:N_HEADS * HEAD_PAD + lo + HEAD_PAD] * sq
        q_ref[0, :, lo:lo + HEAD_PAD] = qh.astype(BF16)

    pkv = p[:, XOFF_KV:XOFF_KR]
    kvn = pkv * lax.rsqrt(jnp.mean(pkv * pkv, axis=-1, keepdims=True) + RMS_EPS) * gkv_ref[...]
    kv = _dot(kvn.astype(BF16), wukv_ref[...])
    kr = p[:, XOFF_KR:XOFF_KRR] * ck_ref[...] + p[:, XOFF_KRR:XOFF_CONV] * sk_ref[...]
    for hd in range(N_HEADS):
        lo = hd * HEAD_PAD
        k_ref[0, :, lo:lo + HEAD_PAD] = (kv[:, lo:lo + HEAD_PAD] + kr).astype(BF16)
    v_ref[0] = kv[:, N_HEADS * HEAD_PAD:].astype(BF16)

    a = p[:, XOFF_CONV:XOFF_CONV + D_CONV]
    gt = p[:, XOFF_CONV + D_CONV:XOFF_POOL]
    u_ref[0] = (a * jax.nn.sigmoid(gt)).astype(BF16)
    pool_ref[0] = p[:, XOFF_POOL:]


def _inproj(xa, mods, tabs, win_ext, gq, wuq_ext, gkv, wukv_ext):
    B = xa.shape[0]
    nb = mods.shape[0] - 1

    def mod_idx(b, i):
        return (jnp.where(i == 0, nb, b), 0, 0)

    tab_spec = pl.BlockSpec((TILE, LANE), lambda b, i: (i, 0))
    full = lambda a: pl.BlockSpec(a.shape, lambda b, i: (0,) * a.ndim)
    tok = lambda w: pl.BlockSpec((1, TILE, w), lambda b, i: (b, i, 0))
    return pl.pallas_call(
        _inproj_kernel,
        out_shape=(
            jax.ShapeDtypeStruct((B, SEQ_ALL, N_HEADS * HEAD_PAD), BF16),
            jax.ShapeDtypeStruct((B, SEQ_ALL, N_HEADS * HEAD_PAD), BF16),
            jax.ShapeDtypeStruct((B, SEQ_ALL, N_HEADS * V_HEAD), BF16),
            jax.ShapeDtypeStruct((B, SEQ_ALL, D_CONV), BF16),
            jax.ShapeDtypeStruct((B, SEQ_ALL, D_POOL), F32),
        ),
        grid=(B, N_TILES),
        in_specs=[tok(D_MODEL), pl.BlockSpec((1, 8, D_MODEL), mod_idx),
                  tab_spec, tab_spec, tab_spec, tab_spec,
                  full(win_ext), full(gq), full(wuq_ext), full(gkv), full(wukv_ext)],
        out_specs=(tok(N_HEADS * HEAD_PAD), tok(N_HEADS * HEAD_PAD), tok(N_HEADS * V_HEAD),
                   tok(D_CONV), tok(D_POOL)),
        compiler_params=pltpu.CompilerParams(
            dimension_semantics=("parallel", "parallel"), vmem_limit_bytes=VMEM_LIMIT),
        name="inproj",
    )(xa, mods, *tabs, win_ext, gq, wuq_ext, gkv, wukv_ext)


def _attend(q, k_ref, v_ref, n_keys):
    lane = lax.broadcasted_iota(jnp.int32, (n_keys, 2 * V_HEAD), 1)
    vv = v_ref[0, 0:n_keys, :]
    acc = jnp.zeros((TILE, 2 * V_HEAD), F32)
    for hd in range(2):
        qh = q[:, hd * HEAD_PAD:(hd + 1) * HEAD_PAD]
        kh = k_ref[0, 0:n_keys, hd * HEAD_PAD:(hd + 1) * HEAD_PAD]
        s = lax.dot_general(qh, kh, (((1,), (1,)), ((), ())), preferred_element_type=F32)
        m = jnp.max(s, axis=-1, keepdims=True)
        p = jnp.exp(s - m)
        l = jnp.sum(p, axis=-1, keepdims=True)
        in_head = (lane >= hd * V_HEAD) & (lane < (hd + 1) * V_HEAD)
        vh = jnp.where(in_head, vv, jnp.zeros_like(vv))
        acc = acc + _dot(p.astype(BF16), vh) / l
    return acc


def _attn_kernel(q_ref, k_ref, v_ref, o_ref, *, first_tile):
    q = q_ref[0]
    if first_tile == 0:
        i = pl.program_id(2)

        @pl.when(i == 0)
        def _():
            o_ref[0] = _attend(q, k_ref, v_ref, CTX_LEN).astype(BF16)

        @pl.when(i > 0)
        def _():
            o_ref[0] = _attend(q, k_ref, v_ref, SEQ_ALL).astype(BF16)
    else:
        o_ref[0] = _attend(q, k_ref, v_ref, SEQ_ALL).astype(BF16)


def _attention(q, k, v, first_tile):
    B = q.shape[0]
    n_pairs = N_HEADS // 2
    nq = N_TILES - first_tile
    return pl.pallas_call(
        functools.partial(_attn_kernel, first_tile=first_tile),
        out_shape=jax.ShapeDtypeStruct((B, nq * TILE, N_HEADS * V_HEAD), BF16),
        grid=(B, n_pairs, nq),
        in_specs=[
            pl.BlockSpec((1, TILE, 2 * HEAD_PAD), lambda b, hp, i: (b, i + first_tile, hp)),
            pl.BlockSpec((1, SEQ_ALL, 2 * HEAD_PAD), lambda b, hp, i: (b, 0, hp)),
            pl.BlockSpec((1, SEQ_ALL, 2 * V_HEAD), lambda b, hp, i: (b, 0, hp)),
        ],
        out_specs=pl.BlockSpec((1, TILE, 2 * V_HEAD), lambda b, hp, i: (b, i, hp)),
        compiler_params=pltpu.CompilerParams(
            dimension_semantics=("parallel", "parallel", "parallel"), vmem_limit_bytes=VMEM_LIMIT),
        name="attn",
    )(q, k, v)


def _mixout_kernel(attn_ref, u_ref, ul_ref, ur_ref, pm_ref, pl_ref, pr_ref, x_ref, mod_ref,
                   cw_ref, cb_ref, cg_ref, cbeta_ref, pw_ref, ps_ref, wout_ref, g1_ref, b1_ref,
                   wr_ref, br_ref, tri_ref,
                   x1_ref, h2_ref, rinfo_ref, tot_ref,
                   winu, winp, cnt, *, first_tile):
    b = pl.program_id(0)
    i = pl.program_id(1) + first_tile
    left_ok = i >= 2
    right_ok = (i >= 1) & (i <= N_TILES - 2)

    @pl.when((pl.program_id(0) == 0) & (pl.program_id(1) == 0))
    def _():
        cnt[...] = jnp.zeros_like(cnt)

    zu = jnp.zeros((HALO, D_CONV), F32)
    winu[0:HALO, :] = jnp.where(left_ok, ul_ref[0].astype(F32), zu)
    winu[HALO:HALO + TILE, :] = u_ref[0].astype(F32)
    winu[HALO + TILE:, :] = jnp.where(right_ok, ur_ref[0].astype(F32), zu)
    winp[0:HALO, :] = jnp.where(left_ok, pl_ref[0], zu)
    winp[HALO:HALO + TILE, :] = pm_ref[0]
    winp[HALO + TILE:, :] = jnp.where(right_ok, pr_ref[0], zu)

    acc = jnp.zeros((TILE, D_CONV), F32)
    for j in range(CONV_WIDTH):
        off = HALO - CONV_WIDTH // 2 + j
        acc = acc + winu[off:off + TILE, :] * cw_ref[j:j + 1, :]
    conv = _silu(_layer_norm_rows(acc + cb_ref[...], cg_ref[...], cbeta_ref[...]))

    def pw(o):
        return winp[HALO + o:HALO + o + TILE, :]

    centre = pw(0)
    s2 = pw(-1) + centre
    s4 = s2 + pw(-2) + pw(1)
    s8 = s4 + pw(-4) + pw(-3) + pw(2) + pw(3)
    s16 = s8 + pw(-8) + pw(-7) + pw(-6) + pw(-5) + pw(4) + pw(5) + pw(6) + pw(7)
    row = lax.broadcasted_iota(jnp.int32, (TILE, 1), 0)
    seg_len = jnp.where(i == 0, CTX_LEN, SEQ)
    t = jnp.where(i == 0, 0, (i - 1) * TILE) + row

    def mean(sw, w):
        hi = jnp.minimum(t + (w // 2 - 1), seg_len - 1)
        lo = jnp.maximum(t - w // 2, 0)
        return sw / (hi - lo + 1).astype(F32)

    lane_p = lax.broadcasted_iota(jnp.int32, (1, D_POOL), 1)
    gdim = D_POOL // len(POOL_WINDOWS)
    mixed = jnp.where(lane_p < gdim, mean(s2, 2),
                      jnp.where(lane_p < 2 * gdim, mean(s4, 4),
                                jnp.where(lane_p < 3 * gdim, mean(s8, 8), mean(s16, 16)))) - centre
    pool = _dot(mixed.astype(BF16), pw_ref[...]) * ps_ref[...]

    d_attn = N_HEADS * V_HEAD
    mix = (_dot(attn_ref[0], wout_ref[0:d_attn, :])
           + _dot(conv.astype(BF16), wout_ref[d_attn:d_attn + D_CONV, :])
           + _dot(pool.astype(BF16), wout_ref[d_attn + D_CONV:, :]))
    gate1 = mod_ref[0, 2:3, :]
    x1 = _layer_norm_rows(DEEPNORM_ALPHA * x_ref[0] + gate1 * mix, g1_ref[...], b1_ref[...])
    x1_ref[0] = x1

    h2 = x1 * (1.0 + mod_ref[0, 4:5, :]) + mod_ref[0, 3:4, :]
    h2_ref[...] = h2
    logits = _dot(h2.astype(BF16), wr_ref[...]) + br_ref[...]
    lane = lax.broadcasted_iota(jnp.int32, (TILE, LANE), 1)
    neg = jnp.float32(-jnp.inf)
    big = jnp.int32(1 << 20)
    is_g = (lane >= N_EXPERTS) & (lane < N_EXPERTS + N_GROUPS)
    gl = jnp.where(is_g, logits, neg)
    gmax = jnp.max(gl, axis=-1, keepdims=True)
    g_sel = jnp.min(jnp.where(gl == gmax, lane - N_EXPERTS, big), axis=-1, keepdims=True)
    p_sel = 1.0 / jnp.sum(jnp.exp(gl - gmax), axis=-1, keepdims=True)
    in_grp = (lane >= g_sel * EXPERTS_PER_GROUP) & (lane < (g_sel + 1) * EXPERTS_PER_GROUP)
    el = jnp.where(in_grp, logits, neg)
    v1 = jnp.max(el, axis=-1, keepdims=True)
    i1 = jnp.min(jnp.where(el == v1, lane, big), axis=-1, keepdims=True)
    el2 = jnp.where(lane == i1, neg, el)
    v2 = jnp.max(el2, axis=-1, keepdims=True)
    i2 = jnp.min(jnp.where(el2 == v2, lane, big), axis=-1, keepdims=True)
    e2 = jnp.exp(v2 - v1)
    w1 = p_sel / (1.0 + e2)
    w2 = p_sel * e2 / (1.0 + e2)

    oh1 = (lane == i1).astype(F32)
    oh2 = (lane == i2).astype(F32)
    both = oh1 + oh2
    before = _dot(tri_ref[...], both.astype(BF16)) + cnt[0:1, :]
    r1 = jnp.sum(oh1 * before, axis=-1, keepdims=True)
    r2 = jnp.sum(oh2 * before, axis=-1, keepdims=True)
    cnt[...] = cnt[...] + jnp.sum(both, axis=0, keepdims=True)
    tot_ref[...] = cnt[...]
    fields = (w1, w2, i1.astype(F32), i2.astype(F32), r1, r2)
    rinfo = jnp.zeros((TILE, LANE), F32)
    for n, f in enumerate(fields):
        rinfo = jnp.where(lane == n, f, rinfo)
    rinfo_ref[...] = rinfo


def _mixout(attn, u, poolin, xa, mods, cw, cb, cg, cbeta, pw, ps, wout, g1, b1, wr, br, tri,
            first_tile):
    B = xa.shape[0]
    nt = N_TILES - first_tile
    nb = mods.shape[0] - 1
    hb = TILE // HALO
    n_hb = SEQ_ALL // HALO

    def mod_idx(b, i):
        return (jnp.where(i + first_tile == 0, nb, b), 0, 0)

    tok = lambda w: pl.BlockSpec((1, TILE, w), lambda b, i: (b, i + first_tile, 0))
    own = lambda w: pl.BlockSpec((1, TILE, w), lambda b, i: (b, i, 0))
    left = lambda w: pl.BlockSpec(
        (1, HALO, w), lambda b, i: (b, jnp.maximum((i + first_tile) * hb - 1, 0), 0))
    right = lambda w: pl.BlockSpec(
        (1, HALO, w), lambda b, i: (b, jnp.minimum((i + first_tile + 1) * hb, n_hb - 1), 0))
    full = lambda a: pl.BlockSpec(a.shape, lambda b, i: (0,) * a.ndim)
    flat = pl.BlockSpec((TILE, D_MODEL), lambda b, i: (b * nt + i, 0))
    n_tok = B * nt * TILE
    return pl.pallas_call(
        functools.partial(_mixout_kernel, first_tile=first_tile),
        out_shape=(
            jax.ShapeDtypeStruct((B, nt * TILE, D_MODEL), F32),
            jax.ShapeDtypeStruct((n_tok, D_MODEL), F32),
            jax.ShapeDtypeStruct((n_tok, LANE), F32),
            jax.ShapeDtypeStruct((8, LANE), F32),
        ),
        grid=(B, nt),
        in_specs=[own(N_HEADS * V_HEAD), tok(D_CONV), left(D_CONV), right(D_CONV),
                  tok(D_POOL), left(D_POOL), right(D_POOL), tok(D_MODEL),
                  pl.BlockSpec((1, 8, D_MODEL), mod_idx),
                  full(cw), full(cb), full(cg), full(cbeta), full(pw), full(ps), full(wout),
                  full(g1), full(b1), full(wr), full(br), full(tri)],
        out_specs=(own(D_MODEL), flat,
                   pl.BlockSpec((TILE, LANE), lambda b, i: (b * nt + i, 0)),
                   pl.BlockSpec((8, LANE), lambda b, i: (0, 0))),
        scratch_shapes=[pltpu.VMEM((TILE + 2 * HALO, D_CONV), F32),
                        pltpu.VMEM((TILE + 2 * HALO, D_POOL), F32),
                        pltpu.VMEM((8, LANE), F32)],
        compiler_params=pltpu.CompilerParams(
            dimension_semantics=("arbitrary", "arbitrary"), vmem_limit_bytes=VMEM_LIMIT),
        name="mixout",
    )(attn, u, u, u, poolin, poolin, poolin, xa, mods, cw, cb, cg, cbeta, pw, ps, wout, g1, b1,
      wr, br, tri)


def _row_copy(src, src_row, dst, dst_row, sem):
    return pltpu.make_async_copy(src.at[pl.ds(src_row, 1)], dst.at[pl.ds(dst_row, 1)], sem)


def _dispatch_kernel(pos_ref, h2_ref, xs_in_ref, xs_ref, sem):
    del xs_in_ref

    def issue(j, c):
        _row_copy(h2_ref, lax.rem(j, TILE), xs_ref, pos_ref[0, 0, j], sem).start()
        return c

    lax.fori_loop(0, 2 * TILE, issue, 0)

    def drain(j, c):
        _row_copy(h2_ref, 0, xs_ref, 0, sem).wait()
        return c

    lax.fori_loop(0, 2 * TILE, drain, 0)


def _dispatch(pos, h2, n_rows):
    n_tok = h2.shape[0]
    xs0 = jnp.zeros((n_rows, D_MODEL), F32)
    return pl.pallas_call(
        _dispatch_kernel,
        out_shape=jax.ShapeDtypeStruct((n_rows, D_MODEL), F32),
        grid=(n_tok // TILE,),
        in_specs=[pl.BlockSpec((1, 1, 2 * TILE), lambda i: (i, 0, 0), memory_space=pltpu.SMEM),
                  pl.BlockSpec((TILE, D_MODEL), lambda i: (i, 0)),
                  pl.BlockSpec(memory_space=pl.ANY)],
        out_specs=pl.BlockSpec(memory_space=pl.ANY),
        scratch_shapes=[pltpu.SemaphoreType.DMA],
        input_output_aliases={2: 0},
        compiler_params=pltpu.CompilerParams(
            dimension_semantics=("arbitrary",), has_side_effects=True),
        name="dispatch",
    )(pos, h2, xs0)


def _experts_kernel(te_ref, nv_ref, xs_ref, wg_ref, wu_ref, wd_ref, ys_ref):
    i = pl.program_id(0)

    @pl.when(i < nv_ref[0])
    def _():
        x = xs_ref[...].astype(BF16)
        g = _dot(x, wg_ref[0, 0].astype(BF16))
        u = _dot(x, wu_ref[0, 0].astype(BF16))
        hid = (_silu(g) * u).astype(BF16)
        ys_ref[...] = _dot(hid, wd_ref[0, 0].astype(BF16))

    @pl.when(i >= nv_ref[0])
    def _():
        ys_ref[...] = jnp.zeros_like(ys_ref)


def _experts(tile_expert, n_valid, xs, w_gate, w_up, w_down, layer):
    n_tiles = xs.shape[0] // TILE

    def x_idx(i, te, nv):
        return (jnp.minimum(i, nv[0] - 1), 0)

    def w_idx(i, te, nv):
        return (layer, te[i], 0, 0)

    return pl.pallas_call(
        _experts_kernel,
        out_shape=jax.ShapeDtypeStruct((n_tiles * TILE, D_MODEL), F32),
        grid_spec=pltpu.PrefetchScalarGridSpec(
            num_scalar_prefetch=2,
            grid=(n_tiles,),
            in_specs=[pl.BlockSpec((TILE, D_MODEL), x_idx),
                      pl.BlockSpec((1, 1, D_MODEL, D_EXPERT), w_idx),
                      pl.BlockSpec((1, 1, D_MODEL, D_EXPERT), w_idx),
                      pl.BlockSpec((1, 1, D_EXPERT, D_MODEL), w_idx)],
            out_specs=pl.BlockSpec((TILE, D_MODEL), lambda i, te, nv: (i, 0)),
        ),
        compiler_params=pltpu.CompilerParams(
            dimension_semantics=("arbitrary",), vmem_limit_bytes=VMEM_LIMIT),
        name="experts",
    )(tile_expert, n_valid, xs, w_gate, w_up, w_down)


def _combine_kernel(pos_ref, ys_ref, rinfo_ref, x_ref, mod_ref, g2_ref, b2_ref, o_ref, ybuf, sem):
    def issue(j, c):
        _row_copy(ys_ref, pos_ref[0, 0, j], ybuf, j, sem).start()
        return c

    lax.fori_loop(0, 2 * TILE, issue, 0)

    def drain(j, c):
        _row_copy(ys_ref, 0, ybuf, 0, sem).wait()
        return c

    lax.fori_loop(0, 2 * TILE, drain, 0)

    w1 = rinfo_ref[:, 0:1]
    w2 = rinfo_ref[:, 1:2]
    y = w1 * ybuf[0:TILE, :] + w2 * ybuf[TILE:, :]
    gate2 = mod_ref[0, 5:6, :]
    o_ref[0] = _layer_norm_rows(DEEPNORM_ALPHA * x_ref[0] + gate2 * y, g2_ref[...], b2_ref[...])


def _combine(pos, ys, rinfo, x1, mods, g2, b2, first_tile, out_first_tile):
    B = x1.shape[0]
    nt = N_TILES - first_tile
    nb = mods.shape[0] - 1
    out_len = SEQ_ALL - out_first_tile * TILE

    def mod_idx(b, i):
        return (jnp.where(i + first_tile == 0, nb, b), 0, 0)

    full = lambda a: pl.BlockSpec(a.shape, lambda b, i: (0,) * a.ndim)
    return pl.pallas_call(
        _combine_kernel,
        out_shape=jax.ShapeDtypeStruct((B, out_len, D_MODEL), F32),
        grid=(B, nt),
        in_specs=[pl.BlockSpec((1, 1, 2 * TILE), lambda b, i: (b * nt + i, 0, 0),
                               memory_space=pltpu.SMEM),
                  pl.BlockSpec(memory_space=pl.ANY),
                  pl.BlockSpec((TILE, LANE), lambda b, i: (b * nt + i, 0)),
                  pl.BlockSpec((1, TILE, D_MODEL), lambda b, i: (b, i, 0)),
                  pl.BlockSpec((1, 8, D_MODEL), mod_idx),
                  full(g2), full(b2)],
        out_specs=pl.BlockSpec((1, TILE, D_MODEL),
                               lambda b, i: (b, i + first_tile - out_first_tile, 0)),
        scratch_shapes=[pltpu.VMEM((2 * TILE, D_MODEL), F32), pltpu.SemaphoreType.DMA],
        compiler_params=pltpu.CompilerParams(
            dimension_semantics=("arbitrary", "arbitrary"), vmem_limit_bytes=VMEM_LIMIT),
        name="combine",
    )(pos, ys, rinfo, x1, mods, g2, b2)


def _rot_cols(w):
    w4 = w.reshape(w.shape[:-1] + (2, 2, QK_ROPE // 4))
    return jnp.stack([-w4[..., 1, :], w4[..., 0, :]], axis=-2).reshape(w.shape)


def _rope_tables():
    rows = SEQ // GRID_W
    row = jnp.repeat(jnp.arange(rows), GRID_W)
    col = jnp.tile(jnp.arange(GRID_W), rows)
    d_axis = QK_ROPE // 2
    inv_freq = jnp.power(ROPE_THETA, -jnp.arange(0, d_axis, 2, dtype=F32) / d_axis)

    def axis_angles(p):
        a = p.astype(F32)[:, None] * inv_freq[None, :]
        return jnp.concatenate([a, a], -1)

    ang = jnp.concatenate([axis_angles(row), axis_angles(col)], -1)
    cos = jnp.concatenate([jnp.ones((CTX_LEN, QK_ROPE), F32), jnp.cos(ang)], 0)
    sin = jnp.concatenate([jnp.zeros((CTX_LEN, QK_ROPE), F32), jnp.sin(ang)], 0)
    scale = 1.0 / math.sqrt(QK_HEAD)
    z64 = jnp.zeros((SEQ_ALL, QK_NOPE), F32)
    z32 = jnp.zeros((SEQ_ALL, HEAD_PAD - QK_HEAD), F32)
    cq = jnp.concatenate([jnp.full((SEQ_ALL, QK_NOPE), scale, F32), scale * cos, z32], -1)
    sq = jnp.concatenate([z64, scale * sin, z32], -1)
    ck = jnp.concatenate([z64, cos, z32], -1)
    sk = jnp.concatenate([z64, sin, z32], -1)
    return cq, sq, ck, sk


def _prep_layer(w_in, w_uq, w_ukv, conv_w, pool_w, w_out, w_rg, b_rg, w_re, b_re):
    zc = lambda r, c: jnp.zeros((r, c), F32)
    kr = w_in[:, OFF_KR:OFF_CONV]
    pad_l, pad_r = QK_NOPE, HEAD_PAD - QK_HEAD
    kr_arr = jnp.concatenate([zc(D_MODEL, pad_l), kr, zc(D_MODEL, pad_r)], -1)
    krr_arr = jnp.concatenate([zc(D_MODEL, pad_l), _rot_cols(kr), zc(D_MODEL, pad_r)], -1)
    win_ext = jnp.concatenate(
        [w_in[:, :OFF_KR], kr_arr, krr_arr, w_in[:, OFF_CONV:]], -1).astype(BF16)

    wq3 = w_uq.reshape(Q_RANK, N_HEADS, QK_HEAD)
    nope, rope = wq3[..., :QK_NOPE], wq3[..., QK_NOPE:]
    z3 = lambda c: jnp.zeros((Q_RANK, N_HEADS, c), F32)
    q_arr = jnp.concatenate([nope, rope, z3(pad_r)], -1).reshape(Q_RANK, N_HEADS * HEAD_PAD)
    q_rot = jnp.concatenate([z3(pad_l), _rot_cols(rope), z3(pad_r)], -1).reshape(Q_RANK, N_HEADS * HEAD_PAD)
    wuq_ext = jnp.concatenate([q_arr, q_rot], -1).astype(BF16)

    wkv3 = w_ukv.reshape(KV_RANK, N_HEADS, QK_NOPE + V_HEAD)
    k_arr = jnp.concatenate([wkv3[..., :QK_NOPE], jnp.zeros((KV_RANK, N_HEADS, HEAD_PAD - QK_NOPE), F32)],
                            -1).reshape(KV_RANK, N_HEADS * HEAD_PAD)
    v_arr = wkv3[..., QK_NOPE:].reshape(KV_RANK, N_HEADS * V_HEAD)
    wukv_ext = jnp.concatenate([k_arr, v_arr], -1).astype(BF16)

    cw = jnp.concatenate([conv_w, jnp.zeros((1, D_CONV), F32)], 0)
    gdim = D_POOL // len(POOL_WINDOWS)
    pw = jnp.zeros((D_POOL, D_POOL), F32)
    for g in range(len(POOL_WINDOWS)):
        pw = pw.at[g * gdim:(g + 1) * gdim, g * gdim:(g + 1) * gdim].set(pool_w[g])
    wr = jnp.concatenate([w_re, w_rg, zc(D_MODEL, LANE - N_EXPERTS - N_GROUPS)], -1).astype(BF16)
    br = jnp.concatenate([b_re, b_rg, jnp.zeros((LANE - N_EXPERTS - N_GROUPS,), F32)])[None, :]
    return win_ext, wuq_ext, wukv_ext, cw, pw.astype(BF16), w_out.astype(BF16), wr, br


def _routing_tables(rinfo, totals, n_tiles_max, nt_tok):
    counts = totals[0, :N_EXPERTS].astype(jnp.int32)
    padded = ((counts + TILE - 1) // TILE) * TILE
    ends = jnp.cumsum(padded)
    starts = ends - padded
    e = rinfo[:, 2:4].astype(jnp.int32)
    pos = rinfo[:, 4:6].astype(jnp.int32) + starts[e]
    pos = pos.reshape(nt_tok, TILE, 2).transpose(0, 2, 1).reshape(nt_tok, 1, 2 * TILE)
    n_valid = ends[-1] // TILE
    tile_start = jnp.minimum(jnp.arange(n_tiles_max, dtype=jnp.int32), n_valid - 1) * TILE
    tile_expert = jnp.sum((tile_start[:, None] >= ends[None, :]).astype(jnp.int32), axis=-1)
    return pos, tile_expert.astype(jnp.int32), n_valid.astype(jnp.int32).reshape(1)


def kernel(x, c, ctx, c_ctx, w_ada, b_ada, w_in, g_q, w_uq, g_kv, w_ukv, conv_w, conv_b, conv_ln_g,
           conv_ln_b, pool_w, pool_scale, w_out, ln1_g, ln1_b, w_router_group, b_router_group,
           w_router_expert, b_router_expert, w_gate, w_up, w_down, ln2_g, ln2_b):
    B = x.shape[0]
    L = w_ada.shape[0]
    tabs = _rope_tables()
    tri = jnp.tril(jnp.ones((TILE, TILE), F32), -1).astype(BF16)

    cc = jnp.concatenate([c, c_ctx[None, :], jnp.zeros((16 - B - 1, D_MODEL), F32)], 0)
    ada = _ada(cc, w_ada, b_ada)
    mods_all = ada[:, :B + 1].reshape(L, B + 1, 6, D_MODEL)
    mods_all = jnp.concatenate([mods_all, jnp.zeros((L, B + 1, 2, D_MODEL), F32)], 2)

    xa = jnp.concatenate([ctx, x], axis=1)
    row = lambda a: a[None, :]
    for l in range(L):
        last = l == L - 1
        first_tile = 1 if last else 0
        mods = mods_all[l]
        win_ext, wuq_ext, wukv_ext, cw, pw, wout, wr, br = _prep_layer(
            w_in[l], w_uq[l], w_ukv[l], conv_w[l], pool_w[l], w_out[l], w_router_group[l],
            b_router_group[l], w_router_expert[l], b_router_expert[l])

        q, k, v, u, poolin = _inproj(xa, mods, tabs, win_ext, row(g_q[l]), wuq_ext, row(g_kv[l]),
                                     wukv_ext)
        attn = _attention(q, k, v, first_tile)
        x1, h2, rinfo, totals = _mixout(
            attn, u, poolin, xa, mods, cw, row(conv_b[l]), row(conv_ln_g[l]), row(conv_ln_b[l]),
            pw, row(pool_scale[l]), wout, row(ln1_g[l]), row(ln1_b[l]), wr, br, tri, first_tile)

        nt_tok = h2.shape[0] // TILE
        n_tiles_max = 2 * nt_tok + N_EXPERTS
        pos, tile_expert, n_valid = _routing_tables(rinfo, totals, n_tiles_max, nt_tok)
        xs = _dispatch(pos, h2, n_tiles_max * TILE)
        ys = _experts(tile_expert, n_valid, xs, w_gate, w_up, w_down, l)
        xa = _combine(pos, ys, rinfo, x1, mods, row(ln2_g[l]), row(ln2_b[l]), first_tile,
                      1 if last else 0)
    return xa
```

```python
import functools
import math

import jax
import jax.numpy as jnp
from jax import lax
from jax.experimental import pallas as pl
from jax.experimental.pallas import tpu as pltpu

D_MODEL = 1024
SEQ = 4096
CTX_LEN = 256
SEQ_ALL = CTX_LEN + SEQ
GRID_W = 64
N_HEADS = 8
QK_NOPE = 64
QK_ROPE = 32
QK_HEAD = QK_NOPE + QK_ROPE
V_HEAD = 64
Q_RANK = 384
KV_RANK = 256
D_CONV = 256
CONV_WIDTH = 31
D_POOL = 256
POOL_WINDOWS = (2, 4, 8, 16)
ROPE_THETA = 10000.0
OFF_KV = Q_RANK
OFF_KR = OFF_KV + KV_RANK
OFF_CONV = OFF_KR + QK_ROPE
OFF_POOL = OFF_CONV + 2 * D_CONV
N_GROUPS = 4
EXPERTS_PER_GROUP = 8
N_EXPERTS = 32
D_EXPERT = 256
LN_EPS = 1e-5
RMS_EPS = 1e-6
DEPTH = 2
DEEPNORM_ALPHA = (2 * DEPTH) ** 0.25

LANE = 128
SUBLANE = 8
TILE = 256
HALO = 16
N_TILES = SEQ_ALL // TILE
HEAD_PAD = 128
KEY_CHUNK = 256
VMEM_LIMIT = 56 * 1024 * 1024

CHUNK = 2 * SUBLANE
MAX_USED_CHUNKS = (2 * TILE + (CHUNK - 1) * N_EXPERTS) // CHUNK
LOCAL_ROWS = 1024
LOCAL_CHUNKS = LOCAL_ROWS // CHUNK
assert MAX_USED_CHUNKS < LOCAL_CHUNKS <= LANE
TILE_CHUNKS = TILE // CHUNK

XOFF_Q = 0
XOFF_KV = 384
XOFF_KR = 640
XOFF_KRR = 768
XOFF_CONV = 896
XOFF_POOL = 1408
D_IN_EXT = 1664

BF16 = jnp.bfloat16
F32 = jnp.float32
I32 = jnp.int32


def _dot(a, b):
    return jnp.dot(a, b, preferred_element_type=F32)


def _silu(x):
    return x * jax.nn.sigmoid(x)


def _layer_norm_rows(x, g, b):
    mu = jnp.mean(x, axis=-1, keepdims=True)
    xc = x - mu
    var = jnp.mean(xc * xc, axis=-1, keepdims=True)
    return xc * lax.rsqrt(var + LN_EPS) * g + b


def _ada_kernel(cc_ref, w_ref, b_ref, o_ref):
    a = _silu(cc_ref[...])
    o_ref[0] = jnp.dot(a, w_ref[0], preferred_element_type=F32,
                       precision=lax.Precision.HIGHEST) + b_ref[0]


def _ada(cc, w_ada, b_ada):
    L = w_ada.shape[0]
    n = w_ada.shape[2]
    bn = 512
    return pl.pallas_call(
        _ada_kernel,
        out_shape=jax.ShapeDtypeStruct((L, cc.shape[0], n), F32),
        grid=(L, n // bn),
        in_specs=[
            pl.BlockSpec(cc.shape, lambda l, j: (0, 0)),
            pl.BlockSpec((1, D_MODEL, bn), lambda l, j: (l, 0, j)),
            pl.BlockSpec((1, 1, bn), lambda l, j: (l, 0, j)),
        ],
        out_specs=pl.BlockSpec((1, cc.shape[0], bn), lambda l, j: (l, 0, j)),
        compiler_params=pltpu.CompilerParams(
            dimension_semantics=("parallel", "parallel"), vmem_limit_bytes=VMEM_LIMIT),
        name="ada",
    )(cc, w_ada, b_ada.reshape(L, 1, n))


def _inproj_kernel(x_ref, mod_ref, cq_ref, sq_ref, ck_ref, sk_ref, win_ref, gq_ref, wuq_ref,
                   gkv_ref, wuk_ref, wvt_ref, q_ref, k_ref, vt_ref, u_ref, pool_ref):
    x = x_ref[0]
    sh = mod_ref[0, 0:1, :]
    sc = mod_ref[0, 1:2, :]
    h = (x * (1.0 + sc) + sh).astype(BF16)
    p = _dot(h, win_ref[...])

    pq = p[:, XOFF_Q:XOFF_KV]
    qn = pq * lax.rsqrt(jnp.mean(pq * pq, axis=-1, keepdims=True) + RMS_EPS) * gq_ref[...]
    q2 = _dot(qn.astype(BF16), wuq_ref[...])
    cq = cq_ref[...]
    sq = sq_ref[...]
    nq = N_HEADS * HEAD_PAD
    for hd in range(N_HEADS):
        lo = hd * HEAD_PAD
        qh = q2[:, lo:lo + HEAD_PAD] * cq + q2[:, nq + lo:nq + lo + HEAD_PAD] * sq
        q_ref[0, :, lo:lo + HEAD_PAD] = qh.astype(BF16)

    pkv = p[:, XOFF_KV:XOFF_KR]
    kvn = pkv * lax.rsqrt(jnp.mean(pkv * pkv, axis=-1, keepdims=True) + RMS_EPS) * gkv_ref[...]
    kvn = kvn.astype(BF16)
    kn = _dot(kvn, wuk_ref[...])
    kr = p[:, XOFF_KR:XOFF_KRR] * ck_ref[...] + p[:, XOFF_KRR:XOFF_CONV] * sk_ref[...]
    for hd in range(N_HEADS):
        lo = hd * HEAD_PAD
        k_ref[0, :, lo:lo + HEAD_PAD] = (kn[:, lo:lo + HEAD_PAD] + kr).astype(BF16)
    vt = lax.dot_general(wvt_ref[...], kvn, (((1,), (1,)), ((), ())), preferred_element_type=F32)
    vrow = lax.broadcasted_iota(I32, (N_HEADS * HEAD_PAD, TILE), 0)
    vt = jnp.where((vrow & (HEAD_PAD - 1)) == V_HEAD, 1.0, vt).astype(BF16)
    for hd in range(N_HEADS):
        vt_ref[0, hd] = vt[hd * HEAD_PAD:(hd + 1) * HEAD_PAD, :]

    a = p[:, XOFF_CONV:XOFF_CONV + D_CONV]
    gt = p[:, XOFF_CONV + D_CONV:XOFF_POOL]
    u_ref[0] = (a * jax.nn.sigmoid(gt)).astype(BF16)
    pool_ref[0] = p[:, XOFF_POOL:]


def _inproj(xa, mods, tabs, win_ext, gq, wuq_ext, gkv, wuk, wvt):
    B = xa.shape[0]
    nb = mods.shape[0] - 1

    def mod_idx(b, i):
        return (jnp.where(i == 0, nb, b), 0, 0)

    tab_spec = pl.BlockSpec((TILE, LANE), lambda b, i: (i, 0))
    full = lambda a: pl.BlockSpec(a.shape, lambda b, i: (0,) * a.ndim)
    tok = lambda w: pl.BlockSpec((1, TILE, w), lambda b, i: (b, i, 0))
    hw = N_HEADS * HEAD_PAD
    return pl.pallas_call(
        _inproj_kernel,
        out_shape=(
            jax.ShapeDtypeStruct((B, SEQ_ALL, hw), BF16),
            jax.ShapeDtypeStruct((B, SEQ_ALL, hw), BF16),
            jax.ShapeDtypeStruct((B, N_HEADS, HEAD_PAD, SEQ_ALL), BF16),
            jax.ShapeDtypeStruct((B, SEQ_ALL, D_CONV), BF16),
            jax.ShapeDtypeStruct((B, SEQ_ALL, D_POOL), F32),
        ),
        grid=(B, N_TILES),
        in_specs=[tok(D_MODEL), pl.BlockSpec((1, 8, D_MODEL), mod_idx),
                  tab_spec, tab_spec, tab_spec, tab_spec,
                  full(win_ext), full(gq), full(wuq_ext), full(gkv), full(wuk), full(wvt)],
        out_specs=(tok(hw), tok(hw),
                   pl.BlockSpec((1, N_HEADS, HEAD_PAD, TILE), lambda b, i: (b, 0, 0, i)),
                   tok(D_CONV), tok(D_POOL)),
        compiler_params=pltpu.CompilerParams(
            dimension_semantics=("parallel", "parallel"), vmem_limit_bytes=VMEM_LIMIT),
        name="inproj",
    )(xa, mods, *tabs, win_ext, gq, wuq_ext, gkv, wuk, wvt)


def _scores_t(q, k_ref, n_keys):
    lane = lax.broadcasted_iota(I32, (TILE, 2 * HEAD_PAD), 1)
    qbd = jnp.concatenate([jnp.where(lane < HEAD_PAD, q, jnp.zeros_like(q)),
                           jnp.where(lane >= HEAD_PAD, q, jnp.zeros_like(q))], axis=0)
    return lax.dot_general(k_ref[0, 0:n_keys, :], qbd, (((1,), (1,)), ((), ())),
                           preferred_element_type=F32)


def _softmax_values_t(st, vt_ref, p_ref, n_keys):
    m = jnp.max(st, axis=0, keepdims=True)
    for kc in range(0, n_keys, KEY_CHUNK):
        p_ref[kc:kc + KEY_CHUNK, :] = jnp.exp2(st[kc:kc + KEY_CHUNK, :] - m).astype(BF16)
    outs = []
    for hd in range(2):
        ot = _dot(vt_ref[0, hd, :, 0:n_keys], p_ref[0:n_keys, hd * TILE:(hd + 1) * TILE])
        outs.append(ot[0:V_HEAD, :] / ot[V_HEAD:V_HEAD + 1, :])
    return jnp.concatenate(outs, axis=0).T.astype(BF16)


def _attn_kernel(qa_ref, qb_ref, k_ref, vt_ref, o_ref, pa_ref, pb_ref):
    sa = _scores_t(qa_ref[0], k_ref, SEQ_ALL)
    sb = _scores_t(qb_ref[0], k_ref, SEQ_ALL)
    o_ref[0, 0:TILE, :] = _softmax_values_t(sa, vt_ref, pa_ref, SEQ_ALL)
    o_ref[0, TILE:, :] = _softmax_values_t(sb, vt_ref, pb_ref, SEQ_ALL)


def _attention(q, k, v):
    B = q.shape[0]
    n_pairs = N_HEADS // 2
    pw = 2 * HEAD_PAD
    return pl.pallas_call(
        _attn_kernel,
        out_shape=jax.ShapeDtypeStruct((B, SEQ, N_HEADS * V_HEAD), BF16),
        grid=(B, n_pairs, SEQ // (2 * TILE)),
        in_specs=[
            pl.BlockSpec((1, TILE, pw), lambda b, hp, i: (b, 2 * i + 1, hp)),
            pl.BlockSpec((1, TILE, pw), lambda b, hp, i: (b, 2 * i + 2, hp)),
            pl.BlockSpec((1, SEQ_ALL, pw), lambda b, hp, i: (b, 0, hp)),
            pl.BlockSpec((1, 2, HEAD_PAD, SEQ_ALL), lambda b, hp, i: (b, hp, 0, 0)),
        ],
        out_specs=pl.BlockSpec((1, 2 * TILE, 2 * V_HEAD), lambda b, hp, i: (b, i, hp)),
        scratch_shapes=[pltpu.VMEM((SEQ_ALL, 2 * TILE), BF16), pltpu.VMEM((SEQ_ALL, 2 * TILE), BF16)],
        compiler_params=pltpu.CompilerParams(
            dimension_semantics=("parallel", "parallel", "parallel"), vmem_limit_bytes=VMEM_LIMIT),
        name="attn",
    )(q, q, k, v)


def _attn_ctx_kernel(q_ref, k_ref, vt_ref, o_ref, p_ref):
    o_ref[0] = _softmax_values_t(_scores_t(q_ref[0], k_ref, CTX_LEN), vt_ref, p_ref, CTX_LEN)


def _attention_ctx(q, k, v):
    B = q.shape[0]
    pw = 2 * HEAD_PAD
    blk = pl.BlockSpec((1, CTX_LEN, pw), lambda b, hp: (b, 0, hp))
    return pl.pallas_call(
        _attn_ctx_kernel,
        out_shape=jax.ShapeDtypeStruct((B, CTX_LEN, N_HEADS * V_HEAD), BF16),
        grid=(B, N_HEADS // 2),
        in_specs=[blk, blk, pl.BlockSpec((1, 2, HEAD_PAD, CTX_LEN), lambda b, hp: (b, hp, 0, 0))],
        out_specs=pl.BlockSpec((1, CTX_LEN, 2 * V_HEAD), lambda b, hp: (b, 0, hp)),
        scratch_shapes=[pltpu.VMEM((CTX_LEN, 2 * TILE), BF16)],
        compiler_params=pltpu.CompilerParams(
            dimension_semantics=("parallel", "parallel"), vmem_limit_bytes=VMEM_LIMIT),
        name="attn_ctx",
    )(q, k, v)


def _mixout_kernel(*refs, first_tile):
    if first_tile == 0:
        attn_ref, actx_ref = refs[0], refs[1]
        refs = refs[2:]
    else:
        attn_ref, actx_ref = refs[0], None
        refs = refs[1:]
    (u_ref, ul_ref, ur_ref, pm_ref, pl_ref, pr_ref, x_ref, mod_ref,
     cw_ref, cb_ref, cg_ref, cbeta_ref, pw_ref, ps_ref, wout_ref, g1_ref, b1_ref,
     wr_ref, br_ref, tri_ref, upper_ref,
     x1_ref, xloc_ref, rinfo_ref, ctab_ref, tot_ref,
     winu, winp, shu, shp, cnt) = refs
    i = pl.program_id(1) + first_tile
    left_ok = i >= 2
    right_ok = (i >= 1) & (i <= N_TILES - 2)

    @pl.when((pl.program_id(0) == 0) & (pl.program_id(1) == 0))
    def _():
        cnt[...] = jnp.zeros_like(cnt)

    zu = jnp.zeros((HALO, D_CONV), F32)
    winu[0:HALO, :] = jnp.where(left_ok, ul_ref[0].astype(F32), zu)
    winu[HALO:HALO + TILE, :] = u_ref[0].astype(F32)
    winu[HALO + TILE:, :] = jnp.where(right_ok, ur_ref[0].astype(F32), zu)
    winp[0:HALO, :] = jnp.where(left_ok, pl_ref[0], zu)
    winp[HALO:HALO + TILE, :] = pm_ref[0]
    winp[HALO + TILE:, :] = jnp.where(right_ok, pr_ref[0], zu)
    ph_rows = TILE + 2 * HALO - SUBLANE
    for ph in range(SUBLANE):
        shu[ph] = winu[ph:ph + ph_rows, :]
        shp[ph] = winp[ph:ph + ph_rows, :]

    def shifted(buf, off):
        ph, base = off % SUBLANE, off - off % SUBLANE
        return buf[ph, base:base + TILE, :]

    acc = jnp.zeros((TILE, D_CONV), F32)
    for j in range(CONV_WIDTH):
        acc = acc + shifted(shu, HALO - CONV_WIDTH // 2 + j) * cw_ref[j:j + 1, :]
    conv = _silu(_layer_norm_rows(acc + cb_ref[...], cg_ref[...], cbeta_ref[...]))

    def pw(o):
        return shifted(shp, HALO + o)

    centre = pw(0)
    s2 = pw(-1) + centre
    s4 = s2 + pw(-2) + pw(1)
    s8 = s4 + pw(-4) + pw(-3) + pw(2) + pw(3)
    s16 = s8 + pw(-8) + pw(-7) + pw(-6) + pw(-5) + pw(4) + pw(5) + pw(6) + pw(7)
    row = lax.broadcasted_iota(I32, (TILE, 1), 0)
    seg_len = jnp.where(i == 0, CTX_LEN, SEQ)
    t = jnp.where(i == 0, 0, (i - 1) * TILE) + row

    def mean(sw, w):
        hi = jnp.minimum(t + (w // 2 - 1), seg_len - 1)
        lo = jnp.maximum(t - w // 2, 0)
        return sw / (hi - lo + 1).astype(F32)

    lane_p = lax.broadcasted_iota(I32, (1, D_POOL), 1)
    gdim = D_POOL // len(POOL_WINDOWS)
    mixed = jnp.where(lane_p < gdim, mean(s2, 2),
                      jnp.where(lane_p < 2 * gdim, mean(s4, 4),
                                jnp.where(lane_p < 3 * gdim, mean(s8, 8), mean(s16, 16)))) - centre
    pool = _dot(mixed.astype(BF16), pw_ref[...]) * ps_ref[...]

    d_attn = N_HEADS * V_HEAD
    attn = attn_ref[0]
    if actx_ref is not None:
        attn = jnp.where(i == 0, actx_ref[0], attn)
    mix = (_dot(attn, wout_ref[0:d_attn, :])
           + _dot(conv.astype(BF16), wout_ref[d_attn:d_attn + D_CONV, :])
           + _dot(pool.astype(BF16), wout_ref[d_attn + D_CONV:, :]))
    gate1 = mod_ref[0, 2:3, :]
    x1 = _layer_norm_rows(DEEPNORM_ALPHA * x_ref[0] + gate1 * mix, g1_ref[...], b1_ref[...])
    x1_ref[0] = x1

    h2 = (x1 * (1.0 + mod_ref[0, 4:5, :]) + mod_ref[0, 3:4, :]).astype(BF16)
    logits = _dot(h2, wr_ref[...]) + br_ref[...]
    lane = lax.broadcasted_iota(I32, (TILE, LANE), 1)
    neg = jnp.float32(-jnp.inf)
    big = jnp.int32(1 << 20)
    is_g = (lane >= N_EXPERTS) & (lane < N_EXPERTS + N_GROUPS)
    gl = jnp.where(is_g, logits, neg)
    gmax = jnp.max(gl, axis=-1, keepdims=True)
    g_sel = jnp.min(jnp.where(gl == gmax, lane - N_EXPERTS, big), axis=-1, keepdims=True)
    p_sel = 1.0 / jnp.sum(jnp.exp(gl - gmax), axis=-1, keepdims=True)
    in_grp = (lane >= g_sel * EXPERTS_PER_GROUP) & (lane < (g_sel + 1) * EXPERTS_PER_GROUP)
    el = jnp.where(in_grp, logits, neg)
    v1 = jnp.max(el, axis=-1, keepdims=True)
    i1 = jnp.min(jnp.where(el == v1, lane, big), axis=-1, keepdims=True)
    el2 = jnp.where(lane == i1, neg, el)
    v2 = jnp.max(el2, axis=-1, keepdims=True)
    i2 = jnp.min(jnp.where(el2 == v2, lane, big), axis=-1, keepdims=True)
    e2 = jnp.exp(v2 - v1)
    w1 = p_sel / (1.0 + e2)
    w2 = p_sel * e2 / (1.0 + e2)

    oh1 = (lane == i1).astype(F32)
    oh2 = (lane == i2).astype(F32)
    both = oh1 + oh2
    n_e = jnp.sum(both, axis=0, keepdims=True)
    chunks_e = jnp.floor((n_e + (CHUNK - 1)) * (1.0 / CHUNK))
    chunks8 = jnp.broadcast_to(chunks_e, (SUBLANE, LANE))
    start_e = _dot(chunks8.astype(BF16), upper_ref[...])[0:1, :]
    slot_mat = _dot(tri_ref[...], both.astype(BF16)) + CHUNK * start_e
    s1 = jnp.sum(oh1 * slot_mat, axis=-1, keepdims=True)
    s2_ = jnp.sum(oh2 * slot_mat, axis=-1, keepdims=True)
    fields = (w1, w2, s1, s2_)
    rinfo = jnp.zeros((TILE, LANE), F32)
    for n, f in enumerate(fields):
        rinfo = jnp.where(lane == n, f, rinfo)
    rinfo_ref[...] = rinfo

    slot_lane = lax.broadcasted_iota(I32, (TILE, LOCAL_ROWS), 1).astype(F32)
    perm = ((slot_lane == s1).astype(F32) + (slot_lane == s2_).astype(F32)).astype(BF16)
    sorted_rows = lax.dot_general(perm, h2, (((0,), (0,)), ((), ())), preferred_element_type=F32)
    xloc_ref[...] = sorted_rows.astype(BF16)

    done_e = cnt[0:1, :]
    cidx = lax.broadcasted_iota(I32, (LANE, LANE), 0).astype(F32)
    lane_sq = lax.broadcasted_iota(I32, (LANE, LANE), 1)
    owns = ((cidx >= start_e) & (cidx < start_e + chunks_e)).astype(F32)
    e_of_c = jnp.sum(owns * lane_sq.astype(F32), axis=-1, keepdims=True)
    rel_c = jnp.sum(owns * (done_e + cidx - start_e), axis=-1, keepdims=True)
    used_c = jnp.sum(owns, axis=-1, keepdims=True)
    tab = jnp.where(lane_sq == 0, e_of_c, jnp.where(lane_sq == 1, rel_c,
                                                    jnp.where(lane_sq == 2, used_c, 0.0)))
    ctab_ref[0] = tab.T[0:SUBLANE, :]
    cnt[...] = cnt[...] + chunks_e
    tot_ref[...] = cnt[...]


def _mixout(attn, attn_ctx, u, poolin, xa, mods, cw, cb, cg, cbeta, pw, ps, wout, g1, b1, wr, br,
            tri, upper, first_tile):
    B = xa.shape[0]
    nt = N_TILES - first_tile
    nb = mods.shape[0] - 1
    hb = TILE // HALO
    n_hb = SEQ_ALL // HALO

    def mod_idx(b, i):
        return (jnp.where(i + first_tile == 0, nb, b), 0, 0)

    tok = lambda w: pl.BlockSpec((1, TILE, w), lambda b, i: (b, i + first_tile, 0))
    own = lambda w: pl.BlockSpec((1, TILE, w), lambda b, i: (b, i, 0))
    left = lambda w: pl.BlockSpec(
        (1, HALO, w), lambda b, i: (b, jnp.maximum((i + first_tile) * hb - 1, 0), 0))
    right = lambda w: pl.BlockSpec(
        (1, HALO, w), lambda b, i: (b, jnp.minimum((i + first_tile + 1) * hb, n_hb - 1), 0))
    full = lambda a: pl.BlockSpec(a.shape, lambda b, i: (0,) * a.ndim)
    d_attn = N_HEADS * V_HEAD
    attn_specs = [pl.BlockSpec((1, TILE, d_attn),
                               lambda b, i: (b, jnp.maximum(i + first_tile - 1, 0), 0))]
    attn_args = [attn]
    if first_tile == 0:
        attn_specs.append(pl.BlockSpec((1, TILE, d_attn), lambda b, i: (b, 0, 0)))
        attn_args.append(attn_ctx)
    n_tok_tiles = B * nt
    ph_rows = TILE + 2 * HALO - SUBLANE
    return pl.pallas_call(
        functools.partial(_mixout_kernel, first_tile=first_tile),
        out_shape=(
            jax.ShapeDtypeStruct((B, nt * TILE, D_MODEL), F32),
            jax.ShapeDtypeStruct((n_tok_tiles * LOCAL_ROWS, D_MODEL), BF16),
            jax.ShapeDtypeStruct((n_tok_tiles * TILE, LANE), F32),
            jax.ShapeDtypeStruct((n_tok_tiles, SUBLANE, LANE), F32),
            jax.ShapeDtypeStruct((SUBLANE, LANE), F32),
        ),
        grid=(B, nt),
        in_specs=attn_specs + [
            tok(D_CONV), left(D_CONV), right(D_CONV),
            tok(D_POOL), left(D_POOL), right(D_POOL), tok(D_MODEL),
            pl.BlockSpec((1, 8, D_MODEL), mod_idx),
            full(cw), full(cb), full(cg), full(cbeta), full(pw), full(ps), full(wout),
            full(g1), full(b1), full(wr), full(br), full(tri), full(upper)],
        out_specs=(own(D_MODEL),
                   pl.BlockSpec((LOCAL_ROWS, D_MODEL), lambda b, i: (b * nt + i, 0)),
                   pl.BlockSpec((TILE, LANE), lambda b, i: (b * nt + i, 0)),
                   pl.BlockSpec((1, SUBLANE, LANE), lambda b, i: (b * nt + i, 0, 0)),
                   pl.BlockSpec((SUBLANE, LANE), lambda b, i: (0, 0))),
        scratch_shapes=[pltpu.VMEM((TILE + 2 * HALO, D_CONV), F32),
                        pltpu.VMEM((TILE + 2 * HALO, D_POOL), F32),
                        pltpu.VMEM((SUBLANE, ph_rows, D_CONV), F32),
                        pltpu.VMEM((SUBLANE, ph_rows, D_POOL), F32),
                        pltpu.VMEM((SUBLANE, LANE), F32)],
        compiler_params=pltpu.CompilerParams(
            dimension_semantics=("arbitrary", "arbitrary"), vmem_limit_bytes=VMEM_LIMIT),
        name="mixout",
    )(*attn_args, u, u, u, poolin, poolin, poolin, xa, mods, cw, cb, cg, cbeta, pw, ps, wout,
      g1, b1, wr, br, tri, upper)


def _chunk_copy(src, src_chunk, buf, slot, dst_chunk, sem):
    src_row = src_chunk * CHUNK
    if not isinstance(src_row, int):
        src_row = pl.multiple_of(src_row, CHUNK)
    return pltpu.make_async_copy(
        src.at[pl.ds(src_row, CHUNK)],
        buf.at[slot, pl.ds(dst_chunk * CHUNK, CHUNK)],
        sem.at[slot])


def _start_gather(src, table_ref, n_chunks, buf, slot, sem):
    for c in range(n_chunks):
        _chunk_copy(src, table_ref[0, 0, c], buf, slot, c, sem).start()


def _wait_gather(src, n_chunks, buf, slot, sem):
    for c in range(n_chunks):
        _chunk_copy(src, 0, buf, slot, c, sem).wait()


def _experts_kernel(te_ref, nv_ref, cur_ref, nxt_ref, xloc_ref, wg_ref, wu_ref, wd_ref, ys_ref,
                    xbuf, wgb, wub, wdb, sem):
    i = pl.program_id(0)
    nv = nv_ref[0]
    slot = lax.rem(i, 2)

    @pl.when((i == 0) | (te_ref[i] != te_ref[jnp.maximum(i - 1, 0)]))
    def _():
        wgb[...] = wg_ref[0, 0].astype(BF16)
        wub[...] = wu_ref[0, 0].astype(BF16)
        wdb[...] = wd_ref[0, 0].astype(BF16)

    @pl.when(i == 0)
    def _():
        _start_gather(xloc_ref, cur_ref, TILE_CHUNKS, xbuf, 0, sem)

    @pl.when(i + 1 < nv)
    def _():
        _start_gather(xloc_ref, nxt_ref, TILE_CHUNKS, xbuf, 1 - slot, sem)

    @pl.when(i < nv)
    def _():
        _wait_gather(xloc_ref, TILE_CHUNKS, xbuf, slot, sem)
        xt = xbuf[slot]
        hid = (_silu(_dot(xt, wgb[...])) * _dot(xt, wub[...])).astype(BF16)
        ys_ref[...] = _dot(hid, wdb[...]).astype(BF16)

    @pl.when(i >= nv)
    def _():
        ys_ref[...] = jnp.zeros_like(ys_ref)


def _experts(tile_expert, n_valid, src_tab, xloc, w_gate, w_up, w_down, layer):
    n_tiles = src_tab.shape[0]

    def w_idx(i, te, nv):
        return (layer, te[i], 0, 0)

    tab = lambda f: pl.BlockSpec((1, 1, TILE_CHUNKS), f, memory_space=pltpu.SMEM)
    return pl.pallas_call(
        _experts_kernel,
        out_shape=jax.ShapeDtypeStruct((n_tiles * TILE, D_MODEL), BF16),
        grid_spec=pltpu.PrefetchScalarGridSpec(
            num_scalar_prefetch=2,
            grid=(n_tiles,),
            in_specs=[tab(lambda i, te, nv: (i, 0, 0)),
                      tab(lambda i, te, nv: (jnp.minimum(i + 1, n_tiles - 1), 0, 0)),
                      pl.BlockSpec(memory_space=pl.ANY),
                      pl.BlockSpec((1, 1, D_MODEL, D_EXPERT), w_idx),
                      pl.BlockSpec((1, 1, D_MODEL, D_EXPERT), w_idx),
                      pl.BlockSpec((1, 1, D_EXPERT, D_MODEL), w_idx)],
            out_specs=pl.BlockSpec((TILE, D_MODEL), lambda i, te, nv: (i, 0)),
            scratch_shapes=[pltpu.VMEM((2, TILE, D_MODEL), BF16),
                            pltpu.VMEM((D_MODEL, D_EXPERT), BF16),
                            pltpu.VMEM((D_MODEL, D_EXPERT), BF16),
                            pltpu.VMEM((D_EXPERT, D_MODEL), BF16),
                            pltpu.SemaphoreType.DMA((2,))],
        ),
        compiler_params=pltpu.CompilerParams(
            dimension_semantics=("arbitrary",), vmem_limit_bytes=VMEM_LIMIT),
        name="experts",
    )(tile_expert, n_valid, src_tab, src_tab, xloc, w_gate, w_up, w_down)


def _combine_kernel(cur_ref, nxt_ref, ys_ref, rinfo_ref, x_ref, mod_ref, g2_ref, b2_ref, o_ref,
                    ybuf, sem, *, n_tok_tiles):
    g = pl.program_id(0) * pl.num_programs(1) + pl.program_id(1)
    slot = lax.rem(g, 2)

    @pl.when(g == 0)
    def _():
        _start_gather(ys_ref, cur_ref, LOCAL_CHUNKS, ybuf, 0, sem)

    @pl.when(g + 1 < n_tok_tiles)
    def _():
        _start_gather(ys_ref, nxt_ref, LOCAL_CHUNKS, ybuf, 1 - slot, sem)

    _wait_gather(ys_ref, LOCAL_CHUNKS, ybuf, slot, sem)
    yt = ybuf[slot]
    slot_lane = lax.broadcasted_iota(I32, (TILE, LOCAL_ROWS), 1).astype(F32)
    p1 = (slot_lane == rinfo_ref[:, 2:3]).astype(F32).astype(BF16)
    p2 = (slot_lane == rinfo_ref[:, 3:4]).astype(F32).astype(BF16)
    y = rinfo_ref[:, 0:1] * _dot(p1, yt) + rinfo_ref[:, 1:2] * _dot(p2, yt)
    gate2 = mod_ref[0, 5:6, :]
    o_ref[0] = _layer_norm_rows(DEEPNORM_ALPHA * x_ref[0] + gate2 * y, g2_ref[...], b2_ref[...])


def _combine(dst_tab, ys, rinfo, x1, mods, g2, b2, first_tile):
    B = x1.shape[0]
    nt = N_TILES - first_tile
    nb = mods.shape[0] - 1
    n_tok_tiles = B * nt

    def mod_idx(b, i):
        return (jnp.where(i + first_tile == 0, nb, b), 0, 0)

    full = lambda a: pl.BlockSpec(a.shape, lambda b, i: (0,) * a.ndim)
    tab = lambda f: pl.BlockSpec((1, 1, LANE), f, memory_space=pltpu.SMEM)
    return pl.pallas_call(
        functools.partial(_combine_kernel, n_tok_tiles=n_tok_tiles),
        out_shape=jax.ShapeDtypeStruct((B, nt * TILE, D_MODEL), F32),
        grid=(B, nt),
        in_specs=[tab(lambda b, i: (b * nt + i, 0, 0)),
                  tab(lambda b, i: (jnp.minimum(b * nt + i + 1, n_tok_tiles - 1), 0, 0)),
                  pl.BlockSpec(memory_space=pl.ANY),
                  pl.BlockSpec((TILE, LANE), lambda b, i: (b * nt + i, 0)),
                  pl.BlockSpec((1, TILE, D_MODEL), lambda b, i: (b, i, 0)),
                  pl.BlockSpec((1, 8, D_MODEL), mod_idx),
                  full(g2), full(b2)],
        out_specs=pl.BlockSpec((1, TILE, D_MODEL), lambda b, i: (b, i, 0)),
        scratch_shapes=[pltpu.VMEM((2, LOCAL_ROWS, D_MODEL), BF16), pltpu.SemaphoreType.DMA((2,))],
        compiler_params=pltpu.CompilerParams(
            dimension_semantics=("arbitrary", "arbitrary"), vmem_limit_bytes=VMEM_LIMIT),
        name="combine",
    )(dst_tab, dst_tab, ys, rinfo, x1, mods, g2, b2)


def _rot_cols(w):
    w4 = w.reshape(w.shape[:-1] + (2, 2, QK_ROPE // 4))
    return jnp.stack([-w4[..., 1, :], w4[..., 0, :]], axis=-2).reshape(w.shape)


def _rope_tables():
    rows = SEQ // GRID_W
    row = jnp.repeat(jnp.arange(rows), GRID_W)
    col = jnp.tile(jnp.arange(GRID_W), rows)
    d_axis = QK_ROPE // 2
    inv_freq = jnp.power(ROPE_THETA, -jnp.arange(0, d_axis, 2, dtype=F32) / d_axis)

    def axis_angles(p):
        a = p.astype(F32)[:, None] * inv_freq[None, :]
        return jnp.concatenate([a, a], -1)

    ang = jnp.concatenate([axis_angles(row), axis_angles(col)], -1)
    cos = jnp.concatenate([jnp.ones((CTX_LEN, QK_ROPE), F32), jnp.cos(ang)], 0)
    sin = jnp.concatenate([jnp.zeros((CTX_LEN, QK_ROPE), F32), jnp.sin(ang)], 0)
    scale = math.log2(math.e) / math.sqrt(QK_HEAD)
    z64 = jnp.zeros((SEQ_ALL, QK_NOPE), F32)
    z32 = jnp.zeros((SEQ_ALL, HEAD_PAD - QK_HEAD), F32)
    cq = jnp.concatenate([jnp.full((SEQ_ALL, QK_NOPE), scale, F32), scale * cos, z32], -1)
    sq = jnp.concatenate([z64, scale * sin, z32], -1)
    ck = jnp.concatenate([z64, cos, z32], -1)
    sk = jnp.concatenate([z64, sin, z32], -1)
    return cq, sq, ck, sk


def _prep_layer(w_in, w_uq, w_ukv, conv_w, pool_w, w_out, w_rg, b_rg, w_re, b_re):
    zc = lambda r, c: jnp.zeros((r, c), F32)
    kr = w_in[:, OFF_KR:OFF_CONV]
    pad_l, pad_r = QK_NOPE, HEAD_PAD - QK_HEAD
    kr_arr = jnp.concatenate([zc(D_MODEL, pad_l), kr, zc(D_MODEL, pad_r)], -1)
    krr_arr = jnp.concatenate([zc(D_MODEL, pad_l), _rot_cols(kr), zc(D_MODEL, pad_r)], -1)
    win_ext = jnp.concatenate(
        [w_in[:, :OFF_KR], kr_arr, krr_arr, w_in[:, OFF_CONV:]], -1).astype(BF16)

    wq3 = w_uq.reshape(Q_RANK, N_HEADS, QK_HEAD)
    nope, rope = wq3[..., :QK_NOPE], wq3[..., QK_NOPE:]
    z3 = lambda c: jnp.zeros((Q_RANK, N_HEADS, c), F32)
    q_arr = jnp.concatenate([nope, rope, z3(pad_r)], -1).reshape(Q_RANK, N_HEADS * HEAD_PAD)
    q_rot = jnp.concatenate([z3(pad_l), _rot_cols(rope), z3(pad_r)], -1).reshape(Q_RANK, N_HEADS * HEAD_PAD)
    wuq_ext = jnp.concatenate([q_arr, q_rot], -1).astype(BF16)

    wkv3 = w_ukv.reshape(KV_RANK, N_HEADS, QK_NOPE + V_HEAD)
    zkv = jnp.zeros((KV_RANK, N_HEADS, HEAD_PAD - QK_NOPE), F32)
    k_arr = jnp.concatenate([wkv3[..., :QK_NOPE], zkv], -1).reshape(KV_RANK, N_HEADS * HEAD_PAD)
    v_arr = jnp.concatenate([wkv3[..., QK_NOPE:], zkv], -1).reshape(KV_RANK, N_HEADS * HEAD_PAD)
    wuk, wvt = k_arr.astype(BF16), v_arr.T.astype(BF16)

    cw = jnp.concatenate([conv_w, jnp.zeros((1, D_CONV), F32)], 0)
    gdim = D_POOL // len(POOL_WINDOWS)
    pw = jnp.zeros((D_POOL, D_POOL), F32)
    for g in range(len(POOL_WINDOWS)):
        pw = pw.at[g * gdim:(g + 1) * gdim, g * gdim:(g + 1) * gdim].set(pool_w[g])
    wr = jnp.concatenate([w_re, w_rg, zc(D_MODEL, LANE - N_EXPERTS - N_GROUPS)], -1).astype(BF16)
    br = jnp.concatenate([b_re, b_rg, jnp.zeros((LANE - N_EXPERTS - N_GROUPS,), F32)])[None, :]
    return win_ext, wuq_ext, wuk, wvt, cw, pw.astype(BF16), w_out.astype(BF16), wr, br


def _routing_tables(ctab, totals, n_tok_tiles):
    n_tiles_max = -(-(MAX_USED_CHUNKS * n_tok_tiles + (TILE_CHUNKS - 1) * N_EXPERTS) // TILE_CHUNKS)
    n_chunks_max = n_tiles_max * TILE_CHUNKS
    chunks = totals[0, :N_EXPERTS].astype(I32)
    padded = ((chunks + TILE_CHUNKS - 1) // TILE_CHUNKS) * TILE_CHUNKS
    ends = jnp.cumsum(padded)
    starts = ends - padded
    e_of_c = ctab[:, 0, :LOCAL_CHUNKS].astype(I32)
    rel_c = ctab[:, 1, :LOCAL_CHUNKS].astype(I32)
    used = ctab[:, 2, :LOCAL_CHUNKS] > 0.5
    owner = e_of_c[..., None] == jnp.arange(N_EXPERTS, dtype=I32)
    dst = jnp.sum(jnp.where(owner, starts, 0), axis=-1) + rel_c
    local_id = (jnp.arange(n_tok_tiles, dtype=I32)[:, None] * LOCAL_CHUNKS
                + jnp.arange(LOCAL_CHUNKS, dtype=I32)[None, :])
    src = jnp.full((n_chunks_max,), LOCAL_CHUNKS - 1, I32)
    src = src.at[jnp.where(used, dst, n_chunks_max).reshape(-1)].set(local_id.reshape(-1), mode="drop")
    src_tab = src.reshape(n_tiles_max, 1, TILE_CHUNKS)
    dst_tab = jnp.where(used, dst, dst[:, 0:1])
    dst_tab = jnp.concatenate(
        [dst_tab, jnp.zeros((n_tok_tiles, LANE - LOCAL_CHUNKS), I32)], -1).reshape(n_tok_tiles, 1, LANE)
    n_valid = ends[-1] // TILE_CHUNKS
    tile_start = jnp.minimum(jnp.arange(n_tiles_max, dtype=I32), n_valid - 1) * TILE_CHUNKS
    tile_expert = jnp.sum((tile_start[:, None] >= ends[None, :]).astype(I32), axis=-1)
    return src_tab, dst_tab, tile_expert.astype(I32), n_valid.astype(I32).reshape(1)


def kernel(x, c, ctx, c_ctx, w_ada, b_ada, w_in, g_q, w_uq, g_kv, w_ukv, conv_w, conv_b, conv_ln_g,
           conv_ln_b, pool_w, pool_scale, w_out, ln1_g, ln1_b, w_router_group, b_router_group,
           w_router_expert, b_router_expert, w_gate, w_up, w_down, ln2_g, ln2_b):
    B = x.shape[0]
    L = w_ada.shape[0]
    tabs = _rope_tables()
    tri = jnp.tril(jnp.ones((TILE, TILE), F32), -1).astype(BF16)
    upper = jnp.triu(jnp.ones((LANE, LANE), F32), 1).astype(BF16)

    cc = jnp.concatenate([c, c_ctx[None, :], jnp.zeros((16 - B - 1, D_MODEL), F32)], 0)
    ada = _ada(cc, w_ada, b_ada)
    mods_all = ada[:, :B + 1].reshape(L, B + 1, 6, D_MODEL)
    mods_all = jnp.concatenate([mods_all, jnp.zeros((L, B + 1, 2, D_MODEL), F32)], 2)

    xa = jnp.concatenate([ctx, x], axis=1)
    row = lambda a: a[None, :]
    for l in range(L):
        last = l == L - 1
        first_tile = 1 if last else 0
        mods = mods_all[l]
        win_ext, wuq_ext, wuk, wvt, cw, pw, wout, wr, br = _prep_layer(
            w_in[l], w_uq[l], w_ukv[l], conv_w[l], pool_w[l], w_out[l], w_router_group[l],
            b_router_group[l], w_router_expert[l], b_router_expert[l])

        q, k, v, u, poolin = _inproj(xa, mods, tabs, win_ext, row(g_q[l]), wuq_ext, row(g_kv[l]),
                                     wuk, wvt)
        attn = _attention(q, k, v)
        attn_ctx = None if last else _attention_ctx(q, k, v)
        x1, xloc, rinfo, ctab, totals = _mixout(
            attn, attn_ctx, u, poolin, xa, mods, cw, row(conv_b[l]), row(conv_ln_g[l]),
            row(conv_ln_b[l]), pw, row(pool_scale[l]), wout, row(ln1_g[l]), row(ln1_b[l]), wr, br,
            tri, upper, first_tile)

        n_tok_tiles = B * (N_TILES - first_tile)
        src_tab, dst_tab, tile_expert, n_valid = _routing_tables(ctab, totals, n_tok_tiles)
        ys = _experts(tile_expert, n_valid, src_tab, xloc, w_gate, w_up, w_down, l)
        xa = _combine(dst_tab, ys, rinfo, x1, mods, row(ln2_g[l]), row(ln2_b[l]), first_tile)
    return xa
```

```python
import functools
import math

import jax
import jax.numpy as jnp
from jax import lax
from jax.experimental import pallas as pl
from jax.experimental.pallas import tpu as pltpu

D_MODEL = 1024
SEQ = 4096
CTX_LEN = 256
SEQ_ALL = CTX_LEN + SEQ
GRID_W = 64
N_HEADS = 8
QK_NOPE = 64
QK_ROPE = 32
QK_HEAD = QK_NOPE + QK_ROPE
V_HEAD = 64
Q_RANK = 384
KV_RANK = 256
D_CONV = 256
CONV_WIDTH = 31
D_POOL = 256
POOL_WINDOWS = (2, 4, 8, 16)
ROPE_THETA = 10000.0
OFF_KV = Q_RANK
OFF_KR = OFF_KV + KV_RANK
OFF_CONV = OFF_KR + QK_ROPE
OFF_POOL = OFF_CONV + 2 * D_CONV
N_GROUPS = 4
EXPERTS_PER_GROUP = 8
N_EXPERTS = 32
D_EXPERT = 256
LN_EPS = 1e-5
RMS_EPS = 1e-6
DEPTH = 2
DEEPNORM_ALPHA = (2 * DEPTH) ** 0.25

LANE = 128
SUBLANE = 8
TILE = 256
HALO = 16
N_TILES = SEQ_ALL // TILE
HEAD_PAD = 128
KEY_CHUNK = 256
VMEM_LIMIT = 56 * 1024 * 1024

CHUNK = 2 * SUBLANE
MAX_USED_CHUNKS = (2 * TILE + (CHUNK - 1) * N_EXPERTS) // CHUNK
LOCAL_ROWS = 1024
LOCAL_CHUNKS = LOCAL_ROWS // CHUNK
assert MAX_USED_CHUNKS < LOCAL_CHUNKS <= LANE
TILE_CHUNKS = TILE // CHUNK
EXPERT_TILES = 2

XOFF_Q = 0
XOFF_KV = 384
XOFF_KR = 640
XOFF_KRR = 768
XOFF_CONV = 896
XOFF_POOL = 1408
D_IN_EXT = 1664

BF16 = jnp.bfloat16
F32 = jnp.float32
I32 = jnp.int32


def _dot(a, b):
    return jnp.dot(a, b, preferred_element_type=F32)


def _silu(x):
    return x * jax.nn.sigmoid(x)


def _layer_norm_rows(x, g, b):
    mu = jnp.mean(x, axis=-1, keepdims=True)
    xc = x - mu
    var = jnp.mean(xc * xc, axis=-1, keepdims=True)
    return xc * lax.rsqrt(var + LN_EPS) * g + b


def _ada_kernel(cc_ref, w_ref, b_ref, o_ref):
    a = _silu(cc_ref[...])
    o_ref[0] = jnp.dot(a, w_ref[0], preferred_element_type=F32,
                       precision=lax.Precision.HIGHEST) + b_ref[0]


def _ada(cc, w_ada, b_ada):
    L = w_ada.shape[0]
    n = w_ada.shape[2]
    bn = 512
    return pl.pallas_call(
        _ada_kernel,
        out_shape=jax.ShapeDtypeStruct((L, cc.shape[0], n), F32),
        grid=(L, n // bn),
        in_specs=[
            pl.BlockSpec(cc.shape, lambda l, j: (0, 0)),
            pl.BlockSpec((1, D_MODEL, bn), lambda l, j: (l, 0, j)),
            pl.BlockSpec((1, 1, bn), lambda l, j: (l, 0, j)),
        ],
        out_specs=pl.BlockSpec((1, cc.shape[0], bn), lambda l, j: (l, 0, j)),
        compiler_params=pltpu.CompilerParams(
            dimension_semantics=("parallel", "parallel"), vmem_limit_bytes=VMEM_LIMIT),
        name="ada",
    )(cc, w_ada, b_ada.reshape(L, 1, n))


def _inproj_kernel(x_ref, mod_ref, cq_ref, sq_ref, ck_ref, sk_ref, win_ref, gq_ref, wuq_ref,
                   gkv_ref, wuk_ref, wvt_ref, q_ref, k_ref, vt_ref, u_ref, pool_ref):
    x = x_ref[0]
    sh = mod_ref[0, 0:1, :]
    sc = mod_ref[0, 1:2, :]
    h = (x * (1.0 + sc) + sh).astype(BF16)
    p = _dot(h, win_ref[...])

    pq = p[:, XOFF_Q:XOFF_KV]
    qn = pq * lax.rsqrt(jnp.mean(pq * pq, axis=-1, keepdims=True) + RMS_EPS) * gq_ref[...]
    q2 = _dot(qn.astype(BF16), wuq_ref[...])
    cq = cq_ref[...]
    sq = sq_ref[...]
    nq = N_HEADS * HEAD_PAD
    for hd in range(N_HEADS):
        lo = hd * HEAD_PAD
        qh = q2[:, lo:lo + HEAD_PAD] * cq + q2[:, nq + lo:nq + lo + HEAD_PAD] * sq
        q_ref[0, :, lo:lo + HEAD_PAD] = qh.astype(BF16)

    pkv = p[:, XOFF_KV:XOFF_KR]
    kvn = pkv * lax.rsqrt(jnp.mean(pkv * pkv, axis=-1, keepdims=True) + RMS_EPS) * gkv_ref[...]
    kvn = kvn.astype(BF16)
    kn = _dot(kvn, wuk_ref[...])
    kr = p[:, XOFF_KR:XOFF_KRR] * ck_ref[...] + p[:, XOFF_KRR:XOFF_CONV] * sk_ref[...]
    for hd in range(N_HEADS):
        lo = hd * HEAD_PAD
        k_ref[0, :, lo:lo + HEAD_PAD] = (kn[:, lo:lo + HEAD_PAD] + kr).astype(BF16)
    vt = lax.dot_general(wvt_ref[...], kvn, (((1,), (1,)), ((), ())), preferred_element_type=F32)
    vrow = lax.broadcasted_iota(I32, (N_HEADS * HEAD_PAD, TILE), 0)
    vt = jnp.where((vrow & (HEAD_PAD - 1)) == V_HEAD, 1.0, vt).astype(BF16)
    for hd in range(N_HEADS):
        vt_ref[0, hd] = vt[hd * HEAD_PAD:(hd + 1) * HEAD_PAD, :]

    a = p[:, XOFF_CONV:XOFF_CONV + D_CONV]
    gt = p[:, XOFF_CONV + D_CONV:XOFF_POOL]
    u_ref[0] = (a * jax.nn.sigmoid(gt)).astype(BF16)
    pool_ref[0] = p[:, XOFF_POOL:]


def _inproj(xa, mods, tabs, win_ext, gq, wuq_ext, gkv, wuk, wvt):
    B = xa.shape[0]
    nb = mods.shape[0] - 1

    def mod_idx(b, i):
        return (jnp.where(i == 0, nb, b), 0, 0)

    tab_spec = pl.BlockSpec((TILE, LANE), lambda b, i: (i, 0))
    full = lambda a: pl.BlockSpec(a.shape, lambda b, i: (0,) * a.ndim)
    tok = lambda w: pl.BlockSpec((1, TILE, w), lambda b, i: (b, i, 0))
    hw = N_HEADS * HEAD_PAD
    return pl.pallas_call(
        _inproj_kernel,
        out_shape=(
            jax.ShapeDtypeStruct((B, SEQ_ALL, hw), BF16),
            jax.ShapeDtypeStruct((B, SEQ_ALL, hw), BF16),
            jax.ShapeDtypeStruct((B, N_HEADS, HEAD_PAD, SEQ_ALL), BF16),
            jax.ShapeDtypeStruct((B, SEQ_ALL, D_CONV), BF16),
            jax.ShapeDtypeStruct((B, SEQ_ALL, D_POOL), F32),
        ),
        grid=(B, N_TILES),
        in_specs=[tok(D_MODEL), pl.BlockSpec((1, 8, D_MODEL), mod_idx),
                  tab_spec, tab_spec, tab_spec, tab_spec,
                  full(win_ext), full(gq), full(wuq_ext), full(gkv), full(wuk), full(wvt)],
        out_specs=(tok(hw), tok(hw),
                   pl.BlockSpec((1, N_HEADS, HEAD_PAD, TILE), lambda b, i: (b, 0, 0, i)),
                   tok(D_CONV), tok(D_POOL)),
        compiler_params=pltpu.CompilerParams(
            dimension_semantics=("parallel", "parallel"), vmem_limit_bytes=VMEM_LIMIT),
        name="inproj",
    )(xa, mods, *tabs, win_ext, gq, wuq_ext, gkv, wuk, wvt)


def _scores_t(q, k_ref, n_keys):
    lane = lax.broadcasted_iota(I32, (TILE, 2 * HEAD_PAD), 1)
    qbd = jnp.concatenate([jnp.where(lane < HEAD_PAD, q, jnp.zeros_like(q)),
                           jnp.where(lane >= HEAD_PAD, q, jnp.zeros_like(q))], axis=0)
    return lax.dot_general(k_ref[0, 0:n_keys, :], qbd, (((1,), (1,)), ((), ())),
                           preferred_element_type=F32)


def _store_scores(s_ref, m_ref, st):
    s_ref[...] = st
    m_ref[...] = jnp.broadcast_to(jnp.max(st, axis=0, keepdims=True), m_ref.shape)


def _softmax_values_t(s_ref, m_ref, vt_ref, p_ref, n_keys):
    m = m_ref[0:1, :]
    for kc in range(0, n_keys, KEY_CHUNK):
        p_ref[kc:kc + KEY_CHUNK, :] = jnp.exp2(s_ref[kc:kc + KEY_CHUNK, :] - m).astype(BF16)
    outs = []
    for hd in range(2):
        ot = _dot(vt_ref[0, hd, :, 0:n_keys], p_ref[0:n_keys, hd * TILE:(hd + 1) * TILE])
        outs.append(ot[0:V_HEAD, :] / ot[V_HEAD:V_HEAD + 1, :])
    return jnp.concatenate(outs, axis=0).T.astype(BF16)


def _attn_kernel(qc_ref, qn_ref, k_ref, vt_ref, o_ref, sa_ref, sb_ref, ma_ref, mb_ref, p_ref):
    i = pl.program_id(2)

    @pl.when(i == 0)
    def _():
        _store_scores(sa_ref, ma_ref, _scores_t(qc_ref[0], k_ref, SEQ_ALL))

    def step(cur_ref, cur_m_ref, nxt_ref, nxt_m_ref):
        _store_scores(nxt_ref, nxt_m_ref, _scores_t(qn_ref[0], k_ref, SEQ_ALL))
        o_ref[0] = _softmax_values_t(cur_ref, cur_m_ref, vt_ref, p_ref, SEQ_ALL)

    parity = lax.rem(i, 2)

    @pl.when(parity == 0)
    def _():
        step(sa_ref, ma_ref, sb_ref, mb_ref)

    @pl.when(parity == 1)
    def _():
        step(sb_ref, mb_ref, sa_ref, ma_ref)


def _attention(q, k, v):
    B = q.shape[0]
    n_pairs = N_HEADS // 2
    pw = 2 * HEAD_PAD
    nq = SEQ // TILE
    assert nq % 2 == 0
    return pl.pallas_call(
        _attn_kernel,
        out_shape=jax.ShapeDtypeStruct((B, SEQ, N_HEADS * V_HEAD), BF16),
        grid=(B, n_pairs, nq),
        in_specs=[
            pl.BlockSpec((1, TILE, pw), lambda b, hp, i: (b, i + 1, hp)),
            pl.BlockSpec((1, TILE, pw), lambda b, hp, i: (b, jnp.minimum(i + 2, nq), hp)),
            pl.BlockSpec((1, SEQ_ALL, pw), lambda b, hp, i: (b, 0, hp)),
            pl.BlockSpec((1, 2, HEAD_PAD, SEQ_ALL), lambda b, hp, i: (b, hp, 0, 0)),
        ],
        out_specs=pl.BlockSpec((1, TILE, 2 * V_HEAD), lambda b, hp, i: (b, i, hp)),
        scratch_shapes=[pltpu.VMEM((SEQ_ALL, 2 * TILE), F32), pltpu.VMEM((SEQ_ALL, 2 * TILE), F32),
                        pltpu.VMEM((SUBLANE, 2 * TILE), F32), pltpu.VMEM((SUBLANE, 2 * TILE), F32),
                        pltpu.VMEM((SEQ_ALL, 2 * TILE), BF16)],
        compiler_params=pltpu.CompilerParams(
            dimension_semantics=("arbitrary", "arbitrary", "arbitrary"),
            vmem_limit_bytes=VMEM_LIMIT),
        name="attn",
    )(q, q, k, v)


def _attn_ctx_kernel(q_ref, k_ref, vt_ref, o_ref, s_ref, m_ref, p_ref):
    _store_scores(s_ref, m_ref, _scores_t(q_ref[0], k_ref, CTX_LEN))
    o_ref[0] = _softmax_values_t(s_ref, m_ref, vt_ref, p_ref, CTX_LEN)


def _attention_ctx(q, k, v):
    B = q.shape[0]
    pw = 2 * HEAD_PAD
    blk = pl.BlockSpec((1, CTX_LEN, pw), lambda b, hp: (b, 0, hp))
    return pl.pallas_call(
        _attn_ctx_kernel,
        out_shape=jax.ShapeDtypeStruct((B, CTX_LEN, N_HEADS * V_HEAD), BF16),
        grid=(B, N_HEADS // 2),
        in_specs=[blk, blk, pl.BlockSpec((1, 2, HEAD_PAD, CTX_LEN), lambda b, hp: (b, hp, 0, 0))],
        out_specs=pl.BlockSpec((1, CTX_LEN, 2 * V_HEAD), lambda b, hp: (b, 0, hp)),
        scratch_shapes=[pltpu.VMEM((CTX_LEN, 2 * TILE), F32), pltpu.VMEM((SUBLANE, 2 * TILE), F32),
                        pltpu.VMEM((CTX_LEN, 2 * TILE), BF16)],
        compiler_params=pltpu.CompilerParams(
            dimension_semantics=("parallel", "parallel"), vmem_limit_bytes=VMEM_LIMIT),
        name="attn_ctx",
    )(q, k, v)


def _mixout_kernel(*refs, first_tile):
    if first_tile == 0:
        attn_ref, actx_ref = refs[0], refs[1]
        refs = refs[2:]
    else:
        attn_ref, actx_ref = refs[0], None
        refs = refs[1:]
    (u_ref, ul_ref, ur_ref, pm_ref, pl_ref, pr_ref, x_ref, mod_ref,
     cw_ref, cb_ref, cg_ref, cbeta_ref, pw_ref, ps_ref, wout_ref, g1_ref, b1_ref,
     wr_ref, br_ref, tri_ref, upper_ref,
     x1_ref, xloc_ref, rinfo_ref, ctab_ref, tot_ref,
     winu, winp, shu, shp, cnt) = refs
    i = pl.program_id(1) + first_tile
    left_ok = i >= 2
    right_ok = (i >= 1) & (i <= N_TILES - 2)

    @pl.when((pl.program_id(0) == 0) & (pl.program_id(1) == 0))
    def _():
        cnt[...] = jnp.zeros_like(cnt)

    zu = jnp.zeros((HALO, D_CONV), F32)
    winu[0:HALO, :] = jnp.where(left_ok, ul_ref[0].astype(F32), zu)
    winu[HALO:HALO + TILE, :] = u_ref[0].astype(F32)
    winu[HALO + TILE:, :] = jnp.where(right_ok, ur_ref[0].astype(F32), zu)
    winp[0:HALO, :] = jnp.where(left_ok, pl_ref[0], zu)
    winp[HALO:HALO + TILE, :] = pm_ref[0]
    winp[HALO + TILE:, :] = jnp.where(right_ok, pr_ref[0], zu)
    ph_rows = TILE + 2 * HALO - SUBLANE
    for ph in range(SUBLANE):
        shu[ph] = winu[ph:ph + ph_rows, :]
        shp[ph] = winp[ph:ph + ph_rows, :]

    def shifted(buf, off):
        ph, base = off % SUBLANE, off - off % SUBLANE
        return buf[ph, base:base + TILE, :]

    acc = jnp.zeros((TILE, D_CONV), F32)
    for j in range(CONV_WIDTH):
        acc = acc + shifted(shu, HALO - CONV_WIDTH // 2 + j) * cw_ref[j:j + 1, :]
    conv = _silu(_layer_norm_rows(acc + cb_ref[...], cg_ref[...], cbeta_ref[...]))

    def pw(o):
        return shifted(shp, HALO + o)

    centre = pw(0)
    s2 = pw(-1) + centre
    s4 = s2 + pw(-2) + pw(1)
    s8 = s4 + pw(-4) + pw(-3) + pw(2) + pw(3)
    s16 = s8 + pw(-8) + pw(-7) + pw(-6) + pw(-5) + pw(4) + pw(5) + pw(6) + pw(7)
    row = lax.broadcasted_iota(I32, (TILE, 1), 0)
    seg_len = jnp.where(i == 0, CTX_LEN, SEQ)
    t = jnp.where(i == 0, 0, (i - 1) * TILE) + row

    def mean(sw, w):
        hi = jnp.minimum(t + (w // 2 - 1), seg_len - 1)
        lo = jnp.maximum(t - w // 2, 0)
        return sw / (hi - lo + 1).astype(F32)

    lane_p = lax.broadcasted_iota(I32, (1, D_POOL), 1)
    gdim = D_POOL // len(POOL_WINDOWS)
    mixed = jnp.where(lane_p < gdim, mean(s2, 2),
                      jnp.where(lane_p < 2 * gdim, mean(s4, 4),
                                jnp.where(lane_p < 3 * gdim, mean(s8, 8), mean(s16, 16)))) - centre
    pool = _dot(mixed.astype(BF16), pw_ref[...]) * ps_ref[...]

    d_attn = N_HEADS * V_HEAD
    attn = attn_ref[0]
    if actx_ref is not None:
        attn = jnp.where(i == 0, actx_ref[0], attn)
    mix = (_dot(attn, wout_ref[0:d_attn, :])
           + _dot(conv.astype(BF16), wout_ref[d_attn:d_attn + D_CONV, :])
           + _dot(pool.astype(BF16), wout_ref[d_attn + D_CONV:, :]))
    gate1 = mod_ref[0, 2:3, :]
    x1 = _layer_norm_rows(DEEPNORM_ALPHA * x_ref[0] + gate1 * mix, g1_ref[...], b1_ref[...])
    x1_ref[0] = x1

    h2 = (x1 * (1.0 + mod_ref[0, 4:5, :]) + mod_ref[0, 3:4, :]).astype(BF16)
    logits = _dot(h2, wr_ref[...]) + br_ref[...]
    lane = lax.broadcasted_iota(I32, (TILE, LANE), 1)
    neg = jnp.float32(-jnp.inf)
    big = jnp.int32(1 << 20)
    is_g = (lane >= N_EXPERTS) & (lane < N_EXPERTS + N_GROUPS)
    gl = jnp.where(is_g, logits, neg)
    gmax = jnp.max(gl, axis=-1, keepdims=True)
    g_sel = jnp.min(jnp.where(gl == gmax, lane - N_EXPERTS, big), axis=-1, keepdims=True)
    p_sel = 1.0 / jnp.sum(jnp.exp(gl - gmax), axis=-1, keepdims=True)
    in_grp = (lane >= g_sel * EXPERTS_PER_GROUP) & (lane < (g_sel + 1) * EXPERTS_PER_GROUP)
    el = jnp.where(in_grp, logits, neg)
    v1 = jnp.max(el, axis=-1, keepdims=True)
    i1 = jnp.min(jnp.where(el == v1, lane, big), axis=-1, keepdims=True)
    el2 = jnp.where(lane == i1, neg, el)
    v2 = jnp.max(el2, axis=-1, keepdims=True)
    i2 = jnp.min(jnp.where(el2 == v2, lane, big), axis=-1, keepdims=True)
    e2 = jnp.exp(v2 - v1)
    w1 = p_sel / (1.0 + e2)
    w2 = p_sel * e2 / (1.0 + e2)

    oh1 = (lane == i1).astype(F32)
    oh2 = (lane == i2).astype(F32)
    both = oh1 + oh2
    n_e = jnp.sum(both, axis=0, keepdims=True)
    chunks_e = jnp.floor((n_e + (CHUNK - 1)) * (1.0 / CHUNK))
    chunks8 = jnp.broadcast_to(chunks_e, (SUBLANE, LANE))
    start_e = _dot(chunks8.astype(BF16), upper_ref[...])[0:1, :]
    slot_mat = _dot(tri_ref[...], both.astype(BF16)) + CHUNK * start_e
    s1 = jnp.sum(oh1 * slot_mat, axis=-1, keepdims=True)
    s2_ = jnp.sum(oh2 * slot_mat, axis=-1, keepdims=True)
    fields = (w1, w2, s1, s2_)
    rinfo = jnp.zeros((TILE, LANE), F32)
    for n, f in enumerate(fields):
        rinfo = jnp.where(lane == n, f, rinfo)
    rinfo_ref[...] = rinfo

    slot_lane = lax.broadcasted_iota(I32, (TILE, LOCAL_ROWS), 1).astype(F32)
    perm = ((slot_lane == s1).astype(F32) + (slot_lane == s2_).astype(F32)).astype(BF16)
    sorted_rows = lax.dot_general(perm, h2, (((0,), (0,)), ((), ())), preferred_element_type=F32)
    xloc_ref[...] = sorted_rows.astype(BF16)

    done_e = cnt[0:1, :]
    cidx = lax.broadcasted_iota(I32, (LANE, LANE), 0).astype(F32)
    lane_sq = lax.broadcasted_iota(I32, (LANE, LANE), 1)
    owns = ((cidx >= start_e) & (cidx < start_e + chunks_e)).astype(F32)
    e_of_c = jnp.sum(owns * lane_sq.astype(F32), axis=-1, keepdims=True)
    rel_c = jnp.sum(owns * (done_e + cidx - start_e), axis=-1, keepdims=True)
    used_c = jnp.sum(owns, axis=-1, keepdims=True)
    tab = jnp.where(lane_sq == 0, e_of_c, jnp.where(lane_sq == 1, rel_c,
                                                    jnp.where(lane_sq == 2, used_c, 0.0)))
    ctab_ref[0] = tab.T[0:SUBLANE, :]
    cnt[...] = cnt[...] + chunks_e
    tot_ref[...] = cnt[...]


def _mixout(attn, attn_ctx, u, poolin, xa, mods, cw, cb, cg, cbeta, pw, ps, wout, g1, b1, wr, br,
            tri, upper, first_tile):
    B = xa.shape[0]
    nt = N_TILES - first_tile
    nb = mods.shape[0] - 1
    hb = TILE // HALO
    n_hb = SEQ_ALL // HALO

    def mod_idx(b, i):
        return (jnp.where(i + first_tile == 0, nb, b), 0, 0)

    tok = lambda w: pl.BlockSpec((1, TILE, w), lambda b, i: (b, i + first_tile, 0))
    own = lambda w: pl.BlockSpec((1, TILE, w), lambda b, i: (b, i, 0))
    left = lambda w: pl.BlockSpec(
        (1, HALO, w), lambda b, i: (b, jnp.maximum((i + first_tile) * hb - 1, 0), 0))
    right = lambda w: pl.BlockSpec(
        (1, HALO, w), lambda b, i: (b, jnp.minimum((i + first_tile + 1) * hb, n_hb - 1), 0))
    full = lambda a: pl.BlockSpec(a.shape, lambda b, i: (0,) * a.ndim)
    d_attn = N_HEADS * V_HEAD
    attn_specs = [pl.BlockSpec((1, TILE, d_attn),
                               lambda b, i: (b, jnp.maximum(i + first_tile - 1, 0), 0))]
    attn_args = [attn]
    if first_tile == 0:
        attn_specs.append(pl.BlockSpec((1, TILE, d_attn), lambda b, i: (b, 0, 0)))
        attn_args.append(attn_ctx)
    n_tok_tiles = B * nt
    ph_rows = TILE + 2 * HALO - SUBLANE
    return pl.pallas_call(
        functools.partial(_mixout_kernel, first_tile=first_tile),
        out_shape=(
            jax.ShapeDtypeStruct((B, nt * TILE, D_MODEL), F32),
            jax.ShapeDtypeStruct((n_tok_tiles * LOCAL_ROWS, D_MODEL), BF16),
            jax.ShapeDtypeStruct((n_tok_tiles * TILE, LANE), F32),
            jax.ShapeDtypeStruct((n_tok_tiles, SUBLANE, LANE), F32),
            jax.ShapeDtypeStruct((SUBLANE, LANE), F32),
        ),
        grid=(B, nt),
        in_specs=attn_specs + [
            tok(D_CONV), left(D_CONV), right(D_CONV),
            tok(D_POOL), left(D_POOL), right(D_POOL), tok(D_MODEL),
            pl.BlockSpec((1, 8, D_MODEL), mod_idx),
            full(cw), full(cb), full(cg), full(cbeta), full(pw), full(ps), full(wout),
            full(g1), full(b1), full(wr), full(br), full(tri), full(upper)],
        out_specs=(own(D_MODEL),
                   pl.BlockSpec((LOCAL_ROWS, D_MODEL), lambda b, i: (b * nt + i, 0)),
                   pl.BlockSpec((TILE, LANE), lambda b, i: (b * nt + i, 0)),
                   pl.BlockSpec((1, SUBLANE, LANE), lambda b, i: (b * nt + i, 0, 0)),
                   pl.BlockSpec((SUBLANE, LANE), lambda b, i: (0, 0))),
        scratch_shapes=[pltpu.VMEM((TILE + 2 * HALO, D_CONV), F32),
                        pltpu.VMEM((TILE + 2 * HALO, D_POOL), F32),
                        pltpu.VMEM((SUBLANE, ph_rows, D_CONV), F32),
                        pltpu.VMEM((SUBLANE, ph_rows, D_POOL), F32),
                        pltpu.VMEM((SUBLANE, LANE), F32)],
        compiler_params=pltpu.CompilerParams(
            dimension_semantics=("arbitrary", "arbitrary"), vmem_limit_bytes=VMEM_LIMIT),
        name="mixout",
    )(*attn_args, u, u, u, poolin, poolin, poolin, xa, mods, cw, cb, cg, cbeta, pw, ps, wout,
      g1, b1, wr, br, tri, upper)


def _chunk_copy(src, src_chunk, buf, slot, dst_chunk, sem):
    src_row = src_chunk * CHUNK
    if not isinstance(src_row, int):
        src_row = pl.multiple_of(src_row, CHUNK)
    return pltpu.make_async_copy(
        src.at[pl.ds(src_row, CHUNK)],
        buf.at[slot, pl.ds(dst_chunk * CHUNK, CHUNK)],
        sem.at[slot])


def _start_gather(src, table_ref, n_chunks, buf, slot, sem):
    for c in range(n_chunks):
        _chunk_copy(src, table_ref[0, 0, c], buf, slot, c, sem).start()


def _wait_gather(src, n_chunks, buf, slot, sem):
    for c in range(n_chunks):
        _chunk_copy(src, 0, buf, slot, c, sem).wait()


def _experts_kernel(te_ref, nv_ref, cur_ref, nxt_ref, xloc_ref, wga_ref, wua_ref, wda_ref,
                    wgb_ref, wub_ref, wdb_ref, ys_ref,
                    xbuf, cga, cua, cda, cgb, cub, cdb, sem):
    i = pl.program_id(0)
    nv = nv_ref[0]
    slot = lax.rem(i, 2)
    n_chunks = EXPERT_TILES * TILE_CHUNKS
    w_refs = ((wga_ref, wua_ref, wda_ref), (wgb_ref, wub_ref, wdb_ref))
    caches = ((cga, cua, cda), (cgb, cub, cdb))

    for t in range(EXPERT_TILES):
        tile = EXPERT_TILES * i + t
        prev = jnp.maximum(tile - EXPERT_TILES, 0)

        @pl.when((i == 0) | (te_ref[tile] != te_ref[prev]))
        def _(t=t):
            for cache, w_ref in zip(caches[t], w_refs[t]):
                cache[...] = w_ref[0, 0].astype(BF16)

    @pl.when(i == 0)
    def _():
        _start_gather(xloc_ref, cur_ref, n_chunks, xbuf, 0, sem)

    @pl.when(EXPERT_TILES * (i + 1) < nv)
    def _():
        _start_gather(xloc_ref, nxt_ref, n_chunks, xbuf, 1 - slot, sem)

    @pl.when(EXPERT_TILES * i < nv)
    def _():
        _wait_gather(xloc_ref, n_chunks, xbuf, slot, sem)
        for t in range(EXPERT_TILES):
            wg, wu, wd = caches[t]
            xt = xbuf[slot, t * TILE:(t + 1) * TILE, :]
            hid = (_silu(_dot(xt, wg[...])) * _dot(xt, wu[...])).astype(BF16)
            ys_ref[t * TILE:(t + 1) * TILE, :] = _dot(hid, wd[...]).astype(BF16)

    @pl.when(EXPERT_TILES * i >= nv)
    def _():
        ys_ref[...] = jnp.zeros_like(ys_ref)


def _experts(tile_expert, n_valid, src_tab, xloc, w_gate, w_up, w_down, layer):
    n_steps = src_tab.shape[0]
    rows = EXPERT_TILES * TILE

    def w_idx(t):
        return lambda i, te, nv: (layer, te[EXPERT_TILES * i + t], 0, 0)

    tab = lambda f: pl.BlockSpec((1, 1, EXPERT_TILES * TILE_CHUNKS), f, memory_space=pltpu.SMEM)
    w_specs, w_cache = [], []
    for t in range(EXPERT_TILES):
        w_specs += [pl.BlockSpec((1, 1, D_MODEL, D_EXPERT), w_idx(t)),
                    pl.BlockSpec((1, 1, D_MODEL, D_EXPERT), w_idx(t)),
                    pl.BlockSpec((1, 1, D_EXPERT, D_MODEL), w_idx(t))]
        w_cache += [pltpu.VMEM((D_MODEL, D_EXPERT), BF16), pltpu.VMEM((D_MODEL, D_EXPERT), BF16),
                    pltpu.VMEM((D_EXPERT, D_MODEL), BF16)]
    return pl.pallas_call(
        _experts_kernel,
        out_shape=jax.ShapeDtypeStruct((n_steps * rows, D_MODEL), BF16),
        grid_spec=pltpu.PrefetchScalarGridSpec(
            num_scalar_prefetch=2,
            grid=(n_steps,),
            in_specs=[tab(lambda i, te, nv: (i, 0, 0)),
                      tab(lambda i, te, nv: (jnp.minimum(i + 1, n_steps - 1), 0, 0)),
                      pl.BlockSpec(memory_space=pl.ANY)] + w_specs,
            out_specs=pl.BlockSpec((rows, D_MODEL), lambda i, te, nv: (i, 0)),
            scratch_shapes=[pltpu.VMEM((2, rows, D_MODEL), BF16)] + w_cache
            + [pltpu.SemaphoreType.DMA((2,))],
        ),
        compiler_params=pltpu.CompilerParams(
            dimension_semantics=("arbitrary",), vmem_limit_bytes=VMEM_LIMIT),
        name="experts",
    )(tile_expert, n_valid, src_tab, src_tab, xloc, *([w_gate, w_up, w_down] * EXPERT_TILES))


def _combine_kernel(cur_ref, nxt_ref, ys_ref, rinfo_ref, x_ref, mod_ref, g2_ref, b2_ref, o_ref,
                    ybuf, sem, *, n_tok_tiles):
    g = pl.program_id(0) * pl.num_programs(1) + pl.program_id(1)
    slot = lax.rem(g, 2)

    @pl.when(g == 0)
    def _():
        _start_gather(ys_ref, cur_ref, LOCAL_CHUNKS, ybuf, 0, sem)

    @pl.when(g + 1 < n_tok_tiles)
    def _():
        _start_gather(ys_ref, nxt_ref, LOCAL_CHUNKS, ybuf, 1 - slot, sem)

    _wait_gather(ys_ref, LOCAL_CHUNKS, ybuf, slot, sem)
    yt = ybuf[slot]
    slot_lane = lax.broadcasted_iota(I32, (TILE, LOCAL_ROWS), 1).astype(F32)
    sel = (jnp.where(slot_lane == rinfo_ref[:, 2:3], rinfo_ref[:, 0:1], 0.0)
           + jnp.where(slot_lane == rinfo_ref[:, 3:4], rinfo_ref[:, 1:2], 0.0))
    y = _dot(sel.astype(BF16), yt)
    gate2 = mod_ref[0, 5:6, :]
    o_ref[0] = _layer_norm_rows(DEEPNORM_ALPHA * x_ref[0] + gate2 * y, g2_ref[...], b2_ref[...])


def _combine(dst_tab, ys, rinfo, x1, mods, g2, b2, first_tile):
    B = x1.shape[0]
    nt = N_TILES - first_tile
    nb = mods.shape[0] - 1
    n_tok_tiles = B * nt

    def mod_idx(b, i):
        return (jnp.where(i + first_tile == 0, nb, b), 0, 0)

    full = lambda a: pl.BlockSpec(a.shape, lambda b, i: (0,) * a.ndim)
    tab = lambda f: pl.BlockSpec((1, 1, LANE), f, memory_space=pltpu.SMEM)
    return pl.pallas_call(
        functools.partial(_combine_kernel, n_tok_tiles=n_tok_tiles),
        out_shape=jax.ShapeDtypeStruct((B, nt * TILE, D_MODEL), F32),
        grid=(B, nt),
        in_specs=[tab(lambda b, i: (b * nt + i, 0, 0)),
                  tab(lambda b, i: (jnp.minimum(b * nt + i + 1, n_tok_tiles - 1), 0, 0)),
                  pl.BlockSpec(memory_space=pl.ANY),
                  pl.BlockSpec((TILE, LANE), lambda b, i: (b * nt + i, 0)),
                  pl.BlockSpec((1, TILE, D_MODEL), lambda b, i: (b, i, 0)),
                  pl.BlockSpec((1, 8, D_MODEL), mod_idx),
                  full(g2), full(b2)],
        out_specs=pl.BlockSpec((1, TILE, D_MODEL), lambda b, i: (b, i, 0)),
        scratch_shapes=[pltpu.VMEM((2, LOCAL_ROWS, D_MODEL), BF16), pltpu.SemaphoreType.DMA((2,))],
        compiler_params=pltpu.CompilerParams(
            dimension_semantics=("arbitrary", "arbitrary"), vmem_limit_bytes=VMEM_LIMIT),
        name="combine",
    )(dst_tab, dst_tab, ys, rinfo, x1, mods, g2, b2)


def _rot_cols(w):
    w4 = w.reshape(w.shape[:-1] + (2, 2, QK_ROPE // 4))
    return jnp.stack([-w4[..., 1, :], w4[..., 0, :]], axis=-2).reshape(w.shape)


def _rope_tables():
    rows = SEQ // GRID_W
    row = jnp.repeat(jnp.arange(rows), GRID_W)
    col = jnp.tile(jnp.arange(GRID_W), rows)
    d_axis = QK_ROPE // 2
    inv_freq = jnp.power(ROPE_THETA, -jnp.arange(0, d_axis, 2, dtype=F32) / d_axis)

    def axis_angles(p):
        a = p.astype(F32)[:, None] * inv_freq[None, :]
        return jnp.concatenate([a, a], -1)

    ang = jnp.concatenate([axis_angles(row), axis_angles(col)], -1)
    cos = jnp.concatenate([jnp.ones((CTX_LEN, QK_ROPE), F32), jnp.cos(ang)], 0)
    sin = jnp.concatenate([jnp.zeros((CTX_LEN, QK_ROPE), F32), jnp.sin(ang)], 0)
    scale = math.log2(math.e) / math.sqrt(QK_HEAD)
    z64 = jnp.zeros((SEQ_ALL, QK_NOPE), F32)
    z32 = jnp.zeros((SEQ_ALL, HEAD_PAD - QK_HEAD), F32)
    cq = jnp.concatenate([jnp.full((SEQ_ALL, QK_NOPE), scale, F32), scale * cos, z32], -1)
    sq = jnp.concatenate([z64, scale * sin, z32], -1)
    ck = jnp.concatenate([z64, cos, z32], -1)
    sk = jnp.concatenate([z64, sin, z32], -1)
    return cq, sq, ck, sk


def _prep_layer(w_in, w_uq, w_ukv, conv_w, pool_w, w_out, w_rg, b_rg, w_re, b_re):
    zc = lambda r, c: jnp.zeros((r, c), F32)
    kr = w_in[:, OFF_KR:OFF_CONV]
    pad_l, pad_r = QK_NOPE, HEAD_PAD - QK_HEAD
    kr_arr = jnp.concatenate([zc(D_MODEL, pad_l), kr, zc(D_MODEL, pad_r)], -1)
    krr_arr = jnp.concatenate([zc(D_MODEL, pad_l), _rot_cols(kr), zc(D_MODEL, pad_r)], -1)
    win_ext = jnp.concatenate(
        [w_in[:, :OFF_KR], kr_arr, krr_arr, w_in[:, OFF_CONV:]], -1).astype(BF16)

    wq3 = w_uq.reshape(Q_RANK, N_HEADS, QK_HEAD)
    nope, rope = wq3[..., :QK_NOPE], wq3[..., QK_NOPE:]
    z3 = lambda c: jnp.zeros((Q_RANK, N_HEADS, c), F32)
    q_arr = jnp.concatenate([nope, rope, z3(pad_r)], -1).reshape(Q_RANK, N_HEADS * HEAD_PAD)
    q_rot = jnp.concatenate([z3(pad_l), _rot_cols(rope), z3(pad_r)], -1).reshape(Q_RANK, N_HEADS * HEAD_PAD)
    wuq_ext = jnp.concatenate([q_arr, q_rot], -1).astype(BF16)

    wkv3 = w_ukv.reshape(KV_RANK, N_HEADS, QK_NOPE + V_HEAD)
    zkv = jnp.zeros((KV_RANK, N_HEADS, HEAD_PAD - QK_NOPE), F32)
    k_arr = jnp.concatenate([wkv3[..., :QK_NOPE], zkv], -1).reshape(KV_RANK, N_HEADS * HEAD_PAD)
    v_arr = jnp.concatenate([wkv3[..., QK_NOPE:], zkv], -1).reshape(KV_RANK, N_HEADS * HEAD_PAD)
    wuk, wvt = k_arr.astype(BF16), v_arr.T.astype(BF16)

    cw = jnp.concatenate([conv_w, jnp.zeros((1, D_CONV), F32)], 0)
    gdim = D_POOL // len(POOL_WINDOWS)
    pw = jnp.zeros((D_POOL, D_POOL), F32)
    for g in range(len(POOL_WINDOWS)):
        pw = pw.at[g * gdim:(g + 1) * gdim, g * gdim:(g + 1) * gdim].set(pool_w[g])
    wr = jnp.concatenate([w_re, w_rg, zc(D_MODEL, LANE - N_EXPERTS - N_GROUPS)], -1).astype(BF16)
    br = jnp.concatenate([b_re, b_rg, jnp.zeros((LANE - N_EXPERTS - N_GROUPS,), F32)])[None, :]
    return win_ext, wuq_ext, wuk, wvt, cw, pw.astype(BF16), w_out.astype(BF16), wr, br


def _routing_tables(ctab, totals, n_tok_tiles):
    n_tiles_max = -(-(MAX_USED_CHUNKS * n_tok_tiles + (TILE_CHUNKS - 1) * N_EXPERTS) // TILE_CHUNKS)
    n_tiles_max = -(-n_tiles_max // EXPERT_TILES) * EXPERT_TILES
    n_chunks_max = n_tiles_max * TILE_CHUNKS
    chunks = totals[0, :N_EXPERTS].astype(I32)
    padded = ((chunks + TILE_CHUNKS - 1) // TILE_CHUNKS) * TILE_CHUNKS
    ends = jnp.cumsum(padded)
    starts = ends - padded
    e_of_c = ctab[:, 0, :LOCAL_CHUNKS].astype(I32)
    rel_c = ctab[:, 1, :LOCAL_CHUNKS].astype(I32)
    used = ctab[:, 2, :LOCAL_CHUNKS] > 0.5
    owner = e_of_c[..., None] == jnp.arange(N_EXPERTS, dtype=I32)
    dst = jnp.sum(jnp.where(owner, starts, 0), axis=-1) + rel_c
    local_id = (jnp.arange(n_tok_tiles, dtype=I32)[:, None] * LOCAL_CHUNKS
                + jnp.arange(LOCAL_CHUNKS, dtype=I32)[None, :])
    src = jnp.full((n_chunks_max,), LOCAL_CHUNKS - 1, I32)
    src = src.at[jnp.where(used, dst, n_chunks_max).reshape(-1)].set(local_id.reshape(-1), mode="drop")
    src_tab = src.reshape(n_tiles_max // EXPERT_TILES, 1, EXPERT_TILES * TILE_CHUNKS)
    dst_tab = jnp.where(used, dst, dst[:, 0:1])
    dst_tab = jnp.concatenate(
        [dst_tab, jnp.zeros((n_tok_tiles, LANE - LOCAL_CHUNKS), I32)], -1).reshape(n_tok_tiles, 1, LANE)
    n_valid = ends[-1] // TILE_CHUNKS
    tile_start = jnp.minimum(jnp.arange(n_tiles_max, dtype=I32), n_valid - 1) * TILE_CHUNKS
    tile_expert = jnp.sum((tile_start[:, None] >= ends[None, :]).astype(I32), axis=-1)
    return src_tab, dst_tab, tile_expert.astype(I32), n_valid.astype(I32).reshape(1)


def kernel(x, c, ctx, c_ctx, w_ada, b_ada, w_in, g_q, w_uq, g_kv, w_ukv, conv_w, conv_b, conv_ln_g,
           conv_ln_b, pool_w, pool_scale, w_out, ln1_g, ln1_b, w_router_group, b_router_group,
           w_router_expert, b_router_expert, w_gate, w_up, w_down, ln2_g, ln2_b):
    B = x.shape[0]
    L = w_ada.shape[0]
    tabs = _rope_tables()
    tri = jnp.tril(jnp.ones((TILE, TILE), F32), -1).astype(BF16)
    upper = jnp.triu(jnp.ones((LANE, LANE), F32), 1).astype(BF16)

    cc = jnp.concatenate([c, c_ctx[None, :], jnp.zeros((16 - B - 1, D_MODEL), F32)], 0)
    ada = _ada(cc, w_ada, b_ada)
    mods_all = ada[:, :B + 1].reshape(L, B + 1, 6, D_MODEL)
    mods_all = jnp.concatenate([mods_all, jnp.zeros((L, B + 1, 2, D_MODEL), F32)], 2)

    xa = jnp.concatenate([ctx, x], axis=1)
    row = lambda a: a[None, :]
    for l in range(L):
        last = l == L - 1
        first_tile = 1 if last else 0
        mods = mods_all[l]
        win_ext, wuq_ext, wuk, wvt, cw, pw, wout, wr, br = _prep_layer(
            w_in[l], w_uq[l], w_ukv[l], conv_w[l], pool_w[l], w_out[l], w_router_group[l],
            b_router_group[l], w_router_expert[l], b_router_expert[l])

        q, k, v, u, poolin = _inproj(xa, mods, tabs, win_ext, row(g_q[l]), wuq_ext, row(g_kv[l]),
                                     wuk, wvt)
        attn = _attention(q, k, v)
        attn_ctx = None if last else _attention_ctx(q, k, v)
        x1, xloc, rinfo, ctab, totals = _mixout(
            attn, attn_ctx, u, poolin, xa, mods, cw, row(conv_b[l]), row(conv_ln_g[l]),
            row(conv_ln_b[l]), pw, row(pool_scale[l]), wout, row(ln1_g[l]), row(ln1_b[l]), wr, br,
            tri, upper, first_tile)

        n_tok_tiles = B * (N_TILES - first_tile)
        src_tab, dst_tab, tile_expert, n_valid = _routing_tables(ctab, totals, n_tok_tiles)
        ys = _experts(tile_expert, n_valid, src_tab, xloc, w_gate, w_up, w_down, l)
        xa = _combine(dst_tab, ys, rinfo, x1, mods, row(ln2_g[l]), row(ln2_b[l]), first_tile)
    return xa
```

```python
import functools
import math

import jax
import jax.numpy as jnp
from jax import lax
from jax.experimental import pallas as pl
from jax.experimental.pallas import tpu as pltpu

D_MODEL = 1024
SEQ = 4096
CTX_LEN = 256
SEQ_ALL = CTX_LEN + SEQ
GRID_W = 64
N_HEADS = 8
QK_NOPE = 64
QK_ROPE = 32
QK_HEAD = QK_NOPE + QK_ROPE
V_HEAD = 64
Q_RANK = 384
KV_RANK = 256
D_CONV = 256
CONV_WIDTH = 31
D_POOL = 256
POOL_WINDOWS = (2, 4, 8, 16)
ROPE_THETA = 10000.0
OFF_KV = Q_RANK
OFF_KR = OFF_KV + KV_RANK
OFF_CONV = OFF_KR + QK_ROPE
OFF_POOL = OFF_CONV + 2 * D_CONV
N_GROUPS = 4
EXPERTS_PER_GROUP = 8
N_EXPERTS = 32
D_EXPERT = 256
LN_EPS = 1e-5
RMS_EPS = 1e-6
DEPTH = 2
DEEPNORM_ALPHA = (2 * DEPTH) ** 0.25

LANE = 128
SUBLANE = 8
TILE = 256
HALO = 16
N_TILES = SEQ_ALL // TILE
HEAD_PAD = 128
KEY_CHUNK = 256
VMEM_LIMIT = 56 * 1024 * 1024

CHUNK = 2 * SUBLANE
MAX_USED_CHUNKS = (2 * TILE + (CHUNK - 1) * N_EXPERTS) // CHUNK
LOCAL_ROWS = 1024
LOCAL_CHUNKS = LOCAL_ROWS // CHUNK
assert MAX_USED_CHUNKS < LOCAL_CHUNKS <= LANE
TILE_CHUNKS = TILE // CHUNK
EXPERT_TILES = 2

XOFF_Q = 0
XOFF_KV = 384
XOFF_KR = 640
XOFF_KRR = 768
XOFF_CONV = 896
XOFF_POOL = 1408
D_IN_EXT = 1664

BF16 = jnp.bfloat16
F32 = jnp.float32
I32 = jnp.int32


def _dot(a, b):
    return jnp.dot(a, b, preferred_element_type=F32)


def _silu(x):
    return x * jax.nn.sigmoid(x)


def _layer_norm_rows(x, g, b):
    mu = jnp.mean(x, axis=-1, keepdims=True)
    xc = x - mu
    var = jnp.mean(xc * xc, axis=-1, keepdims=True)
    return xc * lax.rsqrt(var + LN_EPS) * g + b


def _ada_kernel(cc_ref, w_ref, b_ref, o_ref):
    a = _silu(cc_ref[...])
    o_ref[0] = jnp.dot(a, w_ref[0], preferred_element_type=F32,
                       precision=lax.Precision.HIGHEST) + b_ref[0]


def _ada(cc, w_ada, b_ada):
    L = w_ada.shape[0]
    n = w_ada.shape[2]
    bn = 512
    return pl.pallas_call(
        _ada_kernel,
        out_shape=jax.ShapeDtypeStruct((L, cc.shape[0], n), F32),
        grid=(L, n // bn),
        in_specs=[
            pl.BlockSpec(cc.shape, lambda l, j: (0, 0)),
            pl.BlockSpec((1, D_MODEL, bn), lambda l, j: (l, 0, j)),
            pl.BlockSpec((1, 1, bn), lambda l, j: (l, 0, j)),
        ],
        out_specs=pl.BlockSpec((1, cc.shape[0], bn), lambda l, j: (l, 0, j)),
        compiler_params=pltpu.CompilerParams(
            dimension_semantics=("parallel", "parallel"), vmem_limit_bytes=VMEM_LIMIT),
        name="ada",
    )(cc, w_ada, b_ada.reshape(L, 1, n))


def _inproj_kernel(xc_ref, xl_ref, mod_ref, cq_ref, sq_ref, ck_ref, sk_ref, win_ref, gq_ref, wuq_ref,
                   gkv_ref, wuk_ref, wvt_ref, q_ref, k_ref, vt_ref, u_ref, pool_ref):
    x = jnp.where(pl.program_id(1) == 0, xc_ref[0], xl_ref[0])
    sh = mod_ref[0, 0:1, :]
    sc = mod_ref[0, 1:2, :]
    h = (x * (1.0 + sc) + sh).astype(BF16)
    p = _dot(h, win_ref[...])

    pq = p[:, XOFF_Q:XOFF_KV]
    qn = pq * lax.rsqrt(jnp.mean(pq * pq, axis=-1, keepdims=True) + RMS_EPS) * gq_ref[...]
    q2 = _dot(qn.astype(BF16), wuq_ref[...])
    cq = cq_ref[...]
    sq = sq_ref[...]
    nq = N_HEADS * HEAD_PAD
    for hd in range(N_HEADS):
        lo = hd * HEAD_PAD
        qh = q2[:, lo:lo + HEAD_PAD] * cq + q2[:, nq + lo:nq + lo + HEAD_PAD] * sq
        q_ref[0, :, lo:lo + HEAD_PAD] = qh.astype(BF16)

    pkv = p[:, XOFF_KV:XOFF_KR]
    kvn = pkv * lax.rsqrt(jnp.mean(pkv * pkv, axis=-1, keepdims=True) + RMS_EPS) * gkv_ref[...]
    kvn = kvn.astype(BF16)
    kn = _dot(kvn, wuk_ref[...])
    kr = p[:, XOFF_KR:XOFF_KRR] * ck_ref[...] + p[:, XOFF_KRR:XOFF_CONV] * sk_ref[...]
    for hd in range(N_HEADS):
        lo = hd * HEAD_PAD
        k_ref[0, :, lo:lo + HEAD_PAD] = (kn[:, lo:lo + HEAD_PAD] + kr).astype(BF16)
    vt = lax.dot_general(wvt_ref[...], kvn, (((1,), (1,)), ((), ())), preferred_element_type=F32)
    vrow = lax.broadcasted_iota(I32, (N_HEADS * HEAD_PAD, TILE), 0)
    vt = jnp.where((vrow & (HEAD_PAD - 1)) == V_HEAD, 1.0, vt).astype(BF16)
    for hd in range(N_HEADS):
        vt_ref[0, hd] = vt[hd * HEAD_PAD:(hd + 1) * HEAD_PAD, :]

    a = p[:, XOFF_CONV:XOFF_CONV + D_CONV]
    gt = p[:, XOFF_CONV + D_CONV:XOFF_POOL]
    u_ref[0] = (a * jax.nn.sigmoid(gt)).astype(BF16)
    pool_ref[0] = p[:, XOFF_POOL:]


def _inproj(xc, xl, lat_off, mods, tabs, win_ext, gq, wuq_ext, gkv, wuk, wvt):
    B = xl.shape[0]
    nb = mods.shape[0] - 1

    def mod_idx(b, i):
        return (jnp.where(i == 0, nb, b), 0, 0)

    tab_spec = pl.BlockSpec((TILE, LANE), lambda b, i: (i, 0))
    full = lambda a: pl.BlockSpec(a.shape, lambda b, i: (0,) * a.ndim)
    tok = lambda w: pl.BlockSpec((1, TILE, w), lambda b, i: (b, i, 0))
    hw = N_HEADS * HEAD_PAD
    return pl.pallas_call(
        _inproj_kernel,
        out_shape=(
            jax.ShapeDtypeStruct((B, SEQ_ALL, hw), BF16),
            jax.ShapeDtypeStruct((B, SEQ_ALL, hw), BF16),
            jax.ShapeDtypeStruct((B, N_HEADS, HEAD_PAD, SEQ_ALL), BF16),
            jax.ShapeDtypeStruct((B, SEQ_ALL, D_CONV), BF16),
            jax.ShapeDtypeStruct((B, SEQ_ALL, D_POOL), F32),
        ),
        grid=(B, N_TILES),
        in_specs=[pl.BlockSpec((1, TILE, D_MODEL), lambda b, i: (b, 0, 0)),
                  pl.BlockSpec((1, TILE, D_MODEL), lambda b, i: (b, jnp.maximum(i + lat_off, 0), 0)),
                  pl.BlockSpec((1, 8, D_MODEL), mod_idx),
                  tab_spec, tab_spec, tab_spec, tab_spec,
                  full(win_ext), full(gq), full(wuq_ext), full(gkv), full(wuk), full(wvt)],
        out_specs=(tok(hw), tok(hw),
                   pl.BlockSpec((1, N_HEADS, HEAD_PAD, TILE), lambda b, i: (b, 0, 0, i)),
                   tok(D_CONV), tok(D_POOL)),
        compiler_params=pltpu.CompilerParams(
            dimension_semantics=("parallel", "parallel"), vmem_limit_bytes=VMEM_LIMIT),
        name="inproj",
    )(xc, xl, mods, *tabs, win_ext, gq, wuq_ext, gkv, wuk, wvt)


def _scores_t(q, k_ref, n_keys):
    lane = lax.broadcasted_iota(I32, (TILE, 2 * HEAD_PAD), 1)
    qbd = jnp.concatenate([jnp.where(lane < HEAD_PAD, q, jnp.zeros_like(q)),
                           jnp.where(lane >= HEAD_PAD, q, jnp.zeros_like(q))], axis=0)
    return lax.dot_general(k_ref[0, 0:n_keys, :], qbd, (((1,), (1,)), ((), ())),
                           preferred_element_type=F32)


def _store_scores(s_ref, m_ref, st):
    s_ref[...] = st
    m_ref[...] = jnp.broadcast_to(jnp.max(st, axis=0, keepdims=True), m_ref.shape)


def _softmax_values_t(s_ref, m_ref, vt_ref, p_ref, n_keys):
    m = m_ref[0:1, :]
    for kc in range(0, n_keys, KEY_CHUNK):
        p_ref[kc:kc + KEY_CHUNK, :] = jnp.exp2(s_ref[kc:kc + KEY_CHUNK, :] - m).astype(BF16)
    outs = []
    for hd in range(2):
        ot = _dot(vt_ref[0, hd, :, 0:n_keys], p_ref[0:n_keys, hd * TILE:(hd + 1) * TILE])
        outs.append(ot[0:V_HEAD, :] / ot[V_HEAD:V_HEAD + 1, :])
    return jnp.concatenate(outs, axis=0).T.astype(BF16)


def _attn_kernel(qc_ref, qn_ref, k_ref, vt_ref, o_ref, sa_ref, sb_ref, ma_ref, mb_ref, p_ref):
    i = pl.program_id(2)

    @pl.when(i == 0)
    def _():
        _store_scores(sa_ref, ma_ref, _scores_t(qc_ref[0], k_ref, SEQ_ALL))

    def step(cur_ref, cur_m_ref, nxt_ref, nxt_m_ref):
        _store_scores(nxt_ref, nxt_m_ref, _scores_t(qn_ref[0], k_ref, SEQ_ALL))
        o_ref[0] = _softmax_values_t(cur_ref, cur_m_ref, vt_ref, p_ref, SEQ_ALL)

    parity = lax.rem(i, 2)

    @pl.when(parity == 0)
    def _():
        step(sa_ref, ma_ref, sb_ref, mb_ref)

    @pl.when(parity == 1)
    def _():
        step(sb_ref, mb_ref, sa_ref, ma_ref)


def _attention(q, k, v):
    B = q.shape[0]
    n_pairs = N_HEADS // 2
    pw = 2 * HEAD_PAD
    nq = SEQ // TILE
    assert nq % 2 == 0
    return pl.pallas_call(
        _attn_kernel,
        out_shape=jax.ShapeDtypeStruct((B, SEQ, N_HEADS * V_HEAD), BF16),
        grid=(B, n_pairs, nq),
        in_specs=[
            pl.BlockSpec((1, TILE, pw), lambda b, hp, i: (b, i + 1, hp)),
            pl.BlockSpec((1, TILE, pw), lambda b, hp, i: (b, jnp.minimum(i + 2, nq), hp)),
            pl.BlockSpec((1, SEQ_ALL, pw), lambda b, hp, i: (b, 0, hp)),
            pl.BlockSpec((1, 2, HEAD_PAD, SEQ_ALL), lambda b, hp, i: (b, hp, 0, 0)),
        ],
        out_specs=pl.BlockSpec((1, TILE, 2 * V_HEAD), lambda b, hp, i: (b, i, hp)),
        scratch_shapes=[pltpu.VMEM((SEQ_ALL, 2 * TILE), F32), pltpu.VMEM((SEQ_ALL, 2 * TILE), F32),
                        pltpu.VMEM((SUBLANE, 2 * TILE), F32), pltpu.VMEM((SUBLANE, 2 * TILE), F32),
                        pltpu.VMEM((SEQ_ALL, 2 * TILE), BF16)],
        compiler_params=pltpu.CompilerParams(
            dimension_semantics=("arbitrary", "arbitrary", "arbitrary"),
            vmem_limit_bytes=VMEM_LIMIT),
        name="attn",
    )(q, q, k, v)


def _attn_ctx_kernel(q_ref, k_ref, vt_ref, o_ref, s_ref, m_ref, p_ref):
    _store_scores(s_ref, m_ref, _scores_t(q_ref[0], k_ref, CTX_LEN))
    o_ref[0] = _softmax_values_t(s_ref, m_ref, vt_ref, p_ref, CTX_LEN)


def _attention_ctx(q, k, v):
    B = q.shape[0]
    pw = 2 * HEAD_PAD
    blk = pl.BlockSpec((1, CTX_LEN, pw), lambda b, hp: (b, 0, hp))
    return pl.pallas_call(
        _attn_ctx_kernel,
        out_shape=jax.ShapeDtypeStruct((B, CTX_LEN, N_HEADS * V_HEAD), BF16),
        grid=(B, N_HEADS // 2),
        in_specs=[blk, blk, pl.BlockSpec((1, 2, HEAD_PAD, CTX_LEN), lambda b, hp: (b, hp, 0, 0))],
        out_specs=pl.BlockSpec((1, CTX_LEN, 2 * V_HEAD), lambda b, hp: (b, 0, hp)),
        scratch_shapes=[pltpu.VMEM((CTX_LEN, 2 * TILE), F32), pltpu.VMEM((SUBLANE, 2 * TILE), F32),
                        pltpu.VMEM((CTX_LEN, 2 * TILE), BF16)],
        compiler_params=pltpu.CompilerParams(
            dimension_semantics=("parallel", "parallel"), vmem_limit_bytes=VMEM_LIMIT),
        name="attn_ctx",
    )(q, k, v)


def _mixout_kernel(*refs, first_tile):
    if first_tile == 0:
        attn_ref, actx_ref = refs[0], refs[1]
        refs = refs[2:]
    else:
        attn_ref, actx_ref = refs[0], None
        refs = refs[1:]
    (u_ref, ul_ref, ur_ref, pm_ref, pl_ref, pr_ref, xc_ref, xl_ref, mod_ref,
     cw_ref, cb_ref, cg_ref, cbeta_ref, pw_ref, ps_ref, wout_ref, g1_ref, b1_ref,
     wr_ref, br_ref, tri_ref, upper_ref,
     x1_ref, xloc_ref, rinfo_ref, ctab_ref, tot_ref,
     winu, winp, shu, shp, cnt) = refs
    i = pl.program_id(1) + first_tile
    left_ok = i >= 2
    right_ok = (i >= 1) & (i <= N_TILES - 2)

    @pl.when((pl.program_id(0) == 0) & (pl.program_id(1) == 0))
    def _():
        cnt[...] = jnp.zeros_like(cnt)

    zu = jnp.zeros((HALO, D_CONV), F32)
    winu[0:HALO, :] = jnp.where(left_ok, ul_ref[0].astype(F32), zu)
    winu[HALO:HALO + TILE, :] = u_ref[0].astype(F32)
    winu[HALO + TILE:, :] = jnp.where(right_ok, ur_ref[0].astype(F32), zu)
    winp[0:HALO, :] = jnp.where(left_ok, pl_ref[0], zu)
    winp[HALO:HALO + TILE, :] = pm_ref[0]
    winp[HALO + TILE:, :] = jnp.where(right_ok, pr_ref[0], zu)
    ph_rows = TILE + 2 * HALO - SUBLANE
    for ph in range(SUBLANE):
        shu[ph] = winu[ph:ph + ph_rows, :]
        shp[ph] = winp[ph:ph + ph_rows, :]

    def shifted(buf, off):
        ph, base = off % SUBLANE, off - off % SUBLANE
        return buf[ph, base:base + TILE, :]

    acc = jnp.zeros((TILE, D_CONV), F32)
    for j in range(CONV_WIDTH):
        acc = acc + shifted(shu, HALO - CONV_WIDTH // 2 + j) * cw_ref[j:j + 1, :]
    conv = _silu(_layer_norm_rows(acc + cb_ref[...], cg_ref[...], cbeta_ref[...]))

    def pw(o):
        return shifted(shp, HALO + o)

    centre = pw(0)
    s2 = pw(-1) + centre
    s4 = s2 + pw(-2) + pw(1)
    s8 = s4 + pw(-4) + pw(-3) + pw(2) + pw(3)
    s16 = s8 + pw(-8) + pw(-7) + pw(-6) + pw(-5) + pw(4) + pw(5) + pw(6) + pw(7)
    row = lax.broadcasted_iota(I32, (TILE, 1), 0)
    seg_len = jnp.where(i == 0, CTX_LEN, SEQ)
    t = jnp.where(i == 0, 0, (i - 1) * TILE) + row

    def mean(sw, w):
        hi = jnp.minimum(t + (w // 2 - 1), seg_len - 1)
        lo = jnp.maximum(t - w // 2, 0)
        return sw / (hi - lo + 1).astype(F32)

    lane_p = lax.broadcasted_iota(I32, (1, D_POOL), 1)
    gdim = D_POOL // len(POOL_WINDOWS)
    mixed = jnp.where(lane_p < gdim, mean(s2, 2),
                      jnp.where(lane_p < 2 * gdim, mean(s4, 4),
                                jnp.where(lane_p < 3 * gdim, mean(s8, 8), mean(s16, 16)))) - centre
    pool = _dot(mixed.astype(BF16), pw_ref[...]) * ps_ref[...]

    d_attn = N_HEADS * V_HEAD
    attn = attn_ref[0]
    if actx_ref is not None:
        attn = jnp.where(i == 0, actx_ref[0], attn)
    mix = (_dot(attn, wout_ref[0:d_attn, :])
           + _dot(conv.astype(BF16), wout_ref[d_attn:d_attn + D_CONV, :])
           + _dot(pool.astype(BF16), wout_ref[d_attn + D_CONV:, :]))
    gate1 = mod_ref[0, 2:3, :]
    x_in = jnp.where(i == 0, xc_ref[0], xl_ref[0])
    x1 = _layer_norm_rows(DEEPNORM_ALPHA * x_in + gate1 * mix, g1_ref[...], b1_ref[...])
    x1_ref[0] = x1

    h2 = (x1 * (1.0 + mod_ref[0, 4:5, :]) + mod_ref[0, 3:4, :]).astype(BF16)
    logits = _dot(h2, wr_ref[...]) + br_ref[...]
    lane = lax.broadcasted_iota(I32, (TILE, LANE), 1)
    neg = jnp.float32(-jnp.inf)
    big = jnp.int32(1 << 20)
    is_g = (lane >= N_EXPERTS) & (lane < N_EXPERTS + N_GROUPS)
    gl = jnp.where(is_g, logits, neg)
    gmax = jnp.max(gl, axis=-1, keepdims=True)
    g_sel = jnp.min(jnp.where(gl == gmax, lane - N_EXPERTS, big), axis=-1, keepdims=True)
    p_sel = 1.0 / jnp.sum(jnp.exp(gl - gmax), axis=-1, keepdims=True)
    in_grp = (lane >= g_sel * EXPERTS_PER_GROUP) & (lane < (g_sel + 1) * EXPERTS_PER_GROUP)
    el = jnp.where(in_grp, logits, neg)
    v1 = jnp.max(el, axis=-1, keepdims=True)
    i1 = jnp.min(jnp.where(el == v1, lane, big), axis=-1, keepdims=True)
    el2 = jnp.where(lane == i1, neg, el)
    v2 = jnp.max(el2, axis=-1, keepdims=True)
    i2 = jnp.min(jnp.where(el2 == v2, lane, big), axis=-1, keepdims=True)
    e2 = jnp.exp(v2 - v1)
    w1 = p_sel / (1.0 + e2)
    w2 = p_sel * e2 / (1.0 + e2)

    oh1 = (lane == i1).astype(F32)
    oh2 = (lane == i2).astype(F32)
    both = oh1 + oh2
    n_e = jnp.sum(both, axis=0, keepdims=True)
    chunks_e = jnp.floor((n_e + (CHUNK - 1)) * (1.0 / CHUNK))
    chunks8 = jnp.broadcast_to(chunks_e, (SUBLANE, LANE))
    start_e = _dot(chunks8.astype(BF16), upper_ref[...])[0:1, :]
    slot_mat = _dot(tri_ref[...], both.astype(BF16)) + CHUNK * start_e
    s1 = jnp.sum(oh1 * slot_mat, axis=-1, keepdims=True)
    s2_ = jnp.sum(oh2 * slot_mat, axis=-1, keepdims=True)
    fields = (w1, w2, s1, s2_)
    rinfo = jnp.zeros((TILE, LANE), F32)
    for n, f in enumerate(fields):
        rinfo = jnp.where(lane == n, f, rinfo)
    rinfo_ref[...] = rinfo

    slot_lane = lax.broadcasted_iota(I32, (TILE, LOCAL_ROWS), 1).astype(F32)
    perm = ((slot_lane == s1).astype(F32) + (slot_lane == s2_).astype(F32)).astype(BF16)
    sorted_rows = lax.dot_general(perm, h2, (((0,), (0,)), ((), ())), preferred_element_type=F32)
    xloc_ref[...] = sorted_rows.astype(BF16)

    done_e = cnt[0:1, :]
    cidx = lax.broadcasted_iota(I32, (LANE, LANE), 0).astype(F32)
    lane_sq = lax.broadcasted_iota(I32, (LANE, LANE), 1)
    owns = ((cidx >= start_e) & (cidx < start_e + chunks_e)).astype(F32)
    e_of_c = jnp.sum(owns * lane_sq.astype(F32), axis=-1, keepdims=True)
    rel_c = jnp.sum(owns * (done_e + cidx - start_e), axis=-1, keepdims=True)
    used_c = jnp.sum(owns, axis=-1, keepdims=True)
    tab = jnp.where(lane_sq == 0, e_of_c, jnp.where(lane_sq == 1, rel_c,
                                                    jnp.where(lane_sq == 2, used_c, 0.0)))
    ctab_ref[0] = tab.T[0:SUBLANE, :]
    cnt[...] = cnt[...] + chunks_e
    tot_ref[...] = cnt[...]


def _mixout(attn, attn_ctx, u, poolin, xc, xl, lat_off, mods, cw, cb, cg, cbeta, pw, ps, wout, g1, b1, wr, br,
            tri, upper, first_tile):
    B = xl.shape[0]
    nt = N_TILES - first_tile
    nb = mods.shape[0] - 1
    hb = TILE // HALO
    n_hb = SEQ_ALL // HALO

    def mod_idx(b, i):
        return (jnp.where(i + first_tile == 0, nb, b), 0, 0)

    tok = lambda w: pl.BlockSpec((1, TILE, w), lambda b, i: (b, i + first_tile, 0))
    own = lambda w: pl.BlockSpec((1, TILE, w), lambda b, i: (b, i, 0))
    left = lambda w: pl.BlockSpec(
        (1, HALO, w), lambda b, i: (b, jnp.maximum((i + first_tile) * hb - 1, 0), 0))
    right = lambda w: pl.BlockSpec(
        (1, HALO, w), lambda b, i: (b, jnp.minimum((i + first_tile + 1) * hb, n_hb - 1), 0))
    full = lambda a: pl.BlockSpec(a.shape, lambda b, i: (0,) * a.ndim)
    d_attn = N_HEADS * V_HEAD
    attn_specs = [pl.BlockSpec((1, TILE, d_attn),
                               lambda b, i: (b, jnp.maximum(i + first_tile - 1, 0), 0))]
    attn_args = [attn]
    if first_tile == 0:
        attn_specs.append(pl.BlockSpec((1, TILE, d_attn), lambda b, i: (b, 0, 0)))
        attn_args.append(attn_ctx)
    n_tok_tiles = B * nt
    ph_rows = TILE + 2 * HALO - SUBLANE
    return pl.pallas_call(
        functools.partial(_mixout_kernel, first_tile=first_tile),
        out_shape=(
            jax.ShapeDtypeStruct((B, nt * TILE, D_MODEL), F32),
            jax.ShapeDtypeStruct((n_tok_tiles * LOCAL_ROWS, D_MODEL), BF16),
            jax.ShapeDtypeStruct((n_tok_tiles * TILE, LANE), F32),
            jax.ShapeDtypeStruct((n_tok_tiles, SUBLANE, LANE), F32),
            jax.ShapeDtypeStruct((SUBLANE, LANE), F32),
        ),
        grid=(B, nt),
        in_specs=attn_specs + [
            tok(D_CONV), left(D_CONV), right(D_CONV),
            tok(D_POOL), left(D_POOL), right(D_POOL),
            pl.BlockSpec((1, TILE, D_MODEL), lambda b, i: (b, 0, 0)),
            pl.BlockSpec((1, TILE, D_MODEL),
                         lambda b, i: (b, jnp.maximum(i + first_tile + lat_off, 0), 0)),
            pl.BlockSpec((1, 8, D_MODEL), mod_idx),
            full(cw), full(cb), full(cg), full(cbeta), full(pw), full(ps), full(wout),
            full(g1), full(b1), full(wr), full(br), full(tri), full(upper)],
        out_specs=(own(D_MODEL),
                   pl.BlockSpec((LOCAL_ROWS, D_MODEL), lambda b, i: (b * nt + i, 0)),
                   pl.BlockSpec((TILE, LANE), lambda b, i: (b * nt + i, 0)),
                   pl.BlockSpec((1, SUBLANE, LANE), lambda b, i: (b * nt + i, 0, 0)),
                   pl.BlockSpec((SUBLANE, LANE), lambda b, i: (0, 0))),
        scratch_shapes=[pltpu.VMEM((TILE + 2 * HALO, D_CONV), F32),
                        pltpu.VMEM((TILE + 2 * HALO, D_POOL), F32),
                        pltpu.VMEM((SUBLANE, ph_rows, D_CONV), F32),
                        pltpu.VMEM((SUBLANE, ph_rows, D_POOL), F32),
                        pltpu.VMEM((SUBLANE, LANE), F32)],
        compiler_params=pltpu.CompilerParams(
            dimension_semantics=("arbitrary", "arbitrary"), vmem_limit_bytes=VMEM_LIMIT),
        name="mixout",
    )(*attn_args, u, u, u, poolin, poolin, poolin, xc, xl, mods, cw, cb, cg, cbeta, pw, ps, wout,
      g1, b1, wr, br, tri, upper)


def _chunk_copy(src, src_chunk, buf, dst_chunk, sem):
    src_row = src_chunk * CHUNK
    if not isinstance(src_row, int):
        src_row = pl.multiple_of(src_row, CHUNK)
    return pltpu.make_async_copy(
        src.at[pl.ds(src_row, CHUNK)], buf.at[pl.ds(dst_chunk * CHUNK, CHUNK)], sem)


def _start_gather(src, table_ref, n_chunks, buf, sem):
    for c in range(n_chunks):
        _chunk_copy(src, table_ref[0, 0, c], buf, c, sem).start()


def _wait_gather(src, n_chunks, buf, sem):
    for c in range(n_chunks):
        _chunk_copy(src, 0, buf, c, sem).wait()


def _double_buffered_gather(step, n_steps, src, cur_tab, nxt_tab, n_chunks, bufs, sems, compute):
    @pl.when(step == 0)
    def _():
        _start_gather(src, cur_tab, n_chunks, bufs[0], sems.at[0])

    for par in range(2):
        @pl.when(lax.rem(step, 2) == par)
        def _(par=par):
            _wait_gather(src, n_chunks, bufs[par], sems.at[par])
            _start_gather(src, nxt_tab, n_chunks, bufs[1 - par], sems.at[1 - par])
            compute(bufs[par])

            @pl.when(step == n_steps - 1)
            def _():
                _wait_gather(src, n_chunks, bufs[1 - par], sems.at[1 - par])


def _experts_kernel(te_ref, cur_ref, nxt_ref, xloc_ref, wga_ref, wua_ref, wda_ref,
                    wgb_ref, wub_ref, wdb_ref, ys_ref,
                    xbuf_a, xbuf_b, cga, cua, cda, cgb, cub, cdb, sem, *, n_steps):
    i = pl.program_id(0)
    w_refs = ((wga_ref, wua_ref, wda_ref), (wgb_ref, wub_ref, wdb_ref))
    caches = ((cga, cua, cda), (cgb, cub, cdb))

    for t in range(EXPERT_TILES):
        tile = EXPERT_TILES * i + t
        prev = jnp.maximum(tile - EXPERT_TILES, 0)

        @pl.when((i == 0) | (te_ref[tile] != te_ref[prev]))
        def _(t=t):
            for cache, w_ref in zip(caches[t], w_refs[t]):
                cache[...] = w_ref[0, 0].astype(BF16)

    def compute(xbuf):
        for t in range(EXPERT_TILES):
            wg, wu, wd = caches[t]
            xt = xbuf[t * TILE:(t + 1) * TILE, :]
            hid = (_silu(_dot(xt, wg[...])) * _dot(xt, wu[...])).astype(BF16)
            ys_ref[t * TILE:(t + 1) * TILE, :] = _dot(hid, wd[...]).astype(BF16)

    _double_buffered_gather(i, n_steps, xloc_ref, cur_ref, nxt_ref, EXPERT_TILES * TILE_CHUNKS,
                            (xbuf_a, xbuf_b), sem, compute)


def _experts(tile_expert, src_tab, xloc, w_gate, w_up, w_down, layer):
    n_steps = src_tab.shape[0]
    rows = EXPERT_TILES * TILE

    def w_idx(t):
        return lambda i, te: (layer, te[EXPERT_TILES * i + t], 0, 0)

    tab = lambda f: pl.BlockSpec((1, 1, EXPERT_TILES * TILE_CHUNKS), f, memory_space=pltpu.SMEM)
    w_specs, w_cache = [], []
    for t in range(EXPERT_TILES):
        w_specs += [pl.BlockSpec((1, 1, D_MODEL, D_EXPERT), w_idx(t)),
                    pl.BlockSpec((1, 1, D_MODEL, D_EXPERT), w_idx(t)),
                    pl.BlockSpec((1, 1, D_EXPERT, D_MODEL), w_idx(t))]
        w_cache += [pltpu.VMEM((D_MODEL, D_EXPERT), BF16), pltpu.VMEM((D_MODEL, D_EXPERT), BF16),
                    pltpu.VMEM((D_EXPERT, D_MODEL), BF16)]
    return pl.pallas_call(
        functools.partial(_experts_kernel, n_steps=n_steps),
        out_shape=jax.ShapeDtypeStruct((n_steps * rows, D_MODEL), BF16),
        grid_spec=pltpu.PrefetchScalarGridSpec(
            num_scalar_prefetch=1,
            grid=(n_steps,),
            in_specs=[tab(lambda i, te: (i, 0, 0)),
                      tab(lambda i, te: (jnp.minimum(i + 1, n_steps - 1), 0, 0)),
                      pl.BlockSpec(memory_space=pl.ANY)] + w_specs,
            out_specs=pl.BlockSpec((rows, D_MODEL), lambda i, te: (i, 0)),
            scratch_shapes=[pltpu.VMEM((rows, D_MODEL), BF16), pltpu.VMEM((rows, D_MODEL), BF16)]
            + w_cache + [pltpu.SemaphoreType.DMA((2,))],
        ),
        compiler_params=pltpu.CompilerParams(
            dimension_semantics=("arbitrary",), vmem_limit_bytes=VMEM_LIMIT),
        name="experts",
    )(tile_expert, src_tab, src_tab, xloc, *([w_gate, w_up, w_down] * EXPERT_TILES))


def _combine_kernel(cur_ref, nxt_ref, ys_ref, rinfo_ref, x_ref, mod_ref, g2_ref, b2_ref, o_ref,
                    ybuf_a, ybuf_b, sem, *, n_tok_tiles):
    g = pl.program_id(0) * pl.num_programs(1) + pl.program_id(1)

    def compute(ybuf):
        slot_lane = lax.broadcasted_iota(I32, (TILE, LOCAL_ROWS), 1).astype(F32)
        sel = (jnp.where(slot_lane == rinfo_ref[:, 2:3], rinfo_ref[:, 0:1], 0.0)
               + jnp.where(slot_lane == rinfo_ref[:, 3:4], rinfo_ref[:, 1:2], 0.0))
        y = _dot(sel.astype(BF16), ybuf[...])
        gate2 = mod_ref[0, 5:6, :]
        o_ref[0] = _layer_norm_rows(DEEPNORM_ALPHA * x_ref[0] + gate2 * y, g2_ref[...], b2_ref[...])

    _double_buffered_gather(g, n_tok_tiles, ys_ref, cur_ref, nxt_ref, LOCAL_CHUNKS,
                            (ybuf_a, ybuf_b), sem, compute)


def _combine(dst_tab, ys, rinfo, x1, mods, g2, b2, first_tile):
    B = x1.shape[0]
    nt = N_TILES - first_tile
    nb = mods.shape[0] - 1
    n_tok_tiles = B * nt

    def mod_idx(b, i):
        return (jnp.where(i + first_tile == 0, nb, b), 0, 0)

    full = lambda a: pl.BlockSpec(a.shape, lambda b, i: (0,) * a.ndim)
    tab = lambda f: pl.BlockSpec((1, 1, LANE), f, memory_space=pltpu.SMEM)
    return pl.pallas_call(
        functools.partial(_combine_kernel, n_tok_tiles=n_tok_tiles),
        out_shape=jax.ShapeDtypeStruct((B, nt * TILE, D_MODEL), F32),
        grid=(B, nt),
        in_specs=[tab(lambda b, i: (b * nt + i, 0, 0)),
                  tab(lambda b, i: (jnp.minimum(b * nt + i + 1, n_tok_tiles - 1), 0, 0)),
                  pl.BlockSpec(memory_space=pl.ANY),
                  pl.BlockSpec((TILE, LANE), lambda b, i: (b * nt + i, 0)),
                  pl.BlockSpec((1, TILE, D_MODEL), lambda b, i: (b, i, 0)),
                  pl.BlockSpec((1, 8, D_MODEL), mod_idx),
                  full(g2), full(b2)],
        out_specs=pl.BlockSpec((1, TILE, D_MODEL), lambda b, i: (b, i, 0)),
        scratch_shapes=[pltpu.VMEM((LOCAL_ROWS, D_MODEL), BF16), pltpu.VMEM((LOCAL_ROWS, D_MODEL), BF16),
                        pltpu.SemaphoreType.DMA((2,))],
        compiler_params=pltpu.CompilerParams(
            dimension_semantics=("arbitrary", "arbitrary"), vmem_limit_bytes=VMEM_LIMIT),
        name="combine",
    )(dst_tab, dst_tab, ys, rinfo, x1, mods, g2, b2)


def _rot_cols(w):
    w4 = w.reshape(w.shape[:-1] + (2, 2, QK_ROPE // 4))
    return jnp.stack([-w4[..., 1, :], w4[..., 0, :]], axis=-2).reshape(w.shape)


def _rope_tables():
    rows = SEQ // GRID_W
    row = jnp.repeat(jnp.arange(rows), GRID_W)
    col = jnp.tile(jnp.arange(GRID_W), rows)
    d_axis = QK_ROPE // 2
    inv_freq = jnp.power(ROPE_THETA, -jnp.arange(0, d_axis, 2, dtype=F32) / d_axis)

    def axis_angles(p):
        a = p.astype(F32)[:, None] * inv_freq[None, :]
        return jnp.concatenate([a, a], -1)

    ang = jnp.concatenate([axis_angles(row), axis_angles(col)], -1)
    cos = jnp.concatenate([jnp.ones((CTX_LEN, QK_ROPE), F32), jnp.cos(ang)], 0)
    sin = jnp.concatenate([jnp.zeros((CTX_LEN, QK_ROPE), F32), jnp.sin(ang)], 0)
    scale = math.log2(math.e) / math.sqrt(QK_HEAD)
    z64 = jnp.zeros((SEQ_ALL, QK_NOPE), F32)
    z32 = jnp.zeros((SEQ_ALL, HEAD_PAD - QK_HEAD), F32)
    cq = jnp.concatenate([jnp.full((SEQ_ALL, QK_NOPE), scale, F32), scale * cos, z32], -1)
    sq = jnp.concatenate([z64, scale * sin, z32], -1)
    ck = jnp.concatenate([z64, cos, z32], -1)
    sk = jnp.concatenate([z64, sin, z32], -1)
    return cq, sq, ck, sk


def _prep_layer(w_in, w_uq, w_ukv, conv_w, pool_w, w_out, w_rg, b_rg, w_re, b_re):
    zc = lambda r, c: jnp.zeros((r, c), F32)
    kr = w_in[:, OFF_KR:OFF_CONV]
    pad_l, pad_r = QK_NOPE, HEAD_PAD - QK_HEAD
    kr_arr = jnp.concatenate([zc(D_MODEL, pad_l), kr, zc(D_MODEL, pad_r)], -1)
    krr_arr = jnp.concatenate([zc(D_MODEL, pad_l), _rot_cols(kr), zc(D_MODEL, pad_r)], -1)
    win_ext = jnp.concatenate(
        [w_in[:, :OFF_KR], kr_arr, krr_arr, w_in[:, OFF_CONV:]], -1).astype(BF16)

    wq3 = w_uq.reshape(Q_RANK, N_HEADS, QK_HEAD)
    nope, rope = wq3[..., :QK_NOPE], wq3[..., QK_NOPE:]
    z3 = lambda c: jnp.zeros((Q_RANK, N_HEADS, c), F32)
    q_arr = jnp.concatenate([nope, rope, z3(pad_r)], -1).reshape(Q_RANK, N_HEADS * HEAD_PAD)
    q_rot = jnp.concatenate([z3(pad_l), _rot_cols(rope), z3(pad_r)], -1).reshape(Q_RANK, N_HEADS * HEAD_PAD)
    wuq_ext = jnp.concatenate([q_arr, q_rot], -1).astype(BF16)

    wkv3 = w_ukv.reshape(KV_RANK, N_HEADS, QK_NOPE + V_HEAD)
    zkv = jnp.zeros((KV_RANK, N_HEADS, HEAD_PAD - QK_NOPE), F32)
    k_arr = jnp.concatenate([wkv3[..., :QK_NOPE], zkv], -1).reshape(KV_RANK, N_HEADS * HEAD_PAD)
    v_arr = jnp.concatenate([wkv3[..., QK_NOPE:], zkv], -1).reshape(KV_RANK, N_HEADS * HEAD_PAD)
    wuk, wvt = k_arr.astype(BF16), v_arr.T.astype(BF16)

    cw = jnp.concatenate([conv_w, jnp.zeros((1, D_CONV), F32)], 0)
    gdim = D_POOL // len(POOL_WINDOWS)
    pw = jnp.zeros((D_POOL, D_POOL), F32)
    for g in range(len(POOL_WINDOWS)):
        pw = pw.at[g * gdim:(g + 1) * gdim, g * gdim:(g + 1) * gdim].set(pool_w[g])
    wr = jnp.concatenate([w_re, w_rg, zc(D_MODEL, LANE - N_EXPERTS - N_GROUPS)], -1).astype(BF16)
    br = jnp.concatenate([b_re, b_rg, jnp.zeros((LANE - N_EXPERTS - N_GROUPS,), F32)])[None, :]
    return win_ext, wuq_ext, wuk, wvt, cw, pw.astype(BF16), w_out.astype(BF16), wr, br


def _routing_tables(ctab, totals, n_tok_tiles):
    n_tiles_max = -(-(MAX_USED_CHUNKS * n_tok_tiles + (TILE_CHUNKS - 1) * N_EXPERTS) // TILE_CHUNKS)
    n_tiles_max = -(-n_tiles_max // EXPERT_TILES) * EXPERT_TILES
    n_chunks_max = n_tiles_max * TILE_CHUNKS
    chunks = totals[0, :N_EXPERTS].astype(I32)
    padded = ((chunks + TILE_CHUNKS - 1) // TILE_CHUNKS) * TILE_CHUNKS
    ends = jnp.cumsum(padded)
    starts = ends - padded
    e_of_c = ctab[:, 0, :LOCAL_CHUNKS].astype(I32)
    rel_c = ctab[:, 1, :LOCAL_CHUNKS].astype(I32)
    used = ctab[:, 2, :LOCAL_CHUNKS] > 0.5
    owner = e_of_c[..., None] == jnp.arange(N_EXPERTS, dtype=I32)
    dst = jnp.sum(jnp.where(owner, starts, 0), axis=-1) + rel_c
    local_id = (jnp.arange(n_tok_tiles, dtype=I32)[:, None] * LOCAL_CHUNKS
                + jnp.arange(LOCAL_CHUNKS, dtype=I32)[None, :])
    src = jnp.full((n_chunks_max,), LOCAL_CHUNKS - 1, I32)
    src = src.at[jnp.where(used, dst, n_chunks_max).reshape(-1)].set(local_id.reshape(-1), mode="drop")
    src_tab = src.reshape(n_tiles_max // EXPERT_TILES, 1, EXPERT_TILES * TILE_CHUNKS)
    dst_tab = jnp.where(used, dst, dst[:, 0:1])
    dst_tab = jnp.concatenate(
        [dst_tab, jnp.zeros((n_tok_tiles, LANE - LOCAL_CHUNKS), I32)], -1).reshape(n_tok_tiles, 1, LANE)
    n_valid = ends[-1] // TILE_CHUNKS
    tile_start = jnp.minimum(jnp.arange(n_tiles_max, dtype=I32), n_valid - 1) * TILE_CHUNKS
    tile_expert = jnp.sum((tile_start[:, None] >= ends[None, :]).astype(I32), axis=-1)
    return src_tab, dst_tab, tile_expert.astype(I32)


def kernel(x, c, ctx, c_ctx, w_ada, b_ada, w_in, g_q, w_uq, g_kv, w_ukv, conv_w, conv_b, conv_ln_g,
           conv_ln_b, pool_w, pool_scale, w_out, ln1_g, ln1_b, w_router_group, b_router_group,
           w_router_expert, b_router_expert, w_gate, w_up, w_down, ln2_g, ln2_b):
    B = x.shape[0]
    L = w_ada.shape[0]
    tabs = _rope_tables()
    tri = jnp.tril(jnp.ones((TILE, TILE), F32), -1).astype(BF16)
    upper = jnp.triu(jnp.ones((LANE, LANE), F32), 1).astype(BF16)

    cc = jnp.concatenate([c, c_ctx[None, :], jnp.zeros((16 - B - 1, D_MODEL), F32)], 0)
    ada = _ada(cc, w_ada, b_ada)
    mods_all = ada[:, :B + 1].reshape(L, B + 1, 6, D_MODEL)
    mods_all = jnp.concatenate([mods_all, jnp.zeros((L, B + 1, 2, D_MODEL), F32)], 2)

    xc, xl, lat_off = ctx, x, -1
    row = lambda a: a[None, :]
    for l in range(L):
        last = l == L - 1
        first_tile = 1 if last else 0
        mods = mods_all[l]
        win_ext, wuq_ext, wuk, wvt, cw, pw, wout, wr, br = _prep_layer(
            w_in[l], w_uq[l], w_ukv[l], conv_w[l], pool_w[l], w_out[l], w_router_group[l],
            b_router_group[l], w_router_expert[l], b_router_expert[l])

        q, k, v, u, poolin = _inproj(xc, xl, lat_off, mods, tabs, win_ext, row(g_q[l]), wuq_ext, row(g_kv[l]),
                                     wuk, wvt)
        attn = _attention(q, k, v)
        attn_ctx = None if last else _attention_ctx(q, k, v)
        x1, xloc, rinfo, ctab, totals = _mixout(
            attn, attn_ctx, u, poolin, xc, xl, lat_off, mods, cw, row(conv_b[l]), row(conv_ln_g[l]),
            row(conv_ln_b[l]), pw, row(pool_scale[l]), wout, row(ln1_g[l]), row(ln1_b[l]), wr, br,
            tri, upper, first_tile)

        n_tok_tiles = B * (N_TILES - first_tile)
        src_tab, dst_tab, tile_expert = _routing_tables(ctab, totals, n_tok_tiles)
        ys = _experts(tile_expert, src_tab, xloc, w_gate, w_up, w_down, l)
        xl = _combine(dst_tab, ys, rinfo, x1, mods, row(ln2_g[l]), row(ln2_b[l]), first_tile)
        xc, lat_off = xl, 0
    return xl
```

```python
import functools
import math

import jax
import jax.numpy as jnp
from jax import lax
from jax.experimental import pallas as pl
from jax.experimental.pallas import tpu as pltpu

D_MODEL = 1024
SEQ = 4096
CTX_LEN = 256
SEQ_ALL = CTX_LEN + SEQ
GRID_W = 64
N_HEADS = 8
QK_NOPE = 64
QK_ROPE = 32
QK_HEAD = QK_NOPE + QK_ROPE
V_HEAD = 64
Q_RANK = 384
KV_RANK = 256
D_CONV = 256
CONV_WIDTH = 31
D_POOL = 256
POOL_WINDOWS = (2, 4, 8, 16)
ROPE_THETA = 10000.0
OFF_KV = Q_RANK
OFF_KR = OFF_KV + KV_RANK
OFF_CONV = OFF_KR + QK_ROPE
OFF_POOL = OFF_CONV + 2 * D_CONV
N_GROUPS = 4
EXPERTS_PER_GROUP = 8
N_EXPERTS = 32
D_EXPERT = 256
LN_EPS = 1e-5
RMS_EPS = 1e-6
DEPTH = 2
DEEPNORM_ALPHA = (2 * DEPTH) ** 0.25

LANE = 128
SUBLANE = 8
TILE = 256
HALO = 16
N_TILES = SEQ_ALL // TILE
HEAD_PAD = 128
KEY_CHUNK = 256
VMEM_LIMIT = 56 * 1024 * 1024

CHUNK = 2 * SUBLANE
MAX_USED_CHUNKS = (2 * TILE + (CHUNK - 1) * N_EXPERTS) // CHUNK
LOCAL_ROWS = 1024
LOCAL_CHUNKS = LOCAL_ROWS // CHUNK
assert MAX_USED_CHUNKS < LOCAL_CHUNKS <= LANE
TILE_CHUNKS = TILE // CHUNK
EXPERT_TILES = 2

XOFF_Q = 0
XOFF_KV = 384
XOFF_KR = 640
XOFF_KRR = 768
XOFF_CONV = 896
XOFF_POOL = 1408
D_IN_EXT = 1664

BF16 = jnp.bfloat16
F32 = jnp.float32
I32 = jnp.int32


def _dot(a, b):
    return jnp.dot(a, b, preferred_element_type=F32)


def _silu(x):
    return x * jax.nn.sigmoid(x)


def _layer_norm_rows(x, g, b):
    mu = jnp.mean(x, axis=-1, keepdims=True)
    xc = x - mu
    var = jnp.mean(xc * xc, axis=-1, keepdims=True)
    return xc * lax.rsqrt(var + LN_EPS) * g + b


def _ada_kernel(cc_ref, w_ref, b_ref, o_ref):
    a = _silu(cc_ref[...])
    o_ref[0] = jnp.dot(a, w_ref[0], preferred_element_type=F32,
                       precision=lax.Precision.HIGHEST) + b_ref[0]


def _ada(cc, w_ada, b_ada):
    L = w_ada.shape[0]
    n = w_ada.shape[2]
    bn = 512
    return pl.pallas_call(
        _ada_kernel,
        out_shape=jax.ShapeDtypeStruct((L, cc.shape[0], n), F32),
        grid=(L, n // bn),
        in_specs=[
            pl.BlockSpec(cc.shape, lambda l, j: (0, 0)),
            pl.BlockSpec((1, D_MODEL, bn), lambda l, j: (l, 0, j)),
            pl.BlockSpec((1, 1, bn), lambda l, j: (l, 0, j)),
        ],
        out_specs=pl.BlockSpec((1, cc.shape[0], bn), lambda l, j: (l, 0, j)),
        compiler_params=pltpu.CompilerParams(
            dimension_semantics=("parallel", "parallel"), vmem_limit_bytes=VMEM_LIMIT),
        name="ada",
    )(cc, w_ada, b_ada.reshape(L, 1, n))


def _inproj_kernel(xc_ref, xl_ref, mod_ref, cq_ref, sq_ref, ck_ref, sk_ref, win_ref, gq_ref, wuq_ref,
                   gkv_ref, wuk_ref, wvt_ref, q_ref, k_ref, vt_ref, u_ref, pool_ref):
    x = jnp.where(pl.program_id(1) == 0, xc_ref[0], xl_ref[0])
    sh = mod_ref[0, 0:1, :]
    sc = mod_ref[0, 1:2, :]
    h = (x * (1.0 + sc) + sh).astype(BF16)
    p = _dot(h, win_ref[...])

    pq = p[:, XOFF_Q:XOFF_KV]
    qn = pq * lax.rsqrt(jnp.mean(pq * pq, axis=-1, keepdims=True) + RMS_EPS) * gq_ref[...]
    q2 = _dot(qn.astype(BF16), wuq_ref[...])
    cq = cq_ref[...]
    sq = sq_ref[...]
    nq = N_HEADS * HEAD_PAD
    for hd in range(N_HEADS):
        lo = hd * HEAD_PAD
        qh = q2[:, lo:lo + HEAD_PAD] * cq + q2[:, nq + lo:nq + lo + HEAD_PAD] * sq
        q_ref[0, :, lo:lo + HEAD_PAD] = qh.astype(BF16)

    pkv = p[:, XOFF_KV:XOFF_KR]
    kvn = pkv * lax.rsqrt(jnp.mean(pkv * pkv, axis=-1, keepdims=True) + RMS_EPS) * gkv_ref[...]
    kvn = kvn.astype(BF16)
    kn = _dot(kvn, wuk_ref[...])
    kr = p[:, XOFF_KR:XOFF_KRR] * ck_ref[...] + p[:, XOFF_KRR:XOFF_CONV] * sk_ref[...]
    for hd in range(N_HEADS):
        lo = hd * HEAD_PAD
        k_ref[0, :, lo:lo + HEAD_PAD] = (kn[:, lo:lo + HEAD_PAD] + kr).astype(BF16)
    vt = lax.dot_general(wvt_ref[...], kvn, (((1,), (1,)), ((), ())), preferred_element_type=F32)
    vrow = lax.broadcasted_iota(I32, (N_HEADS * HEAD_PAD, TILE), 0)
    vt = jnp.where((vrow & (HEAD_PAD - 1)) == V_HEAD, 1.0, vt).astype(BF16)
    for hd in range(N_HEADS):
        vt_ref[0, hd] = vt[hd * HEAD_PAD:(hd + 1) * HEAD_PAD, :]

    a = p[:, XOFF_CONV:XOFF_CONV + D_CONV]
    gt = p[:, XOFF_CONV + D_CONV:XOFF_POOL]
    u_ref[0] = (a * jax.nn.sigmoid(gt)).astype(BF16)
    pool_ref[0] = p[:, XOFF_POOL:]


def _inproj(xc, xl, lat_off, mods, tabs, win_ext, gq, wuq_ext, gkv, wuk, wvt):
    B = xl.shape[0]
    nb = mods.shape[0] - 1

    def mod_idx(b, i):
        return (jnp.where(i == 0, nb, b), 0, 0)

    tab_spec = pl.BlockSpec((TILE, LANE), lambda b, i: (i, 0))
    full = lambda a: pl.BlockSpec(a.shape, lambda b, i: (0,) * a.ndim)
    tok = lambda w: pl.BlockSpec((1, TILE, w), lambda b, i: (b, i, 0))
    hw = N_HEADS * HEAD_PAD
    return pl.pallas_call(
        _inproj_kernel,
        out_shape=(
            jax.ShapeDtypeStruct((B, SEQ_ALL, hw), BF16),
            jax.ShapeDtypeStruct((B, SEQ_ALL, hw), BF16),
            jax.ShapeDtypeStruct((B, N_HEADS, HEAD_PAD, SEQ_ALL), BF16),
            jax.ShapeDtypeStruct((B, SEQ_ALL, D_CONV), BF16),
            jax.ShapeDtypeStruct((B, SEQ_ALL, D_POOL), F32),
        ),
        grid=(B, N_TILES),
        in_specs=[pl.BlockSpec((1, TILE, D_MODEL), lambda b, i: (b, 0, 0)),
                  pl.BlockSpec((1, TILE, D_MODEL), lambda b, i: (b, jnp.maximum(i + lat_off, 0), 0)),
                  pl.BlockSpec((1, 8, D_MODEL), mod_idx),
                  tab_spec, tab_spec, tab_spec, tab_spec,
                  full(win_ext), full(gq), full(wuq_ext), full(gkv), full(wuk), full(wvt)],
        out_specs=(tok(hw), tok(hw),
                   pl.BlockSpec((1, N_HEADS, HEAD_PAD, TILE), lambda b, i: (b, 0, 0, i)),
                   tok(D_CONV), tok(D_POOL)),
        compiler_params=pltpu.CompilerParams(
            dimension_semantics=("parallel", "parallel"), vmem_limit_bytes=VMEM_LIMIT),
        name="inproj",
    )(xc, xl, mods, *tabs, win_ext, gq, wuq_ext, gkv, wuk, wvt)


def _scores_t(q, k_ref, n_keys):
    lane = lax.broadcasted_iota(I32, q.shape, 1)
    qbd = jnp.concatenate([jnp.where(lane < HEAD_PAD, q, jnp.zeros_like(q)),
                           jnp.where(lane >= HEAD_PAD, q, jnp.zeros_like(q))], axis=0)
    return lax.dot_general(k_ref[0, 0:n_keys, :], qbd, (((1,), (1,)), ((), ())),
                           preferred_element_type=F32)


def _store_scores(s_ref, m_ref, st):
    s_ref[...] = st
    m_ref[...] = jnp.broadcast_to(jnp.max(st, axis=0, keepdims=True), m_ref.shape)


def _softmax_values_t(s_ref, m_ref, vt_ref, p_ref, n_keys):
    nq = s_ref.shape[1] // 2
    m = m_ref[0:1, :]
    for kc in range(0, n_keys, KEY_CHUNK):
        p_ref[kc:kc + KEY_CHUNK, :] = jnp.exp2(s_ref[kc:kc + KEY_CHUNK, :] - m).astype(BF16)
    outs = []
    for hd in range(2):
        ot = _dot(vt_ref[0, hd, :, 0:n_keys], p_ref[0:n_keys, hd * nq:(hd + 1) * nq])
        outs.append(ot[0:V_HEAD, :] / ot[V_HEAD:V_HEAD + 1, :])
    return jnp.concatenate(outs, axis=0).T.astype(BF16)


def _attn_kernel(qca_ref, qcb_ref, qna_ref, qnb_ref, k_ref, vt_ref, o_ref,
                 sa_ref, sb_ref, ma_ref, mb_ref, p_ref):
    i = pl.program_id(2)

    @pl.when(i == 0)
    def _():
        qc = jnp.concatenate([qca_ref[0], qcb_ref[0]], axis=0)
        _store_scores(sa_ref, ma_ref, _scores_t(qc, k_ref, SEQ_ALL))

    def step(cur_ref, cur_m_ref, nxt_ref, nxt_m_ref):
        qn = jnp.concatenate([qna_ref[0], qnb_ref[0]], axis=0)
        _store_scores(nxt_ref, nxt_m_ref, _scores_t(qn, k_ref, SEQ_ALL))
        o_ref[0] = _softmax_values_t(cur_ref, cur_m_ref, vt_ref, p_ref, SEQ_ALL)

    parity = lax.rem(i, 2)

    @pl.when(parity == 0)
    def _():
        step(sa_ref, ma_ref, sb_ref, mb_ref)

    @pl.when(parity == 1)
    def _():
        step(sb_ref, mb_ref, sa_ref, ma_ref)


def _attention(q, k, v):
    B = q.shape[0]
    n_pairs = N_HEADS // 2
    pw = 2 * HEAD_PAD
    n_steps = SEQ // (2 * TILE)
    assert n_steps % 2 == 0
    last = SEQ_ALL // TILE - 1
    qspec = lambda f: pl.BlockSpec((1, TILE, pw), f)
    cols = 4 * TILE
    return pl.pallas_call(
        _attn_kernel,
        out_shape=jax.ShapeDtypeStruct((B, SEQ, N_HEADS * V_HEAD), BF16),
        grid=(B, n_pairs, n_steps),
        in_specs=[
            qspec(lambda b, hp, i: (b, 2 * i + 1, hp)),
            qspec(lambda b, hp, i: (b, 2 * i + 2, hp)),
            qspec(lambda b, hp, i: (b, jnp.minimum(2 * i + 3, last), hp)),
            qspec(lambda b, hp, i: (b, jnp.minimum(2 * i + 4, last), hp)),
            pl.BlockSpec((1, SEQ_ALL, pw), lambda b, hp, i: (b, 0, hp)),
            pl.BlockSpec((1, 2, HEAD_PAD, SEQ_ALL), lambda b, hp, i: (b, hp, 0, 0)),
        ],
        out_specs=pl.BlockSpec((1, 2 * TILE, 2 * V_HEAD), lambda b, hp, i: (b, i, hp)),
        scratch_shapes=[pltpu.VMEM((SEQ_ALL, cols), F32), pltpu.VMEM((SEQ_ALL, cols), F32),
                        pltpu.VMEM((SUBLANE, cols), F32), pltpu.VMEM((SUBLANE, cols), F32),
                        pltpu.VMEM((SEQ_ALL, cols), BF16)],
        compiler_params=pltpu.CompilerParams(
            dimension_semantics=("arbitrary", "arbitrary", "arbitrary"),
            vmem_limit_bytes=VMEM_LIMIT),
        name="attn",
    )(q, q, q, q, k, v)


def _attn_ctx_kernel(q_ref, k_ref, vt_ref, o_ref, s_ref, m_ref, p_ref):
    _store_scores(s_ref, m_ref, _scores_t(q_ref[0], k_ref, CTX_LEN))
    o_ref[0] = _softmax_values_t(s_ref, m_ref, vt_ref, p_ref, CTX_LEN)


def _attention_ctx(q, k, v):
    B = q.shape[0]
    pw = 2 * HEAD_PAD
    blk = pl.BlockSpec((1, CTX_LEN, pw), lambda b, hp: (b, 0, hp))
    return pl.pallas_call(
        _attn_ctx_kernel,
        out_shape=jax.ShapeDtypeStruct((B, CTX_LEN, N_HEADS * V_HEAD), BF16),
        grid=(B, N_HEADS // 2),
        in_specs=[blk, blk, pl.BlockSpec((1, 2, HEAD_PAD, CTX_LEN), lambda b, hp: (b, hp, 0, 0))],
        out_specs=pl.BlockSpec((1, CTX_LEN, 2 * V_HEAD), lambda b, hp: (b, 0, hp)),
        scratch_shapes=[pltpu.VMEM((CTX_LEN, 2 * TILE), F32), pltpu.VMEM((SUBLANE, 2 * TILE), F32),
                        pltpu.VMEM((CTX_LEN, 2 * TILE), BF16)],
        compiler_params=pltpu.CompilerParams(
            dimension_semantics=("parallel", "parallel"), vmem_limit_bytes=VMEM_LIMIT),
        name="attn_ctx",
    )(q, k, v)


def _mixout_kernel(*refs, first_tile):
    if first_tile == 0:
        attn_ref, actx_ref = refs[0], refs[1]
        refs = refs[2:]
    else:
        attn_ref, actx_ref = refs[0], None
        refs = refs[1:]
    (u_ref, ul_ref, ur_ref, pm_ref, pl_ref, pr_ref, xc_ref, xl_ref, mod_ref,
     cw_ref, cb_ref, cg_ref, cbeta_ref, pw_ref, ps_ref, wout_ref, g1_ref, b1_ref,
     wr_ref, br_ref, tri_ref, upper_ref,
     x1_ref, xloc_ref, rinfo_ref, ctab_ref, tot_ref,
     winu, winp, shu, shp, cnt) = refs
    i = pl.program_id(1) + first_tile
    left_ok = i >= 2
    right_ok = (i >= 1) & (i <= N_TILES - 2)

    @pl.when((pl.program_id(0) == 0) & (pl.program_id(1) == 0))
    def _():
        cnt[...] = jnp.zeros_like(cnt)

    zu = jnp.zeros((HALO, D_CONV), F32)
    winu[0:HALO, :] = jnp.where(left_ok, ul_ref[0].astype(F32), zu)
    winu[HALO:HALO + TILE, :] = u_ref[0].astype(F32)
    winu[HALO + TILE:, :] = jnp.where(right_ok, ur_ref[0].astype(F32), zu)
    winp[0:HALO, :] = jnp.where(left_ok, pl_ref[0], zu)
    winp[HALO:HALO + TILE, :] = pm_ref[0]
    winp[HALO + TILE:, :] = jnp.where(right_ok, pr_ref[0], zu)
    ph_rows = TILE + 2 * HALO - SUBLANE
    for ph in range(SUBLANE):
        shu[ph] = winu[ph:ph + ph_rows, :]
        shp[ph] = winp[ph:ph + ph_rows, :]

    def shifted(buf, off):
        ph, base = off % SUBLANE, off - off % SUBLANE
        return buf[ph, base:base + TILE, :]

    acc = jnp.zeros((TILE, D_CONV), F32)
    for j in range(CONV_WIDTH):
        acc = acc + shifted(shu, HALO - CONV_WIDTH // 2 + j) * cw_ref[j:j + 1, :]
    conv = _silu(_layer_norm_rows(acc + cb_ref[...], cg_ref[...], cbeta_ref[...]))

    def pw(o):
        return shifted(shp, HALO + o)

    centre = pw(0)
    s2 = pw(-1) + centre
    s4 = s2 + pw(-2) + pw(1)
    s8 = s4 + pw(-4) + pw(-3) + pw(2) + pw(3)
    s16 = s8 + pw(-8) + pw(-7) + pw(-6) + pw(-5) + pw(4) + pw(5) + pw(6) + pw(7)
    row = lax.broadcasted_iota(I32, (TILE, 1), 0)
    seg_len = jnp.where(i == 0, CTX_LEN, SEQ)
    t = jnp.where(i == 0, 0, (i - 1) * TILE) + row

    def mean(sw, w):
        hi = jnp.minimum(t + (w // 2 - 1), seg_len - 1)
        lo = jnp.maximum(t - w // 2, 0)
        return sw / (hi - lo + 1).astype(F32)

    lane_p = lax.broadcasted_iota(I32, (1, D_POOL), 1)
    gdim = D_POOL // len(POOL_WINDOWS)
    mixed = jnp.where(lane_p < gdim, mean(s2, 2),
                      jnp.where(lane_p < 2 * gdim, mean(s4, 4),
                                jnp.where(lane_p < 3 * gdim, mean(s8, 8), mean(s16, 16)))) - centre
    pool = _dot(mixed.astype(BF16), pw_ref[...]) * ps_ref[...]

    d_attn = N_HEADS * V_HEAD
    attn = attn_ref[0]
    if actx_ref is not None:
        attn = jnp.where(i == 0, actx_ref[0], attn)
    mix = (_dot(attn, wout_ref[0:d_attn, :])
           + _dot(conv.astype(BF16), wout_ref[d_attn:d_attn + D_CONV, :])
           + _dot(pool.astype(BF16), wout_ref[d_attn + D_CONV:, :]))
    gate1 = mod_ref[0, 2:3, :]
    x_in = jnp.where(i == 0, xc_ref[0], xl_ref[0])
    x1 = _layer_norm_rows(DEEPNORM_ALPHA * x_in + gate1 * mix, g1_ref[...], b1_ref[...])
    x1_ref[0] = x1

    h2 = (x1 * (1.0 + mod_ref[0, 4:5, :]) + mod_ref[0, 3:4, :]).astype(BF16)
    logits = _dot(h2, wr_ref[...]) + br_ref[...]
    lane = lax.broadcasted_iota(I32, (TILE, LANE), 1)
    neg = jnp.float32(-jnp.inf)
    big = jnp.int32(1 << 20)
    is_g = (lane >= N_EXPERTS) & (lane < N_EXPERTS + N_GROUPS)
    gl = jnp.where(is_g, logits, neg)
    gmax = jnp.max(gl, axis=-1, keepdims=True)
    g_sel = jnp.min(jnp.where(gl == gmax, lane - N_EXPERTS, big), axis=-1, keepdims=True)
    p_sel = 1.0 / jnp.sum(jnp.exp(gl - gmax), axis=-1, keepdims=True)
    in_grp = (lane >= g_sel * EXPERTS_PER_GROUP) & (lane < (g_sel + 1) * EXPERTS_PER_GROUP)
    el = jnp.where(in_grp, logits, neg)
    v1 = jnp.max(el, axis=-1, keepdims=True)
    i1 = jnp.min(jnp.where(el == v1, lane, big), axis=-1, keepdims=True)
    el2 = jnp.where(lane == i1, neg, el)
    v2 = jnp.max(el2, axis=-1, keepdims=True)
    i2 = jnp.min(jnp.where(el2 == v2, lane, big), axis=-1, keepdims=True)
    e2 = jnp.exp(v2 - v1)
    w1 = p_sel / (1.0 + e2)
    w2 = p_sel * e2 / (1.0 + e2)

    oh1 = (lane == i1).astype(F32)
    oh2 = (lane == i2).astype(F32)
    both = oh1 + oh2
    n_e = jnp.sum(both, axis=0, keepdims=True)
    chunks_e = jnp.floor((n_e + (CHUNK - 1)) * (1.0 / CHUNK))
    chunks8 = jnp.broadcast_to(chunks_e, (SUBLANE, LANE))
    start_e = _dot(chunks8.astype(BF16), upper_ref[...])[0:1, :]
    slot_mat = _dot(tri_ref[...], both.astype(BF16)) + CHUNK * start_e
    s1 = jnp.sum(oh1 * slot_mat, axis=-1, keepdims=True)
    s2_ = jnp.sum(oh2 * slot_mat, axis=-1, keepdims=True)
    fields = (w1, w2, s1, s2_)
    rinfo = jnp.zeros((TILE, LANE), F32)
    for n, f in enumerate(fields):
        rinfo = jnp.where(lane == n, f, rinfo)
    rinfo_ref[...] = rinfo

    slot_lane = lax.broadcasted_iota(I32, (TILE, LOCAL_ROWS), 1).astype(F32)
    perm = ((slot_lane == s1).astype(F32) + (slot_lane == s2_).astype(F32)).astype(BF16)
    sorted_rows = lax.dot_general(perm, h2, (((0,), (0,)), ((), ())), preferred_element_type=F32)
    xloc_ref[...] = sorted_rows.astype(BF16)

    done_e = cnt[0:1, :]
    cidx = lax.broadcasted_iota(I32, (LANE, LANE), 0).astype(F32)
    lane_sq = lax.broadcasted_iota(I32, (LANE, LANE), 1)
    owns = ((cidx >= start_e) & (cidx < start_e + chunks_e)).astype(F32)
    e_of_c = jnp.sum(owns * lane_sq.astype(F32), axis=-1, keepdims=True)
    rel_c = jnp.sum(owns * (done_e + cidx - start_e), axis=-1, keepdims=True)
    used_c = jnp.sum(owns, axis=-1, keepdims=True)
    tab = jnp.where(lane_sq == 0, e_of_c, jnp.where(lane_sq == 1, rel_c,
                                                    jnp.where(lane_sq == 2, used_c, 0.0)))
    ctab_ref[0] = tab.T[0:SUBLANE, :]
    cnt[...] = cnt[...] + chunks_e
    tot_ref[...] = cnt[...]


def _mixout(attn, attn_ctx, u, poolin, xc, xl, lat_off, mods, cw, cb, cg, cbeta, pw, ps, wout, g1, b1, wr, br,
            tri, upper, first_tile):
    B = xl.shape[0]
    nt = N_TILES - first_tile
    nb = mods.shape[0] - 1
    hb = TILE // HALO
    n_hb = SEQ_ALL // HALO

    def mod_idx(b, i):
        return (jnp.where(i + first_tile == 0, nb, b), 0, 0)

    tok = lambda w: pl.BlockSpec((1, TILE, w), lambda b, i: (b, i + first_tile, 0))
    own = lambda w: pl.BlockSpec((1, TILE, w), lambda b, i: (b, i, 0))
    left = lambda w: pl.BlockSpec(
        (1, HALO, w), lambda b, i: (b, jnp.maximum((i + first_tile) * hb - 1, 0), 0))
    right = lambda w: pl.BlockSpec(
        (1, HALO, w), lambda b, i: (b, jnp.minimum((i + first_tile + 1) * hb, n_hb - 1), 0))
    full = lambda a: pl.BlockSpec(a.shape, lambda b, i: (0,) * a.ndim)
    d_attn = N_HEADS * V_HEAD
    attn_specs = [pl.BlockSpec((1, TILE, d_attn),
                               lambda b, i: (b, jnp.maximum(i + first_tile - 1, 0), 0))]
    attn_args = [attn]
    if first_tile == 0:
        attn_specs.append(pl.BlockSpec((1, TILE, d_attn), lambda b, i: (b, 0, 0)))
        attn_args.append(attn_ctx)
    n_tok_tiles = B * nt
    ph_rows = TILE + 2 * HALO - SUBLANE
    return pl.pallas_call(
        functools.partial(_mixout_kernel, first_tile=first_tile),
        out_shape=(
            jax.ShapeDtypeStruct((B, nt * TILE, D_MODEL), F32),
            jax.ShapeDtypeStruct((n_tok_tiles * LOCAL_ROWS, D_MODEL), BF16),
            jax.ShapeDtypeStruct((n_tok_tiles * TILE, LANE), F32),
            jax.ShapeDtypeStruct((n_tok_tiles, SUBLANE, LANE), F32),
            jax.ShapeDtypeStruct((SUBLANE, LANE), F32),
        ),
        grid=(B, nt),
        in_specs=attn_specs + [
            tok(D_CONV), left(D_CONV), right(D_CONV),
            tok(D_POOL), left(D_POOL), right(D_POOL),
            pl.BlockSpec((1, TILE, D_MODEL), lambda b, i: (b, 0, 0)),
            pl.BlockSpec((1, TILE, D_MODEL),
                         lambda b, i: (b, jnp.maximum(i + first_tile + lat_off, 0), 0)),
            pl.BlockSpec((1, 8, D_MODEL), mod_idx),
            full(cw), full(cb), full(cg), full(cbeta), full(pw), full(ps), full(wout),
            full(g1), full(b1), full(wr), full(br), full(tri), full(upper)],
        out_specs=(own(D_MODEL),
                   pl.BlockSpec((LOCAL_ROWS, D_MODEL), lambda b, i: (b * nt + i, 0)),
                   pl.BlockSpec((TILE, LANE), lambda b, i: (b * nt + i, 0)),
                   pl.BlockSpec((1, SUBLANE, LANE), lambda b, i: (b * nt + i, 0, 0)),
                   pl.BlockSpec((SUBLANE, LANE), lambda b, i: (0, 0))),
        scratch_shapes=[pltpu.VMEM((TILE + 2 * HALO, D_CONV), F32),
                        pltpu.VMEM((TILE + 2 * HALO, D_POOL), F32),
                        pltpu.VMEM((SUBLANE, ph_rows, D_CONV), F32),
                        pltpu.VMEM((SUBLANE, ph_rows, D_POOL), F32),
                        pltpu.VMEM((SUBLANE, LANE), F32)],
        compiler_params=pltpu.CompilerParams(
            dimension_semantics=("arbitrary", "arbitrary"), vmem_limit_bytes=VMEM_LIMIT),
        name="mixout",
    )(*attn_args, u, u, u, poolin, poolin, poolin, xc, xl, mods, cw, cb, cg, cbeta, pw, ps, wout,
      g1, b1, wr, br, tri, upper)


def _chunk_copy(src, src_chunk, buf, dst_chunk, sem):
    src_row = src_chunk * CHUNK
    if not isinstance(src_row, int):
        src_row = pl.multiple_of(src_row, CHUNK)
    return pltpu.make_async_copy(
        src.at[pl.ds(src_row, CHUNK)], buf.at[pl.ds(dst_chunk * CHUNK, CHUNK)], sem)


def _start_gather(src, table_ref, n_chunks, buf, sem):
    for c in range(n_chunks):
        _chunk_copy(src, table_ref[0, 0, c], buf, c, sem).start()


def _wait_gather(src, n_chunks, buf, sem):
    for c in range(n_chunks):
        _chunk_copy(src, 0, buf, c, sem).wait()


def _double_buffered_gather(step, n_steps, src, cur_tab, nxt_tab, n_chunks, bufs, sems, compute):
    @pl.when(step == 0)
    def _():
        _start_gather(src, cur_tab, n_chunks, bufs[0], sems.at[0])

    for par in range(2):
        @pl.when(lax.rem(step, 2) == par)
        def _(par=par):
            _start_gather(src, nxt_tab, n_chunks, bufs[1 - par], sems.at[1 - par])
            _wait_gather(src, n_chunks, bufs[par], sems.at[par])
            compute(bufs[par])

            @pl.when(step == n_steps - 1)
            def _():
                _wait_gather(src, n_chunks, bufs[1 - par], sems.at[1 - par])


def _experts_kernel(te_ref, cur_ref, nxt_ref, xloc_ref, wga_ref, wua_ref, wda_ref,
                    wgb_ref, wub_ref, wdb_ref, ys_ref,
                    xbuf_a, xbuf_b, cga, cua, cda, cgb, cub, cdb, sem, *, n_steps):
    i = pl.program_id(0)
    w_refs = ((wga_ref, wua_ref, wda_ref), (wgb_ref, wub_ref, wdb_ref))
    caches = ((cga, cua, cda), (cgb, cub, cdb))

    for t in range(EXPERT_TILES):
        tile = EXPERT_TILES * i + t
        prev = jnp.maximum(tile - EXPERT_TILES, 0)

        @pl.when((i == 0) | (te_ref[tile] != te_ref[prev]))
        def _(t=t):
            for cache, w_ref in zip(caches[t], w_refs[t]):
                cache[...] = w_ref[0, 0].astype(BF16)

    def compute(xbuf):
        for t in range(EXPERT_TILES):
            wg, wu, wd = caches[t]
            xt = xbuf[t * TILE:(t + 1) * TILE, :]
            hid = (_silu(_dot(xt, wg[...])) * _dot(xt, wu[...])).astype(BF16)
            ys_ref[t * TILE:(t + 1) * TILE, :] = _dot(hid, wd[...]).astype(BF16)

    _double_buffered_gather(i, n_steps, xloc_ref, cur_ref, nxt_ref, EXPERT_TILES * TILE_CHUNKS,
                            (xbuf_a, xbuf_b), sem, compute)


def _experts(tile_expert, src_tab, xloc, w_gate, w_up, w_down, layer):
    n_steps = src_tab.shape[0]
    rows = EXPERT_TILES * TILE

    def w_idx(t):
        return lambda i, te: (layer, te[EXPERT_TILES * i + t], 0, 0)

    tab = lambda f: pl.BlockSpec((1, 1, EXPERT_TILES * TILE_CHUNKS), f, memory_space=pltpu.SMEM)
    w_specs, w_cache = [], []
    for t in range(EXPERT_TILES):
        w_specs += [pl.BlockSpec((1, 1, D_MODEL, D_EXPERT), w_idx(t)),
                    pl.BlockSpec((1, 1, D_MODEL, D_EXPERT), w_idx(t)),
                    pl.BlockSpec((1, 1, D_EXPERT, D_MODEL), w_idx(t))]
        w_cache += [pltpu.VMEM((D_MODEL, D_EXPERT), BF16), pltpu.VMEM((D_MODEL, D_EXPERT), BF16),
                    pltpu.VMEM((D_EXPERT, D_MODEL), BF16)]
    return pl.pallas_call(
        functools.partial(_experts_kernel, n_steps=n_steps),
        out_shape=jax.ShapeDtypeStruct((n_steps * rows, D_MODEL), BF16),
        grid_spec=pltpu.PrefetchScalarGridSpec(
            num_scalar_prefetch=1,
            grid=(n_steps,),
            in_specs=[tab(lambda i, te: (i, 0, 0)),
                      tab(lambda i, te: (jnp.minimum(i + 1, n_steps - 1), 0, 0)),
                      pl.BlockSpec(memory_space=pl.ANY)] + w_specs,
            out_specs=pl.BlockSpec((rows, D_MODEL), lambda i, te: (i, 0)),
            scratch_shapes=[pltpu.VMEM((rows, D_MODEL), BF16), pltpu.VMEM((rows, D_MODEL), BF16)]
            + w_cache + [pltpu.SemaphoreType.DMA((2,))],
        ),
        compiler_params=pltpu.CompilerParams(
            dimension_semantics=("arbitrary",), vmem_limit_bytes=VMEM_LIMIT),
        name="experts",
    )(tile_expert, src_tab, src_tab, xloc, *([w_gate, w_up, w_down] * EXPERT_TILES))


def _combine_kernel(cur_ref, nxt_ref, ys_ref, rinfo_ref, x_ref, mod_ref, g2_ref, b2_ref, o_ref,
                    ybuf_a, ybuf_b, sem, *, n_tok_tiles):
    g = pl.program_id(0) * pl.num_programs(1) + pl.program_id(1)

    def compute(ybuf):
        slot_lane = lax.broadcasted_iota(I32, (TILE, LOCAL_ROWS), 1).astype(F32)
        sel = (jnp.where(slot_lane == rinfo_ref[:, 2:3], rinfo_ref[:, 0:1], 0.0)
               + jnp.where(slot_lane == rinfo_ref[:, 3:4], rinfo_ref[:, 1:2], 0.0))
        y = _dot(sel.astype(BF16), ybuf[...])
        gate2 = mod_ref[0, 5:6, :]
        o_ref[0] = _layer_norm_rows(DEEPNORM_ALPHA * x_ref[0] + gate2 * y, g2_ref[...], b2_ref[...])

    _double_buffered_gather(g, n_tok_tiles, ys_ref, cur_ref, nxt_ref, LOCAL_CHUNKS,
                            (ybuf_a, ybuf_b), sem, compute)


def _combine(dst_tab, ys, rinfo, x1, mods, g2, b2, first_tile):
    B = x1.shape[0]
    nt = N_TILES - first_tile
    nb = mods.shape[0] - 1
    n_tok_tiles = B * nt

    def mod_idx(b, i):
        return (jnp.where(i + first_tile == 0, nb, b), 0, 0)

    full = lambda a: pl.BlockSpec(a.shape, lambda b, i: (0,) * a.ndim)
    tab = lambda f: pl.BlockSpec((1, 1, LANE), f, memory_space=pltpu.SMEM)
    return pl.pallas_call(
        functools.partial(_combine_kernel, n_tok_tiles=n_tok_tiles),
        out_shape=jax.ShapeDtypeStruct((B, nt * TILE, D_MODEL), F32),
        grid=(B, nt),
        in_specs=[tab(lambda b, i: (b * nt + i, 0, 0)),
                  tab(lambda b, i: (jnp.minimum(b * nt + i + 1, n_tok_tiles - 1), 0, 0)),
                  pl.BlockSpec(memory_space=pl.ANY),
                  pl.BlockSpec((TILE, LANE), lambda b, i: (b * nt + i, 0)),
                  pl.BlockSpec((1, TILE, D_MODEL), lambda b, i: (b, i, 0)),
                  pl.BlockSpec((1, 8, D_MODEL), mod_idx),
                  full(g2), full(b2)],
        out_specs=pl.BlockSpec((1, TILE, D_MODEL), lambda b, i: (b, i, 0)),
        scratch_shapes=[pltpu.VMEM((LOCAL_ROWS, D_MODEL), BF16), pltpu.VMEM((LOCAL_ROWS, D_MODEL), BF16),
                        pltpu.SemaphoreType.DMA((2,))],
        compiler_params=pltpu.CompilerParams(
            dimension_semantics=("arbitrary", "arbitrary"), vmem_limit_bytes=VMEM_LIMIT),
        name="combine",
    )(dst_tab, dst_tab, ys, rinfo, x1, mods, g2, b2)


def _rot_cols(w):
    w4 = w.reshape(w.shape[:-1] + (2, 2, QK_ROPE // 4))
    return jnp.stack([-w4[..., 1, :], w4[..., 0, :]], axis=-2).reshape(w.shape)


def _rope_tables():
    rows = SEQ // GRID_W
    row = jnp.repeat(jnp.arange(rows), GRID_W)
    col = jnp.tile(jnp.arange(GRID_W), rows)
    d_axis = QK_ROPE // 2
    inv_freq = jnp.power(ROPE_THETA, -jnp.arange(0, d_axis, 2, dtype=F32) / d_axis)

    def axis_angles(p):
        a = p.astype(F32)[:, None] * inv_freq[None, :]
        return jnp.concatenate([a, a], -1)

    ang = jnp.concatenate([axis_angles(row), axis_angles(col)], -1)
    cos = jnp.concatenate([jnp.ones((CTX_LEN, QK_ROPE), F32), jnp.cos(ang)], 0)
    sin = jnp.concatenate([jnp.zeros((CTX_LEN, QK_ROPE), F32), jnp.sin(ang)], 0)
    scale = math.log2(math.e) / math.sqrt(QK_HEAD)
    z64 = jnp.zeros((SEQ_ALL, QK_NOPE), F32)
    z32 = jnp.zeros((SEQ_ALL, HEAD_PAD - QK_HEAD), F32)
    cq = jnp.concatenate([jnp.full((SEQ_ALL, QK_NOPE), scale, F32), scale * cos, z32], -1)
    sq = jnp.concatenate([z64, scale * sin, z32], -1)
    ck = jnp.concatenate([z64, cos, z32], -1)
    sk = jnp.concatenate([z64, sin, z32], -1)
    return cq, sq, ck, sk


def _prep_layer(w_in, w_uq, w_ukv, conv_w, pool_w, w_out, w_rg, b_rg, w_re, b_re):
    zc = lambda r, c: jnp.zeros((r, c), F32)
    kr = w_in[:, OFF_KR:OFF_CONV]
    pad_l, pad_r = QK_NOPE, HEAD_PAD - QK_HEAD
    kr_arr = jnp.concatenate([zc(D_MODEL, pad_l), kr, zc(D_MODEL, pad_r)], -1)
    krr_arr = jnp.concatenate([zc(D_MODEL, pad_l), _rot_cols(kr), zc(D_MODEL, pad_r)], -1)
    win_ext = jnp.concatenate(
        [w_in[:, :OFF_KR], kr_arr, krr_arr, w_in[:, OFF_CONV:]], -1).astype(BF16)

    wq3 = w_uq.reshape(Q_RANK, N_HEADS, QK_HEAD)
    nope, rope = wq3[..., :QK_NOPE], wq3[..., QK_NOPE:]
    z3 = lambda c: jnp.zeros((Q_RANK, N_HEADS, c), F32)
    q_arr = jnp.concatenate([nope, rope, z3(pad_r)], -1).reshape(Q_RANK, N_HEADS * HEAD_PAD)
    q_rot = jnp.concatenate([z3(pad_l), _rot_cols(rope), z3(pad_r)], -1).reshape(Q_RANK, N_HEADS * HEAD_PAD)
    wuq_ext = jnp.concatenate([q_arr, q_rot], -1).astype(BF16)

    wkv3 = w_ukv.reshape(KV_RANK, N_HEADS, QK_NOPE + V_HEAD)
    zkv = jnp.zeros((KV_RANK, N_HEADS, HEAD_PAD - QK_NOPE), F32)
    k_arr = jnp.concatenate([wkv3[..., :QK_NOPE], zkv], -1).reshape(KV_RANK, N_HEADS * HEAD_PAD)
    v_arr = jnp.concatenate([wkv3[..., QK_NOPE:], zkv], -1).reshape(KV_RANK, N_HEADS * HEAD_PAD)
    wuk, wvt = k_arr.astype(BF16), v_arr.T.astype(BF16)

    cw = jnp.concatenate([conv_w, jnp.zeros((1, D_CONV), F32)], 0)
    gdim = D_POOL // len(POOL_WINDOWS)
    pw = jnp.zeros((D_POOL, D_POOL), F32)
    for g in range(len(POOL_WINDOWS)):
        pw = pw.at[g * gdim:(g + 1) * gdim, g * gdim:(g + 1) * gdim].set(pool_w[g])
    wr = jnp.concatenate([w_re, w_rg, zc(D_MODEL, LANE - N_EXPERTS - N_GROUPS)], -1).astype(BF16)
    br = jnp.concatenate([b_re, b_rg, jnp.zeros((LANE - N_EXPERTS - N_GROUPS,), F32)])[None, :]
    return win_ext, wuq_ext, wuk, wvt, cw, pw.astype(BF16), w_out.astype(BF16), wr, br


def _routing_tables(ctab, totals, n_tok_tiles):
    n_tiles_max = -(-(MAX_USED_CHUNKS * n_tok_tiles + (TILE_CHUNKS - 1) * N_EXPERTS) // TILE_CHUNKS)
    n_tiles_max = -(-n_tiles_max // EXPERT_TILES) * EXPERT_TILES
    n_chunks_max = n_tiles_max * TILE_CHUNKS
    chunks = totals[0, :N_EXPERTS].astype(I32)
    padded = ((chunks + TILE_CHUNKS - 1) // TILE_CHUNKS) * TILE_CHUNKS
    ends = jnp.cumsum(padded)
    starts = ends - padded
    e_of_c = ctab[:, 0, :LOCAL_CHUNKS].astype(I32)
    rel_c = ctab[:, 1, :LOCAL_CHUNKS].astype(I32)
    used = ctab[:, 2, :LOCAL_CHUNKS] > 0.5
    owner = e_of_c[..., None] == jnp.arange(N_EXPERTS, dtype=I32)
    dst = jnp.sum(jnp.where(owner, starts, 0), axis=-1) + rel_c
    local_id = (jnp.arange(n_tok_tiles, dtype=I32)[:, None] * LOCAL_CHUNKS
                + jnp.arange(LOCAL_CHUNKS, dtype=I32)[None, :])
    src = jnp.full((n_chunks_max,), LOCAL_CHUNKS - 1, I32)
    src = src.at[jnp.where(used, dst, n_chunks_max).reshape(-1)].set(local_id.reshape(-1), mode="drop")
    src_tab = src.reshape(n_tiles_max // EXPERT_TILES, 1, EXPERT_TILES * TILE_CHUNKS)
    dst_tab = jnp.where(used, dst, dst[:, 0:1])
    dst_tab = jnp.concatenate(
        [dst_tab, jnp.zeros((n_tok_tiles, LANE - LOCAL_CHUNKS), I32)], -1).reshape(n_tok_tiles, 1, LANE)
    n_valid = ends[-1] // TILE_CHUNKS
    tile_start = jnp.minimum(jnp.arange(n_tiles_max, dtype=I32), n_valid - 1) * TILE_CHUNKS
    tile_expert = jnp.sum((tile_start[:, None] >= ends[None, :]).astype(I32), axis=-1)
    return src_tab, dst_tab, tile_expert.astype(I32)


def kernel(x, c, ctx, c_ctx, w_ada, b_ada, w_in, g_q, w_uq, g_kv, w_ukv, conv_w, conv_b, conv_ln_g,
           conv_ln_b, pool_w, pool_scale, w_out, ln1_g, ln1_b, w_router_group, b_router_group,
           w_router_expert, b_router_expert, w_gate, w_up, w_down, ln2_g, ln2_b):
    B = x.shape[0]
    L = w_ada.shape[0]
    tabs = _rope_tables()
    tri = jnp.tril(jnp.ones((TILE, TILE), F32), -1).astype(BF16)
    upper = jnp.triu(jnp.ones((LANE, LANE), F32), 1).astype(BF16)

    cc = jnp.concatenate([c, c_ctx[None, :], jnp.zeros((16 - B - 1, D_MODEL), F32)], 0)
    ada = _ada(cc, w_ada, b_ada)
    mods_all = ada[:, :B + 1].reshape(L, B + 1, 6, D_MODEL)
    mods_all = jnp.concatenate([mods_all, jnp.zeros((L, B + 1, 2, D_MODEL), F32)], 2)

    xc, xl, lat_off = ctx, x, -1
    row = lambda a: a[None, :]
    for l in range(L):
        last = l == L - 1
        first_tile = 1 if last else 0
        mods = mods_all[l]
        win_ext, wuq_ext, wuk, wvt, cw, pw, wout, wr, br = _prep_layer(
            w_in[l], w_uq[l], w_ukv[l], conv_w[l], pool_w[l], w_out[l], w_router_group[l],
            b_router_group[l], w_router_expert[l], b_router_expert[l])

        q, k, v, u, poolin = _inproj(xc, xl, lat_off, mods, tabs, win_ext, row(g_q[l]), wuq_ext, row(g_kv[l]),
                                     wuk, wvt)
        attn = _attention(q, k, v)
        attn_ctx = None if last else _attention_ctx(q, k, v)
        x1, xloc, rinfo, ctab, totals = _mixout(
            attn, attn_ctx, u, poolin, xc, xl, lat_off, mods, cw, row(conv_b[l]), row(conv_ln_g[l]),
            row(conv_ln_b[l]), pw, row(pool_scale[l]), wout, row(ln1_g[l]), row(ln1_b[l]), wr, br,
            tri, upper, first_tile)

        n_tok_tiles = B * (N_TILES - first_tile)
        src_tab, dst_tab, tile_expert = _routing_tables(ctab, totals, n_tok_tiles)
        ys = _experts(tile_expert, src_tab, xloc, w_gate, w_up, w_down, l)
        xl = _combine(dst_tab, ys, rinfo, x1, mods, row(ln2_g[l]), row(ln2_b[l]), first_tile)
        xc, lat_off = xl, 0
    return xl
```

```python
import functools
import math

import jax
import jax.numpy as jnp
from jax import lax
from jax.experimental import pallas as pl
from jax.experimental.pallas import tpu as pltpu

D_MODEL = 1024
SEQ = 4096
CTX_LEN = 256
SEQ_ALL = CTX_LEN + SEQ
GRID_W = 64
N_HEADS = 8
QK_NOPE = 64
QK_ROPE = 32
QK_HEAD = QK_NOPE + QK_ROPE
V_HEAD = 64
Q_RANK = 384
KV_RANK = 256
D_CONV = 256
CONV_WIDTH = 31
D_POOL = 256
POOL_WINDOWS = (2, 4, 8, 16)
ROPE_THETA = 10000.0
OFF_KV = Q_RANK
OFF_KR = OFF_KV + KV_RANK
OFF_CONV = OFF_KR + QK_ROPE
OFF_POOL = OFF_CONV + 2 * D_CONV
N_GROUPS = 4
EXPERTS_PER_GROUP = 8
N_EXPERTS = 32
D_EXPERT = 256
LN_EPS = 1e-5
RMS_EPS = 1e-6
DEPTH = 2
DEEPNORM_ALPHA = (2 * DEPTH) ** 0.25

LANE = 128
SUBLANE = 8
TILE = 256
HALO = 16
N_TILES = SEQ_ALL // TILE
HEAD_PAD = 128
KEY_CHUNK = 256
VMEM_LIMIT = 56 * 1024 * 1024

CHUNK = 2 * SUBLANE
MAX_USED_CHUNKS = (2 * TILE + (CHUNK - 1) * N_EXPERTS) // CHUNK
LOCAL_ROWS = 1024
LOCAL_CHUNKS = LOCAL_ROWS // CHUNK
assert MAX_USED_CHUNKS < LOCAL_CHUNKS <= LANE
TILE_CHUNKS = TILE // CHUNK
EXPERT_TILES = 2

XOFF_Q = 0
XOFF_KV = 384
XOFF_KR = 640
XOFF_KRR = 768
XOFF_CONV = 896
XOFF_POOL = 1408
D_IN_EXT = 1664

BF16 = jnp.bfloat16
F32 = jnp.float32
I32 = jnp.int32


def _dot(a, b):
    return jnp.dot(a, b, preferred_element_type=F32)


def _silu(x):
    return x * jax.nn.sigmoid(x)


def _layer_norm_rows(x, g, b):
    mu = jnp.mean(x, axis=-1, keepdims=True)
    xc = x - mu
    var = jnp.mean(xc * xc, axis=-1, keepdims=True)
    return xc * lax.rsqrt(var + LN_EPS) * g + b


def _ada_kernel(cc_ref, w_ref, b_ref, o_ref):
    a = _silu(cc_ref[...])
    o_ref[0] = jnp.dot(a, w_ref[0], preferred_element_type=F32,
                       precision=lax.Precision.HIGHEST) + b_ref[0]


def _ada(cc, w_ada, b_ada):
    L = w_ada.shape[0]
    n = w_ada.shape[2]
    bn = 512
    return pl.pallas_call(
        _ada_kernel,
        out_shape=jax.ShapeDtypeStruct((L, cc.shape[0], n), F32),
        grid=(L, n // bn),
        in_specs=[
            pl.BlockSpec(cc.shape, lambda l, j: (0, 0)),
            pl.BlockSpec((1, D_MODEL, bn), lambda l, j: (l, 0, j)),
            pl.BlockSpec((1, 1, bn), lambda l, j: (l, 0, j)),
        ],
        out_specs=pl.BlockSpec((1, cc.shape[0], bn), lambda l, j: (l, 0, j)),
        compiler_params=pltpu.CompilerParams(
            dimension_semantics=("parallel", "parallel"), vmem_limit_bytes=VMEM_LIMIT),
        name="ada",
    )(cc, w_ada, b_ada.reshape(L, 1, n))


def _inproj_kernel(xc_ref, xl_ref, mod_ref, cq_ref, sq_ref, ck_ref, sk_ref, win_ref, gq_ref, wuq_ref,
                   gkv_ref, wuk_ref, wvt_ref, q_ref, k_ref, vt_ref, u_ref, pool_ref):
    x = jnp.where(pl.program_id(1) == 0, xc_ref[0], xl_ref[0])
    sh = mod_ref[0, 0:1, :]
    sc = mod_ref[0, 1:2, :]
    h = (x * (1.0 + sc) + sh).astype(BF16)
    p = _dot(h, win_ref[...])

    pq = p[:, XOFF_Q:XOFF_KV]
    qn = pq * lax.rsqrt(jnp.mean(pq * pq, axis=-1, keepdims=True) + RMS_EPS) * gq_ref[...]
    q2 = _dot(qn.astype(BF16), wuq_ref[...])
    cq = cq_ref[...]
    sq = sq_ref[...]
    nq = N_HEADS * HEAD_PAD
    for hd in range(N_HEADS):
        lo = hd * HEAD_PAD
        qh = q2[:, lo:lo + HEAD_PAD] * cq + q2[:, nq + lo:nq + lo + HEAD_PAD] * sq
        q_ref[0, :, lo:lo + HEAD_PAD] = qh.astype(BF16)

    pkv = p[:, XOFF_KV:XOFF_KR]
    kvn = pkv * lax.rsqrt(jnp.mean(pkv * pkv, axis=-1, keepdims=True) + RMS_EPS) * gkv_ref[...]
    kvn = kvn.astype(BF16)
    kn = _dot(kvn, wuk_ref[...])
    kr = p[:, XOFF_KR:XOFF_KRR] * ck_ref[...] + p[:, XOFF_KRR:XOFF_CONV] * sk_ref[...]
    for hd in range(N_HEADS):
        lo = hd * HEAD_PAD
        k_ref[0, :, lo:lo + HEAD_PAD] = (kn[:, lo:lo + HEAD_PAD] + kr).astype(BF16)
    vt = lax.dot_general(wvt_ref[...], kvn, (((1,), (1,)), ((), ())), preferred_element_type=F32)
    vrow = lax.broadcasted_iota(I32, (N_HEADS * HEAD_PAD, TILE), 0)
    vt = jnp.where((vrow & (HEAD_PAD - 1)) == V_HEAD, 1.0, vt).astype(BF16)
    for hd in range(N_HEADS):
        vt_ref[0, hd] = vt[hd * HEAD_PAD:(hd + 1) * HEAD_PAD, :]

    a = p[:, XOFF_CONV:XOFF_CONV + D_CONV]
    gt = p[:, XOFF_CONV + D_CONV:XOFF_POOL]
    u_ref[0] = (a * jax.nn.sigmoid(gt)).astype(BF16)
    pool_ref[0] = p[:, XOFF_POOL:]


def _inproj(xc, xl, lat_off, mods, tabs, win_ext, gq, wuq_ext, gkv, wuk, wvt):
    B = xl.shape[0]
    nb = mods.shape[0] - 1

    def mod_idx(b, i):
        return (jnp.where(i == 0, nb, b), 0, 0)

    tab_spec = pl.BlockSpec((TILE, LANE), lambda b, i: (i, 0))
    full = lambda a: pl.BlockSpec(a.shape, lambda b, i: (0,) * a.ndim)
    tok = lambda w: pl.BlockSpec((1, TILE, w), lambda b, i: (b, i, 0))
    hw = N_HEADS * HEAD_PAD
    return pl.pallas_call(
        _inproj_kernel,
        out_shape=(
            jax.ShapeDtypeStruct((B, SEQ_ALL, hw), BF16),
            jax.ShapeDtypeStruct((B, SEQ_ALL, hw), BF16),
            jax.ShapeDtypeStruct((B, N_HEADS, HEAD_PAD, SEQ_ALL), BF16),
            jax.ShapeDtypeStruct((B, SEQ_ALL, D_CONV), BF16),
            jax.ShapeDtypeStruct((B, SEQ_ALL, D_POOL), F32),
        ),
        grid=(B, N_TILES),
        in_specs=[pl.BlockSpec((1, TILE, D_MODEL), lambda b, i: (b, 0, 0)),
                  pl.BlockSpec((1, TILE, D_MODEL), lambda b, i: (b, jnp.maximum(i + lat_off, 0), 0)),
                  pl.BlockSpec((1, 8, D_MODEL), mod_idx),
                  tab_spec, tab_spec, tab_spec, tab_spec,
                  full(win_ext), full(gq), full(wuq_ext), full(gkv), full(wuk), full(wvt)],
        out_specs=(tok(hw), tok(hw),
                   pl.BlockSpec((1, N_HEADS, HEAD_PAD, TILE), lambda b, i: (b, 0, 0, i)),
                   tok(D_CONV), tok(D_POOL)),
        compiler_params=pltpu.CompilerParams(
            dimension_semantics=("parallel", "parallel"), vmem_limit_bytes=VMEM_LIMIT),
        name="inproj",
    )(xc, xl, mods, *tabs, win_ext, gq, wuq_ext, gkv, wuk, wvt)


def _scores_t(q, k_ref, n_keys):
    lane = lax.broadcasted_iota(I32, q.shape, 1)
    qbd = jnp.concatenate([jnp.where(lane < HEAD_PAD, q, jnp.zeros_like(q)),
                           jnp.where(lane >= HEAD_PAD, q, jnp.zeros_like(q))], axis=0)
    return lax.dot_general(k_ref[0, 0:n_keys, :], qbd, (((1,), (1,)), ((), ())),
                           preferred_element_type=F32)


def _store_scores(s_ref, m_ref, st):
    s_ref[...] = st
    m_ref[...] = jnp.broadcast_to(jnp.max(st, axis=0, keepdims=True), m_ref.shape)


def _softmax_values_t(s_ref, m_ref, vt_ref, p_ref, n_keys):
    nq = s_ref.shape[1] // 2
    m = m_ref[0:1, :]
    for kc in range(0, n_keys, KEY_CHUNK):
        p_ref[kc:kc + KEY_CHUNK, :] = jnp.exp2(s_ref[kc:kc + KEY_CHUNK, :] - m).astype(BF16)
    outs = []
    for hd in range(2):
        ot = _dot(vt_ref[0, hd, :, 0:n_keys], p_ref[0:n_keys, hd * nq:(hd + 1) * nq])
        outs.append(ot[0:V_HEAD, :] / ot[V_HEAD:V_HEAD + 1, :])
    return jnp.concatenate(outs, axis=0).T.astype(BF16)


def _attn_kernel(qca_ref, qcb_ref, qna_ref, qnb_ref, k_ref, vt_ref, o_ref,
                 sa_ref, sb_ref, ma_ref, mb_ref, p_ref):
    i = pl.program_id(2)

    @pl.when(i == 0)
    def _():
        qc = jnp.concatenate([qca_ref[0], qcb_ref[0]], axis=0)
        _store_scores(sa_ref, ma_ref, _scores_t(qc, k_ref, SEQ_ALL))

    def step(cur_ref, cur_m_ref, nxt_ref, nxt_m_ref):
        qn = jnp.concatenate([qna_ref[0], qnb_ref[0]], axis=0)
        _store_scores(nxt_ref, nxt_m_ref, _scores_t(qn, k_ref, SEQ_ALL))
        o_ref[0] = _softmax_values_t(cur_ref, cur_m_ref, vt_ref, p_ref, SEQ_ALL)

    parity = lax.rem(i, 2)

    @pl.when(parity == 0)
    def _():
        step(sa_ref, ma_ref, sb_ref, mb_ref)

    @pl.when(parity == 1)
    def _():
        step(sb_ref, mb_ref, sa_ref, ma_ref)


def _attention(q, k, v):
    B = q.shape[0]
    n_pairs = N_HEADS // 2
    pw = 2 * HEAD_PAD
    n_steps = SEQ // (2 * TILE)
    assert n_steps % 2 == 0
    last = SEQ_ALL // TILE - 1
    qspec = lambda f: pl.BlockSpec((1, TILE, pw), f)
    cols = 4 * TILE
    return pl.pallas_call(
        _attn_kernel,
        out_shape=jax.ShapeDtypeStruct((B, SEQ, N_HEADS * V_HEAD), BF16),
        grid=(B, n_pairs, n_steps),
        in_specs=[
            qspec(lambda b, hp, i: (b, 2 * i + 1, hp)),
            qspec(lambda b, hp, i: (b, 2 * i + 2, hp)),
            qspec(lambda b, hp, i: (b, jnp.minimum(2 * i + 3, last), hp)),
            qspec(lambda b, hp, i: (b, jnp.minimum(2 * i + 4, last), hp)),
            pl.BlockSpec((1, SEQ_ALL, pw), lambda b, hp, i: (b, 0, hp)),
            pl.BlockSpec((1, 2, HEAD_PAD, SEQ_ALL), lambda b, hp, i: (b, hp, 0, 0)),
        ],
        out_specs=pl.BlockSpec((1, 2 * TILE, 2 * V_HEAD), lambda b, hp, i: (b, i, hp)),
        scratch_shapes=[pltpu.VMEM((SEQ_ALL, cols), F32), pltpu.VMEM((SEQ_ALL, cols), F32),
                        pltpu.VMEM((SUBLANE, cols), F32), pltpu.VMEM((SUBLANE, cols), F32),
                        pltpu.VMEM((SEQ_ALL, cols), BF16)],
        compiler_params=pltpu.CompilerParams(
            dimension_semantics=("arbitrary", "arbitrary", "arbitrary"),
            vmem_limit_bytes=VMEM_LIMIT),
        name="attn",
    )(q, q, q, q, k, v)


def _attn_ctx_kernel(q_ref, k_ref, vt_ref, o_ref, s_ref, m_ref, p_ref):
    _store_scores(s_ref, m_ref, _scores_t(q_ref[0], k_ref, CTX_LEN))
    o_ref[0] = _softmax_values_t(s_ref, m_ref, vt_ref, p_ref, CTX_LEN)


def _attention_ctx(q, k, v):
    B = q.shape[0]
    pw = 2 * HEAD_PAD
    blk = pl.BlockSpec((1, CTX_LEN, pw), lambda b, hp: (b, 0, hp))
    return pl.pallas_call(
        _attn_ctx_kernel,
        out_shape=jax.ShapeDtypeStruct((B, CTX_LEN, N_HEADS * V_HEAD), BF16),
        grid=(B, N_HEADS // 2),
        in_specs=[blk, blk, pl.BlockSpec((1, 2, HEAD_PAD, CTX_LEN), lambda b, hp: (b, hp, 0, 0))],
        out_specs=pl.BlockSpec((1, CTX_LEN, 2 * V_HEAD), lambda b, hp: (b, 0, hp)),
        scratch_shapes=[pltpu.VMEM((CTX_LEN, 2 * TILE), F32), pltpu.VMEM((SUBLANE, 2 * TILE), F32),
                        pltpu.VMEM((CTX_LEN, 2 * TILE), BF16)],
        compiler_params=pltpu.CompilerParams(
            dimension_semantics=("parallel", "parallel"), vmem_limit_bytes=VMEM_LIMIT),
        name="attn_ctx",
    )(q, k, v)


def _mixout_kernel(*refs, first_tile):
    if first_tile == 0:
        attn_ref, actx_ref = refs[0], refs[1]
        refs = refs[2:]
    else:
        attn_ref, actx_ref = refs[0], None
        refs = refs[1:]
    (u_ref, ul_ref, ur_ref, pm_ref, pl_ref, pr_ref, xc_ref, xl_ref, mod_ref,
     cw_ref, cb_ref, cg_ref, cbeta_ref, pw_ref, ps_ref, wout_ref, g1_ref, b1_ref,
     wr_ref, br_ref, tri_ref, upper_ref,
     x1_ref, xloc_ref, rinfo_ref, ctab_ref, tot_ref,
     winu, winp, shu, shp, cnt) = refs
    i = pl.program_id(1) + first_tile
    left_ok = i >= 2
    right_ok = (i >= 1) & (i <= N_TILES - 2)

    @pl.when((pl.program_id(0) == 0) & (pl.program_id(1) == 0))
    def _():
        cnt[...] = jnp.zeros_like(cnt)

    zu = jnp.zeros((HALO, D_CONV), F32)
    winu[0:HALO, :] = jnp.where(left_ok, ul_ref[0].astype(F32), zu)
    winu[HALO:HALO + TILE, :] = u_ref[0].astype(F32)
    winu[HALO + TILE:, :] = jnp.where(right_ok, ur_ref[0].astype(F32), zu)
    winp[0:HALO, :] = jnp.where(left_ok, pl_ref[0], zu)
    winp[HALO:HALO + TILE, :] = pm_ref[0]
    winp[HALO + TILE:, :] = jnp.where(right_ok, pr_ref[0], zu)
    ph_rows = TILE + 2 * HALO - SUBLANE
    for ph in range(SUBLANE):
        shu[ph] = winu[ph:ph + ph_rows, :]
        shp[ph] = winp[ph:ph + ph_rows, :]

    def shifted(buf, off):
        ph, base = off % SUBLANE, off - off % SUBLANE
        return buf[ph, base:base + TILE, :]

    acc = jnp.zeros((TILE, D_CONV), F32)
    for j in range(CONV_WIDTH):
        acc = acc + shifted(shu, HALO - CONV_WIDTH // 2 + j) * cw_ref[j:j + 1, :]
    conv = _silu(_layer_norm_rows(acc + cb_ref[...], cg_ref[...], cbeta_ref[...]))

    def pw(o):
        return shifted(shp, HALO + o)

    centre = pw(0)
    s2 = pw(-1) + centre
    s4 = s2 + pw(-2) + pw(1)
    s8 = s4 + pw(-4) + pw(-3) + pw(2) + pw(3)
    s16 = s8 + pw(-8) + pw(-7) + pw(-6) + pw(-5) + pw(4) + pw(5) + pw(6) + pw(7)
    row = lax.broadcasted_iota(I32, (TILE, 1), 0)
    seg_len = jnp.where(i == 0, CTX_LEN, SEQ)
    t = jnp.where(i == 0, 0, (i - 1) * TILE) + row

    def mean(sw, w):
        hi = jnp.minimum(t + (w // 2 - 1), seg_len - 1)
        lo = jnp.maximum(t - w // 2, 0)
        return sw / (hi - lo + 1).astype(F32)

    lane_p = lax.broadcasted_iota(I32, (1, D_POOL), 1)
    gdim = D_POOL // len(POOL_WINDOWS)
    mixed = jnp.where(lane_p < gdim, mean(s2, 2),
                      jnp.where(lane_p < 2 * gdim, mean(s4, 4),
                                jnp.where(lane_p < 3 * gdim, mean(s8, 8), mean(s16, 16)))) - centre
    pool = _dot(mixed.astype(BF16), pw_ref[...]) * ps_ref[...]

    d_attn = N_HEADS * V_HEAD
    attn = attn_ref[0]
    if actx_ref is not None:
        attn = jnp.where(i == 0, actx_ref[0], attn)
    mix = (_dot(attn, wout_ref[0:d_attn, :])
           + _dot(conv.astype(BF16), wout_ref[d_attn:d_attn + D_CONV, :])
           + _dot(pool.astype(BF16), wout_ref[d_attn + D_CONV:, :]))
    gate1 = mod_ref[0, 2:3, :]
    x_in = jnp.where(i == 0, xc_ref[0], xl_ref[0])
    x1 = _layer_norm_rows(DEEPNORM_ALPHA * x_in + gate1 * mix, g1_ref[...], b1_ref[...])
    x1_ref[0] = x1

    h2 = (x1 * (1.0 + mod_ref[0, 4:5, :]) + mod_ref[0, 3:4, :]).astype(BF16)
    logits = _dot(h2, wr_ref[...]) + br_ref[...]
    lane = lax.broadcasted_iota(I32, (TILE, LANE), 1)
    neg = jnp.float32(-jnp.inf)
    big = jnp.int32(1 << 20)
    is_g = (lane >= N_EXPERTS) & (lane < N_EXPERTS + N_GROUPS)
    gl = jnp.where(is_g, logits, neg)
    gmax = jnp.max(gl, axis=-1, keepdims=True)
    g_sel = jnp.min(jnp.where(gl == gmax, lane - N_EXPERTS, big), axis=-1, keepdims=True)
    p_sel = 1.0 / jnp.sum(jnp.exp(gl - gmax), axis=-1, keepdims=True)
    in_grp = (lane >= g_sel * EXPERTS_PER_GROUP) & (lane < (g_sel + 1) * EXPERTS_PER_GROUP)
    el = jnp.where(in_grp, logits, neg)
    v1 = jnp.max(el, axis=-1, keepdims=True)
    i1 = jnp.min(jnp.where(el == v1, lane, big), axis=-1, keepdims=True)
    el2 = jnp.where(lane == i1, neg, el)
    v2 = jnp.max(el2, axis=-1, keepdims=True)
    i2 = jnp.min(jnp.where(el2 == v2, lane, big), axis=-1, keepdims=True)
    e2 = jnp.exp(v2 - v1)
    w1 = p_sel / (1.0 + e2)
    w2 = p_sel * e2 / (1.0 + e2)

    oh1 = (lane == i1).astype(F32)
    oh2 = (lane == i2).astype(F32)
    both = oh1 + oh2
    n_e = jnp.sum(both, axis=0, keepdims=True)
    chunks_e = jnp.floor((n_e + (CHUNK - 1)) * (1.0 / CHUNK))
    chunks8 = jnp.broadcast_to(chunks_e, (SUBLANE, LANE))
    start_e = _dot(chunks8.astype(BF16), upper_ref[...])[0:1, :]
    slot_mat = _dot(tri_ref[...], both.astype(BF16)) + CHUNK * start_e
    s1 = jnp.sum(oh1 * slot_mat, axis=-1, keepdims=True)
    s2_ = jnp.sum(oh2 * slot_mat, axis=-1, keepdims=True)
    fields = (w1, w2, s1, s2_)
    rinfo = jnp.zeros((TILE, LANE), F32)
    for n, f in enumerate(fields):
        rinfo = jnp.where(lane == n, f, rinfo)
    rinfo_ref[...] = rinfo

    slot_lane = lax.broadcasted_iota(I32, (TILE, LOCAL_ROWS), 1).astype(F32)
    perm = ((slot_lane == s1).astype(F32) + (slot_lane == s2_).astype(F32)).astype(BF16)
    sorted_rows = lax.dot_general(perm, h2, (((0,), (0,)), ((), ())), preferred_element_type=F32)
    xloc_ref[...] = sorted_rows.astype(BF16)

    done_e = cnt[0:1, :]
    cidx = lax.broadcasted_iota(I32, (LANE, LANE), 0).astype(F32)
    lane_sq = lax.broadcasted_iota(I32, (LANE, LANE), 1)
    owns = ((cidx >= start_e) & (cidx < start_e + chunks_e)).astype(F32)
    e_of_c = jnp.sum(owns * lane_sq.astype(F32), axis=-1, keepdims=True)
    rel_c = jnp.sum(owns * (done_e + cidx - start_e), axis=-1, keepdims=True)
    used_c = jnp.sum(owns, axis=-1, keepdims=True)
    tab = jnp.where(lane_sq == 0, e_of_c, jnp.where(lane_sq == 1, rel_c,
                                                    jnp.where(lane_sq == 2, used_c, 0.0)))
    ctab_ref[0] = tab.T[0:SUBLANE, :]
    cnt[...] = cnt[...] + chunks_e
    tot_ref[...] = cnt[...]


def _mixout(attn, attn_ctx, u, poolin, xc, xl, lat_off, mods, cw, cb, cg, cbeta, pw, ps, wout, g1, b1, wr, br,
            tri, upper, first_tile):
    B = xl.shape[0]
    nt = N_TILES - first_tile
    nb = mods.shape[0] - 1
    hb = TILE // HALO
    n_hb = SEQ_ALL // HALO

    def mod_idx(b, i):
        return (jnp.where(i + first_tile == 0, nb, b), 0, 0)

    tok = lambda w: pl.BlockSpec((1, TILE, w), lambda b, i: (b, i + first_tile, 0))
    own = lambda w: pl.BlockSpec((1, TILE, w), lambda b, i: (b, i, 0))
    left = lambda w: pl.BlockSpec(
        (1, HALO, w), lambda b, i: (b, jnp.maximum((i + first_tile) * hb - 1, 0), 0))
    right = lambda w: pl.BlockSpec(
        (1, HALO, w), lambda b, i: (b, jnp.minimum((i + first_tile + 1) * hb, n_hb - 1), 0))
    full = lambda a: pl.BlockSpec(a.shape, lambda b, i: (0,) * a.ndim)
    d_attn = N_HEADS * V_HEAD
    attn_specs = [pl.BlockSpec((1, TILE, d_attn),
                               lambda b, i: (b, jnp.maximum(i + first_tile - 1, 0), 0))]
    attn_args = [attn]
    if first_tile == 0:
        attn_specs.append(pl.BlockSpec((1, TILE, d_attn), lambda b, i: (b, 0, 0)))
        attn_args.append(attn_ctx)
    n_tok_tiles = B * nt
    ph_rows = TILE + 2 * HALO - SUBLANE
    return pl.pallas_call(
        functools.partial(_mixout_kernel, first_tile=first_tile),
        out_shape=(
            jax.ShapeDtypeStruct((B, nt * TILE, D_MODEL), F32),
            jax.ShapeDtypeStruct((n_tok_tiles * LOCAL_ROWS, D_MODEL), BF16),
            jax.ShapeDtypeStruct((n_tok_tiles * TILE, LANE), F32),
            jax.ShapeDtypeStruct((n_tok_tiles, SUBLANE, LANE), F32),
            jax.ShapeDtypeStruct((SUBLANE, LANE), F32),
        ),
        grid=(B, nt),
        in_specs=attn_specs + [
            tok(D_CONV), left(D_CONV), right(D_CONV),
            tok(D_POOL), left(D_POOL), right(D_POOL),
            pl.BlockSpec((1, TILE, D_MODEL), lambda b, i: (b, 0, 0)),
            pl.BlockSpec((1, TILE, D_MODEL),
                         lambda b, i: (b, jnp.maximum(i + first_tile + lat_off, 0), 0)),
            pl.BlockSpec((1, 8, D_MODEL), mod_idx),
            full(cw), full(cb), full(cg), full(cbeta), full(pw), full(ps), full(wout),
            full(g1), full(b1), full(wr), full(br), full(tri), full(upper)],
        out_specs=(own(D_MODEL),
                   pl.BlockSpec((LOCAL_ROWS, D_MODEL), lambda b, i: (b * nt + i, 0)),
                   pl.BlockSpec((TILE, LANE), lambda b, i: (b * nt + i, 0)),
                   pl.BlockSpec((1, SUBLANE, LANE), lambda b, i: (b * nt + i, 0, 0)),
                   pl.BlockSpec((SUBLANE, LANE), lambda b, i: (0, 0))),
        scratch_shapes=[pltpu.VMEM((TILE + 2 * HALO, D_CONV), F32),
                        pltpu.VMEM((TILE + 2 * HALO, D_POOL), F32),
                        pltpu.VMEM((SUBLANE, ph_rows, D_CONV), F32),
                        pltpu.VMEM((SUBLANE, ph_rows, D_POOL), F32),
                        pltpu.VMEM((SUBLANE, LANE), F32)],
        compiler_params=pltpu.CompilerParams(
            dimension_semantics=("arbitrary", "arbitrary"), vmem_limit_bytes=VMEM_LIMIT),
        name="mixout",
    )(*attn_args, u, u, u, poolin, poolin, poolin, xc, xl, mods, cw, cb, cg, cbeta, pw, ps, wout,
      g1, b1, wr, br, tri, upper)


def _chunk_copy(src, src_chunk, buf, dst_chunk, sem):
    src_row = src_chunk * CHUNK
    if not isinstance(src_row, int):
        src_row = pl.multiple_of(src_row, CHUNK)
    return pltpu.make_async_copy(
        src.at[pl.ds(src_row, CHUNK)], buf.at[pl.ds(dst_chunk * CHUNK, CHUNK)], sem)


def _start_gather(src, table_ref, n_chunks, buf, sem):
    for c in range(n_chunks):
        _chunk_copy(src, table_ref[0, 0, c], buf, c, sem).start()


def _wait_gather(src, n_chunks, buf, sem):
    for c in range(n_chunks):
        _chunk_copy(src, 0, buf, c, sem).wait()


def _double_buffered_gather(step, n_steps, src, cur_tab, nxt_tab, n_chunks, bufs, sems, compute):
    @pl.when(step == 0)
    def _():
        _start_gather(src, cur_tab, n_chunks, bufs[0], sems.at[0])

    for par in range(2):
        @pl.when((lax.rem(step, 2) == par) & (step < n_steps))
        def _(par=par):
            _start_gather(src, nxt_tab, n_chunks, bufs[1 - par], sems.at[1 - par])
            _wait_gather(src, n_chunks, bufs[par], sems.at[par])
            compute(bufs[par])

            @pl.when(step == n_steps - 1)
            def _():
                _wait_gather(src, n_chunks, bufs[1 - par], sems.at[1 - par])


def _experts_kernel(te_ref, ns_ref, cur_ref, nxt_ref, xloc_ref, wga_ref, wua_ref, wda_ref,
                    wgb_ref, wub_ref, wdb_ref, ys_ref,
                    xbuf_a, xbuf_b, cga, cua, cda, cgb, cub, cdb, sem):
    i = pl.program_id(0)
    w_refs = ((wga_ref, wua_ref, wda_ref), (wgb_ref, wub_ref, wdb_ref))
    caches = ((cga, cua, cda), (cgb, cub, cdb))

    for t in range(EXPERT_TILES):
        tile = EXPERT_TILES * i + t
        prev = jnp.maximum(tile - EXPERT_TILES, 0)

        @pl.when((i == 0) | (te_ref[tile] != te_ref[prev]))
        def _(t=t):
            for cache, w_ref in zip(caches[t], w_refs[t]):
                cache[...] = w_ref[0, 0].astype(BF16)

    def compute(xbuf):
        for t in range(EXPERT_TILES):
            wg, wu, wd = caches[t]
            xt = xbuf[t * TILE:(t + 1) * TILE, :]
            hid = (_silu(_dot(xt, wg[...])) * _dot(xt, wu[...])).astype(BF16)
            ys_ref[t * TILE:(t + 1) * TILE, :] = _dot(hid, wd[...]).astype(BF16)

    _double_buffered_gather(i, ns_ref[0], xloc_ref, cur_ref, nxt_ref, EXPERT_TILES * TILE_CHUNKS,
                            (xbuf_a, xbuf_b), sem, compute)

    @pl.when(i >= ns_ref[0])
    def _():
        ys_ref[...] = jnp.zeros_like(ys_ref)


def _experts(tile_expert, n_used_steps, src_tab, xloc, w_gate, w_up, w_down, layer):
    n_steps = src_tab.shape[0]
    rows = EXPERT_TILES * TILE

    def w_idx(t):
        return lambda i, te, ns: (layer, te[EXPERT_TILES * i + t], 0, 0)

    tab = lambda f: pl.BlockSpec((1, 1, EXPERT_TILES * TILE_CHUNKS), f, memory_space=pltpu.SMEM)
    w_specs, w_cache = [], []
    for t in range(EXPERT_TILES):
        w_specs += [pl.BlockSpec((1, 1, D_MODEL, D_EXPERT), w_idx(t)),
                    pl.BlockSpec((1, 1, D_MODEL, D_EXPERT), w_idx(t)),
                    pl.BlockSpec((1, 1, D_EXPERT, D_MODEL), w_idx(t))]
        w_cache += [pltpu.VMEM((D_MODEL, D_EXPERT), BF16), pltpu.VMEM((D_MODEL, D_EXPERT), BF16),
                    pltpu.VMEM((D_EXPERT, D_MODEL), BF16)]
    return pl.pallas_call(
        _experts_kernel,
        out_shape=jax.ShapeDtypeStruct((n_steps * rows, D_MODEL), BF16),
        grid_spec=pltpu.PrefetchScalarGridSpec(
            num_scalar_prefetch=2,
            grid=(n_steps,),
            in_specs=[tab(lambda i, te, ns: (i, 0, 0)),
                      tab(lambda i, te, ns: (jnp.minimum(i + 1, n_steps - 1), 0, 0)),
                      pl.BlockSpec(memory_space=pl.ANY)] + w_specs,
            out_specs=pl.BlockSpec((rows, D_MODEL), lambda i, te, ns: (i, 0)),
            scratch_shapes=[pltpu.VMEM((rows, D_MODEL), BF16), pltpu.VMEM((rows, D_MODEL), BF16)]
            + w_cache + [pltpu.SemaphoreType.DMA((2,))],
        ),
        compiler_params=pltpu.CompilerParams(
            dimension_semantics=("arbitrary",), vmem_limit_bytes=VMEM_LIMIT),
        name="experts",
    )(tile_expert, n_used_steps, src_tab, src_tab, xloc, *([w_gate, w_up, w_down] * EXPERT_TILES))


def _combine_kernel(cur_ref, nxt_ref, ys_ref, rinfo_ref, x_ref, mod_ref, g2_ref, b2_ref, o_ref,
                    ybuf_a, ybuf_b, sem, *, n_tok_tiles):
    g = pl.program_id(0) * pl.num_programs(1) + pl.program_id(1)

    def compute(ybuf):
        slot_lane = lax.broadcasted_iota(I32, (TILE, LOCAL_ROWS), 1).astype(F32)
        sel = (jnp.where(slot_lane == rinfo_ref[:, 2:3], rinfo_ref[:, 0:1], 0.0)
               + jnp.where(slot_lane == rinfo_ref[:, 3:4], rinfo_ref[:, 1:2], 0.0))
        y = _dot(sel.astype(BF16), ybuf[...])
        gate2 = mod_ref[0, 5:6, :]
        o_ref[0] = _layer_norm_rows(DEEPNORM_ALPHA * x_ref[0] + gate2 * y, g2_ref[...], b2_ref[...])

    _double_buffered_gather(g, n_tok_tiles, ys_ref, cur_ref, nxt_ref, LOCAL_CHUNKS,
                            (ybuf_a, ybuf_b), sem, compute)


def _combine(dst_tab, ys, rinfo, x1, mods, g2, b2, first_tile):
    B = x1.shape[0]
    nt = N_TILES - first_tile
    nb = mods.shape[0] - 1
    n_tok_tiles = B * nt

    def mod_idx(b, i):
        return (jnp.where(i + first_tile == 0, nb, b), 0, 0)

    full = lambda a: pl.BlockSpec(a.shape, lambda b, i: (0,) * a.ndim)
    tab = lambda f: pl.BlockSpec((1, 1, LANE), f, memory_space=pltpu.SMEM)
    return pl.pallas_call(
        functools.partial(_combine_kernel, n_tok_tiles=n_tok_tiles),
        out_shape=jax.ShapeDtypeStruct((B, nt * TILE, D_MODEL), F32),
        grid=(B, nt),
        in_specs=[tab(lambda b, i: (b * nt + i, 0, 0)),
                  tab(lambda b, i: (jnp.minimum(b * nt + i + 1, n_tok_tiles - 1), 0, 0)),
                  pl.BlockSpec(memory_space=pl.ANY),
                  pl.BlockSpec((TILE, LANE), lambda b, i: (b * nt + i, 0)),
                  pl.BlockSpec((1, TILE, D_MODEL), lambda b, i: (b, i, 0)),
                  pl.BlockSpec((1, 8, D_MODEL), mod_idx),
                  full(g2), full(b2)],
        out_specs=pl.BlockSpec((1, TILE, D_MODEL), lambda b, i: (b, i, 0)),
        scratch_shapes=[pltpu.VMEM((LOCAL_ROWS, D_MODEL), BF16), pltpu.VMEM((LOCAL_ROWS, D_MODEL), BF16),
                        pltpu.SemaphoreType.DMA((2,))],
        compiler_params=pltpu.CompilerParams(
            dimension_semantics=("arbitrary", "arbitrary"), vmem_limit_bytes=VMEM_LIMIT),
        name="combine",
    )(dst_tab, dst_tab, ys, rinfo, x1, mods, g2, b2)


def _rot_cols(w):
    w4 = w.reshape(w.shape[:-1] + (2, 2, QK_ROPE // 4))
    return jnp.stack([-w4[..., 1, :], w4[..., 0, :]], axis=-2).reshape(w.shape)


def _rope_tables():
    rows = SEQ // GRID_W
    row = jnp.repeat(jnp.arange(rows), GRID_W)
    col = jnp.tile(jnp.arange(GRID_W), rows)
    d_axis = QK_ROPE // 2
    inv_freq = jnp.power(ROPE_THETA, -jnp.arange(0, d_axis, 2, dtype=F32) / d_axis)

    def axis_angles(p):
        a = p.astype(F32)[:, None] * inv_freq[None, :]
        return jnp.concatenate([a, a], -1)

    ang = jnp.concatenate([axis_angles(row), axis_angles(col)], -1)
    cos = jnp.concatenate([jnp.ones((CTX_LEN, QK_ROPE), F32), jnp.cos(ang)], 0)
    sin = jnp.concatenate([jnp.zeros((CTX_LEN, QK_ROPE), F32), jnp.sin(ang)], 0)
    scale = math.log2(math.e) / math.sqrt(QK_HEAD)
    z64 = jnp.zeros((SEQ_ALL, QK_NOPE), F32)
    z32 = jnp.zeros((SEQ_ALL, HEAD_PAD - QK_HEAD), F32)
    cq = jnp.concatenate([jnp.full((SEQ_ALL, QK_NOPE), scale, F32), scale * cos, z32], -1)
    sq = jnp.concatenate([z64, scale * sin, z32], -1)
    ck = jnp.concatenate([z64, cos, z32], -1)
    sk = jnp.concatenate([z64, sin, z32], -1)
    return cq, sq, ck, sk


def _prep_layer(w_in, w_uq, w_ukv, conv_w, pool_w, w_out, w_rg, b_rg, w_re, b_re):
    zc = lambda r, c: jnp.zeros((r, c), F32)
    kr = w_in[:, OFF_KR:OFF_CONV]
    pad_l, pad_r = QK_NOPE, HEAD_PAD - QK_HEAD
    kr_arr = jnp.concatenate([zc(D_MODEL, pad_l), kr, zc(D_MODEL, pad_r)], -1)
    krr_arr = jnp.concatenate([zc(D_MODEL, pad_l), _rot_cols(kr), zc(D_MODEL, pad_r)], -1)
    win_ext = jnp.concatenate(
        [w_in[:, :OFF_KR], kr_arr, krr_arr, w_in[:, OFF_CONV:]], -1).astype(BF16)

    wq3 = w_uq.reshape(Q_RANK, N_HEADS, QK_HEAD)
    nope, rope = wq3[..., :QK_NOPE], wq3[..., QK_NOPE:]
    z3 = lambda c: jnp.zeros((Q_RANK, N_HEADS, c), F32)
    q_arr = jnp.concatenate([nope, rope, z3(pad_r)], -1).reshape(Q_RANK, N_HEADS * HEAD_PAD)
    q_rot = jnp.concatenate([z3(pad_l), _rot_cols(rope), z3(pad_r)], -1).reshape(Q_RANK, N_HEADS * HEAD_PAD)
    wuq_ext = jnp.concatenate([q_arr, q_rot], -1).astype(BF16)

    wkv3 = w_ukv.reshape(KV_RANK, N_HEADS, QK_NOPE + V_HEAD)
    zkv = jnp.zeros((KV_RANK, N_HEADS, HEAD_PAD - QK_NOPE), F32)
    k_arr = jnp.concatenate([wkv3[..., :QK_NOPE], zkv], -1).reshape(KV_RANK, N_HEADS * HEAD_PAD)
    v_arr = jnp.concatenate([wkv3[..., QK_NOPE:], zkv], -1).reshape(KV_RANK, N_HEADS * HEAD_PAD)
    wuk, wvt = k_arr.astype(BF16), v_arr.T.astype(BF16)

    cw = jnp.concatenate([conv_w, jnp.zeros((1, D_CONV), F32)], 0)
    gdim = D_POOL // len(POOL_WINDOWS)
    pw = jnp.zeros((D_POOL, D_POOL), F32)
    for g in range(len(POOL_WINDOWS)):
        pw = pw.at[g * gdim:(g + 1) * gdim, g * gdim:(g + 1) * gdim].set(pool_w[g])
    wr = jnp.concatenate([w_re, w_rg, zc(D_MODEL, LANE - N_EXPERTS - N_GROUPS)], -1).astype(BF16)
    br = jnp.concatenate([b_re, b_rg, jnp.zeros((LANE - N_EXPERTS - N_GROUPS,), F32)])[None, :]
    return win_ext, wuq_ext, wuk, wvt, cw, pw.astype(BF16), w_out.astype(BF16), wr, br


def _routing_tables(ctab, totals, n_tok_tiles):
    n_tiles_max = -(-(MAX_USED_CHUNKS * n_tok_tiles + (TILE_CHUNKS - 1) * N_EXPERTS) // TILE_CHUNKS)
    n_tiles_max = -(-n_tiles_max // EXPERT_TILES) * EXPERT_TILES
    n_chunks_max = n_tiles_max * TILE_CHUNKS
    chunks = totals[0, :N_EXPERTS].astype(I32)
    padded = ((chunks + TILE_CHUNKS - 1) // TILE_CHUNKS) * TILE_CHUNKS
    ends = jnp.cumsum(padded)
    starts = ends - padded
    e_of_c = ctab[:, 0, :LOCAL_CHUNKS].astype(I32)
    rel_c = ctab[:, 1, :LOCAL_CHUNKS].astype(I32)
    used = ctab[:, 2, :LOCAL_CHUNKS] > 0.5
    owner = e_of_c[..., None] == jnp.arange(N_EXPERTS, dtype=I32)
    dst = jnp.sum(jnp.where(owner, starts, 0), axis=-1) + rel_c
    local_id = (jnp.arange(n_tok_tiles, dtype=I32)[:, None] * LOCAL_CHUNKS
                + jnp.arange(LOCAL_CHUNKS, dtype=I32)[None, :])
    src = jnp.full((n_chunks_max,), LOCAL_CHUNKS - 1, I32)
    src = src.at[jnp.where(used, dst, n_chunks_max).reshape(-1)].set(local_id.reshape(-1), mode="drop")
    src_tab = src.reshape(n_tiles_max // EXPERT_TILES, 1, EXPERT_TILES * TILE_CHUNKS)
    dst_tab = jnp.where(used, dst, dst[:, 0:1])
    dst_tab = jnp.concatenate(
        [dst_tab, jnp.zeros((n_tok_tiles, LANE - LOCAL_CHUNKS), I32)], -1).reshape(n_tok_tiles, 1, LANE)
    n_valid = ends[-1] // TILE_CHUNKS
    tile_start = jnp.minimum(jnp.arange(n_tiles_max, dtype=I32), n_valid - 1) * TILE_CHUNKS
    tile_expert = jnp.sum((tile_start[:, None] >= ends[None, :]).astype(I32), axis=-1)
    n_used_steps = (n_valid + EXPERT_TILES - 1) // EXPERT_TILES
    return src_tab, dst_tab, tile_expert.astype(I32), n_used_steps.astype(I32).reshape(1)


def kernel(x, c, ctx, c_ctx, w_ada, b_ada, w_in, g_q, w_uq, g_kv, w_ukv, conv_w, conv_b, conv_ln_g,
           conv_ln_b, pool_w, pool_scale, w_out, ln1_g, ln1_b, w_router_group, b_router_group,
           w_router_expert, b_router_expert, w_gate, w_up, w_down, ln2_g, ln2_b):
    B = x.shape[0]
    L = w_ada.shape[0]
    tabs = _rope_tables()
    tri = jnp.tril(jnp.ones((TILE, TILE), F32), -1).astype(BF16)
    upper = jnp.triu(jnp.ones((LANE, LANE), F32), 1).astype(BF16)

    cc = jnp.concatenate([c, c_ctx[None, :], jnp.zeros((16 - B - 1, D_MODEL), F32)], 0)
    ada = _ada(cc, w_ada, b_ada)
    mods_all = ada[:, :B + 1].reshape(L, B + 1, 6, D_MODEL)
    mods_all = jnp.concatenate([mods_all, jnp.zeros((L, B + 1, 2, D_MODEL), F32)], 2)

    xc, xl, lat_off = ctx, x, -1
    row = lambda a: a[None, :]
    for l in range(L):
        last = l == L - 1
        first_tile = 1 if last else 0
        mods = mods_all[l]
        win_ext, wuq_ext, wuk, wvt, cw, pw, wout, wr, br = _prep_layer(
            w_in[l], w_uq[l], w_ukv[l], conv_w[l], pool_w[l], w_out[l], w_router_group[l],
            b_router_group[l], w_router_expert[l], b_router_expert[l])

        q, k, v, u, poolin = _inproj(xc, xl, lat_off, mods, tabs, win_ext, row(g_q[l]), wuq_ext, row(g_kv[l]),
                                     wuk, wvt)
        attn = _attention(q, k, v)
        attn_ctx = None if last else _attention_ctx(q, k, v)
        x1, xloc, rinfo, ctab, totals = _mixout(
            attn, attn_ctx, u, poolin, xc, xl, lat_off, mods, cw, row(conv_b[l]), row(conv_ln_g[l]),
            row(conv_ln_b[l]), pw, row(pool_scale[l]), wout, row(ln1_g[l]), row(ln1_b[l]), wr, br,
            tri, upper, first_tile)

        n_tok_tiles = B * (N_TILES - first_tile)
        src_tab, dst_tab, tile_expert, n_used_steps = _routing_tables(ctab, totals, n_tok_tiles)
        ys = _experts(tile_expert, n_used_steps, src_tab, xloc, w_gate, w_up, w_down, l)
        xl = _combine(dst_tab, ys, rinfo, x1, mods, row(ln2_g[l]), row(ln2_b[l]), first_tile)
        xc, lat_off = xl, 0
    return xl
```

```python
import functools
import math

import jax
import jax.numpy as jnp
from jax import lax
from jax.experimental import pallas as pl
from jax.experimental.pallas import tpu as pltpu

D_MODEL = 1024
SEQ = 4096
CTX_LEN = 256
SEQ_ALL = CTX_LEN + SEQ
GRID_W = 64
N_HEADS = 8
QK_NOPE = 64
QK_ROPE = 32
QK_HEAD = QK_NOPE + QK_ROPE
V_HEAD = 64
Q_RANK = 384
KV_RANK = 256
D_CONV = 256
CONV_WIDTH = 31
D_POOL = 256
POOL_WINDOWS = (2, 4, 8, 16)
ROPE_THETA = 10000.0
OFF_KV = Q_RANK
OFF_KR = OFF_KV + KV_RANK
OFF_CONV = OFF_KR + QK_ROPE
OFF_POOL = OFF_CONV + 2 * D_CONV
N_GROUPS = 4
EXPERTS_PER_GROUP = 8
N_EXPERTS = 32
D_EXPERT = 256
LN_EPS = 1e-5
RMS_EPS = 1e-6
DEPTH = 2
DEEPNORM_ALPHA = (2 * DEPTH) ** 0.25

LANE = 128
SUBLANE = 8
TILE = 256
HALO = 16
N_TILES = SEQ_ALL // TILE
HEAD_PAD = 128
KEY_CHUNK = 256
VMEM_LIMIT = 56 * 1024 * 1024

CHUNK = 2 * SUBLANE
MAX_USED_CHUNKS = (2 * TILE + (CHUNK - 1) * N_EXPERTS) // CHUNK
LOCAL_ROWS = 1024
LOCAL_CHUNKS = LOCAL_ROWS // CHUNK
assert MAX_USED_CHUNKS < LOCAL_CHUNKS <= LANE
TILE_CHUNKS = TILE // CHUNK
EXPERT_TILES = 2
GATHER_GROUP = 8

XOFF_Q = 0
XOFF_KV = 384
XOFF_KR = 640
XOFF_KRR = 768
XOFF_CONV = 896
XOFF_POOL = 1408
D_IN_EXT = 1664

BF16 = jnp.bfloat16
F32 = jnp.float32
I32 = jnp.int32


def _dot(a, b):
    return jnp.dot(a, b, preferred_element_type=F32)


def _silu(x):
    return x * jax.nn.sigmoid(x)


def _layer_norm_rows(x, g, b):
    mu = jnp.mean(x, axis=-1, keepdims=True)
    xc = x - mu
    var = jnp.mean(xc * xc, axis=-1, keepdims=True)
    return xc * lax.rsqrt(var + LN_EPS) * g + b


def _ada_kernel(cc_ref, w_ref, b_ref, o_ref):
    a = _silu(cc_ref[...])
    o_ref[0] = jnp.dot(a, w_ref[0], preferred_element_type=F32,
                       precision=lax.Precision.HIGHEST) + b_ref[0]


def _ada(cc, w_ada, b_ada):
    L = w_ada.shape[0]
    n = w_ada.shape[2]
    bn = 512
    return pl.pallas_call(
        _ada_kernel,
        out_shape=jax.ShapeDtypeStruct((L, cc.shape[0], n), F32),
        grid=(L, n // bn),
        in_specs=[
            pl.BlockSpec(cc.shape, lambda l, j: (0, 0)),
            pl.BlockSpec((1, D_MODEL, bn), lambda l, j: (l, 0, j)),
            pl.BlockSpec((1, 1, bn), lambda l, j: (l, 0, j)),
        ],
        out_specs=pl.BlockSpec((1, cc.shape[0], bn), lambda l, j: (l, 0, j)),
        compiler_params=pltpu.CompilerParams(
            dimension_semantics=("parallel", "parallel"), vmem_limit_bytes=VMEM_LIMIT),
        name="ada",
    )(cc, w_ada, b_ada.reshape(L, 1, n))


def _inproj_kernel(xc_ref, xl_ref, mod_ref, cq_ref, sq_ref, ck_ref, sk_ref, win_ref, gq_ref, wuq_ref,
                   gkv_ref, wuk_ref, wvt_ref, q_ref, k_ref, vt_ref, u_ref, pool_ref):
    x = jnp.where(pl.program_id(1) == 0, xc_ref[0], xl_ref[0])
    sh = mod_ref[0, 0:1, :]
    sc = mod_ref[0, 1:2, :]
    h = (x * (1.0 + sc) + sh).astype(BF16)
    p = _dot(h, win_ref[...])

    pq = p[:, XOFF_Q:XOFF_KV]
    qn = pq * lax.rsqrt(jnp.mean(pq * pq, axis=-1, keepdims=True) + RMS_EPS) * gq_ref[...]
    q2 = _dot(qn.astype(BF16), wuq_ref[...])
    cq = cq_ref[...]
    sq = sq_ref[...]
    nq = N_HEADS * HEAD_PAD
    for hd in range(N_HEADS):
        lo = hd * HEAD_PAD
        qh = q2[:, lo:lo + HEAD_PAD] * cq + q2[:, nq + lo:nq + lo + HEAD_PAD] * sq
        q_ref[0, :, lo:lo + HEAD_PAD] = qh.astype(BF16)

    pkv = p[:, XOFF_KV:XOFF_KR]
    kvn = pkv * lax.rsqrt(jnp.mean(pkv * pkv, axis=-1, keepdims=True) + RMS_EPS) * gkv_ref[...]
    kvn = kvn.astype(BF16)
    kn = _dot(kvn, wuk_ref[...])
    kr = p[:, XOFF_KR:XOFF_KRR] * ck_ref[...] + p[:, XOFF_KRR:XOFF_CONV] * sk_ref[...]
    for hd in range(N_HEADS):
        lo = hd * HEAD_PAD
        k_ref[0, :, lo:lo + HEAD_PAD] = (kn[:, lo:lo + HEAD_PAD] + kr).astype(BF16)
    vt = lax.dot_general(wvt_ref[...], kvn, (((1,), (1,)), ((), ())), preferred_element_type=F32)
    vrow = lax.broadcasted_iota(I32, (N_HEADS * HEAD_PAD, TILE), 0)
    vt = jnp.where((vrow & (HEAD_PAD - 1)) == V_HEAD, 1.0, vt).astype(BF16)
    for hd in range(N_HEADS):
        vt_ref[0, hd] = vt[hd * HEAD_PAD:(hd + 1) * HEAD_PAD, :]

    a = p[:, XOFF_CONV:XOFF_CONV + D_CONV]
    gt = p[:, XOFF_CONV + D_CONV:XOFF_POOL]
    u_ref[0] = (a * jax.nn.sigmoid(gt)).astype(BF16)
    pool_ref[0] = p[:, XOFF_POOL:]


def _inproj(xc, xl, lat_off, mods, tabs, win_ext, gq, wuq_ext, gkv, wuk, wvt):
    B = xl.shape[0]
    nb = mods.shape[0] - 1

    def mod_idx(b, i):
        return (jnp.where(i == 0, nb, b), 0, 0)

    tab_spec = pl.BlockSpec((TILE, LANE), lambda b, i: (i, 0))
    full = lambda a: pl.BlockSpec(a.shape, lambda b, i: (0,) * a.ndim)
    tok = lambda w: pl.BlockSpec((1, TILE, w), lambda b, i: (b, i, 0))
    hw = N_HEADS * HEAD_PAD
    return pl.pallas_call(
        _inproj_kernel,
        out_shape=(
            jax.ShapeDtypeStruct((B, SEQ_ALL, hw), BF16),
            jax.ShapeDtypeStruct((B, SEQ_ALL, hw), BF16),
            jax.ShapeDtypeStruct((B, N_HEADS, HEAD_PAD, SEQ_ALL), BF16),
            jax.ShapeDtypeStruct((B, SEQ_ALL, D_CONV), BF16),
            jax.ShapeDtypeStruct((B, SEQ_ALL, D_POOL), F32),
        ),
        grid=(B, N_TILES),
        in_specs=[pl.BlockSpec((1, TILE, D_MODEL), lambda b, i: (b, 0, 0)),
                  pl.BlockSpec((1, TILE, D_MODEL), lambda b, i: (b, jnp.maximum(i + lat_off, 0), 0)),
                  pl.BlockSpec((1, 8, D_MODEL), mod_idx),
                  tab_spec, tab_spec, tab_spec, tab_spec,
                  full(win_ext), full(gq), full(wuq_ext), full(gkv), full(wuk), full(wvt)],
        out_specs=(tok(hw), tok(hw),
                   pl.BlockSpec((1, N_HEADS, HEAD_PAD, TILE), lambda b, i: (b, 0, 0, i)),
                   tok(D_CONV), tok(D_POOL)),
        compiler_params=pltpu.CompilerParams(
            dimension_semantics=("parallel", "parallel"), vmem_limit_bytes=VMEM_LIMIT),
        name="inproj",
    )(xc, xl, mods, *tabs, win_ext, gq, wuq_ext, gkv, wuk, wvt)


def _scores_t(q, k_ref, n_keys):
    lane = lax.broadcasted_iota(I32, q.shape, 1)
    qbd = jnp.concatenate([jnp.where(lane < HEAD_PAD, q, jnp.zeros_like(q)),
                           jnp.where(lane >= HEAD_PAD, q, jnp.zeros_like(q))], axis=0)
    return lax.dot_general(k_ref[0, 0:n_keys, :], qbd, (((1,), (1,)), ((), ())),
                           preferred_element_type=F32)


def _store_scores(s_ref, m_ref, st):
    s_ref[...] = st
    m_ref[...] = jnp.broadcast_to(jnp.max(st, axis=0, keepdims=True), m_ref.shape)


def _softmax_values_t(s_ref, m_ref, vt_ref, p_ref, n_keys):
    nq = s_ref.shape[1] // 2
    m = m_ref[0:1, :]
    for kc in range(0, n_keys, KEY_CHUNK):
        p_ref[kc:kc + KEY_CHUNK, :] = jnp.exp2(s_ref[kc:kc + KEY_CHUNK, :] - m).astype(BF16)
    outs = []
    for hd in range(2):
        ot = _dot(vt_ref[0, hd, :, 0:n_keys], p_ref[0:n_keys, hd * nq:(hd + 1) * nq])
        outs.append(ot[0:V_HEAD, :] / ot[V_HEAD:V_HEAD + 1, :])
    return jnp.concatenate(outs, axis=0).T.astype(BF16)


def _attn_kernel(qca_ref, qcb_ref, qna_ref, qnb_ref, k_ref, vt_ref, o_ref,
                 sa_ref, sb_ref, ma_ref, mb_ref, p_ref):
    i = pl.program_id(2)

    @pl.when(i == 0)
    def _():
        qc = jnp.concatenate([qca_ref[0], qcb_ref[0]], axis=0)
        _store_scores(sa_ref, ma_ref, _scores_t(qc, k_ref, SEQ_ALL))

    def step(cur_ref, cur_m_ref, nxt_ref, nxt_m_ref):
        qn = jnp.concatenate([qna_ref[0], qnb_ref[0]], axis=0)
        _store_scores(nxt_ref, nxt_m_ref, _scores_t(qn, k_ref, SEQ_ALL))
        o_ref[0] = _softmax_values_t(cur_ref, cur_m_ref, vt_ref, p_ref, SEQ_ALL)

    parity = lax.rem(i, 2)

    @pl.when(parity == 0)
    def _():
        step(sa_ref, ma_ref, sb_ref, mb_ref)

    @pl.when(parity == 1)
    def _():
        step(sb_ref, mb_ref, sa_ref, ma_ref)


def _attention(q, k, v):
    B = q.shape[0]
    n_pairs = N_HEADS // 2
    pw = 2 * HEAD_PAD
    n_steps = SEQ // (2 * TILE)
    assert n_steps % 2 == 0
    last = SEQ_ALL // TILE - 1
    qspec = lambda f: pl.BlockSpec((1, TILE, pw), f)
    cols = 4 * TILE
    return pl.pallas_call(
        _attn_kernel,
        out_shape=jax.ShapeDtypeStruct((B, SEQ, N_HEADS * V_HEAD), BF16),
        grid=(B, n_pairs, n_steps),
        in_specs=[
            qspec(lambda b, hp, i: (b, 2 * i + 1, hp)),
            qspec(lambda b, hp, i: (b, 2 * i + 2, hp)),
            qspec(lambda b, hp, i: (b, jnp.minimum(2 * i + 3, last), hp)),
            qspec(lambda b, hp, i: (b, jnp.minimum(2 * i + 4, last), hp)),
            pl.BlockSpec((1, SEQ_ALL, pw), lambda b, hp, i: (b, 0, hp)),
            pl.BlockSpec((1, 2, HEAD_PAD, SEQ_ALL), lambda b, hp, i: (b, hp, 0, 0)),
        ],
        out_specs=pl.BlockSpec((1, 2 * TILE, 2 * V_HEAD), lambda b, hp, i: (b, i, hp)),
        scratch_shapes=[pltpu.VMEM((SEQ_ALL, cols), F32), pltpu.VMEM((SEQ_ALL, cols), F32),
                        pltpu.VMEM((SUBLANE, cols), F32), pltpu.VMEM((SUBLANE, cols), F32),
                        pltpu.VMEM((SEQ_ALL, cols), BF16)],
        compiler_params=pltpu.CompilerParams(
            dimension_semantics=("arbitrary", "arbitrary", "arbitrary"),
            vmem_limit_bytes=VMEM_LIMIT),
        name="attn",
    )(q, q, q, q, k, v)


def _attn_ctx_kernel(q_ref, k_ref, vt_ref, o_ref, s_ref, m_ref, p_ref):
    _store_scores(s_ref, m_ref, _scores_t(q_ref[0], k_ref, CTX_LEN))
    o_ref[0] = _softmax_values_t(s_ref, m_ref, vt_ref, p_ref, CTX_LEN)


def _attention_ctx(q, k, v):
    B = q.shape[0]
    pw = 2 * HEAD_PAD
    blk = pl.BlockSpec((1, CTX_LEN, pw), lambda b, hp: (b, 0, hp))
    return pl.pallas_call(
        _attn_ctx_kernel,
        out_shape=jax.ShapeDtypeStruct((B, CTX_LEN, N_HEADS * V_HEAD), BF16),
        grid=(B, N_HEADS // 2),
        in_specs=[blk, blk, pl.BlockSpec((1, 2, HEAD_PAD, CTX_LEN), lambda b, hp: (b, hp, 0, 0))],
        out_specs=pl.BlockSpec((1, CTX_LEN, 2 * V_HEAD), lambda b, hp: (b, 0, hp)),
        scratch_shapes=[pltpu.VMEM((CTX_LEN, 2 * TILE), F32), pltpu.VMEM((SUBLANE, 2 * TILE), F32),
                        pltpu.VMEM((CTX_LEN, 2 * TILE), BF16)],
        compiler_params=pltpu.CompilerParams(
            dimension_semantics=("parallel", "parallel"), vmem_limit_bytes=VMEM_LIMIT),
        name="attn_ctx",
    )(q, k, v)


def _mixout_kernel(*refs, first_tile):
    if first_tile == 0:
        attn_ref, actx_ref = refs[0], refs[1]
        refs = refs[2:]
    else:
        attn_ref, actx_ref = refs[0], None
        refs = refs[1:]
    (u_ref, ul_ref, ur_ref, pm_ref, pl_ref, pr_ref, xc_ref, xl_ref, mod_ref,
     cw_ref, cb_ref, cg_ref, cbeta_ref, pw_ref, ps_ref, wout_ref, g1_ref, b1_ref,
     wr_ref, br_ref, tri_ref, upper_ref,
     x1_ref, xloc_ref, rinfo_ref, ctab_ref, tot_ref,
     winu, winp, shu, shp, cnt) = refs
    i = pl.program_id(1) + first_tile
    left_ok = i >= 2
    right_ok = (i >= 1) & (i <= N_TILES - 2)

    @pl.when((pl.program_id(0) == 0) & (pl.program_id(1) == 0))
    def _():
        cnt[...] = jnp.zeros_like(cnt)

    zu = jnp.zeros((HALO, D_CONV), F32)
    winu[0:HALO, :] = jnp.where(left_ok, ul_ref[0].astype(F32), zu)
    winu[HALO:HALO + TILE, :] = u_ref[0].astype(F32)
    winu[HALO + TILE:, :] = jnp.where(right_ok, ur_ref[0].astype(F32), zu)
    winp[0:HALO, :] = jnp.where(left_ok, pl_ref[0], zu)
    winp[HALO:HALO + TILE, :] = pm_ref[0]
    winp[HALO + TILE:, :] = jnp.where(right_ok, pr_ref[0], zu)
    ph_rows = TILE + 2 * HALO - SUBLANE
    for ph in range(SUBLANE):
        shu[ph] = winu[ph:ph + ph_rows, :]
        shp[ph] = winp[ph:ph + ph_rows, :]

    def shifted(buf, off):
        ph, base = off % SUBLANE, off - off % SUBLANE
        return buf[ph, base:base + TILE, :]

    acc = jnp.zeros((TILE, D_CONV), F32)
    for j in range(CONV_WIDTH):
        acc = acc + shifted(shu, HALO - CONV_WIDTH // 2 + j) * cw_ref[j:j + 1, :]
    conv = _silu(_layer_norm_rows(acc + cb_ref[...], cg_ref[...], cbeta_ref[...]))

    def pw(o):
        return shifted(shp, HALO + o)

    centre = pw(0)
    s2 = pw(-1) + centre
    s4 = s2 + pw(-2) + pw(1)
    s8 = s4 + pw(-4) + pw(-3) + pw(2) + pw(3)
    s16 = s8 + pw(-8) + pw(-7) + pw(-6) + pw(-5) + pw(4) + pw(5) + pw(6) + pw(7)
    row = lax.broadcasted_iota(I32, (TILE, 1), 0)
    seg_len = jnp.where(i == 0, CTX_LEN, SEQ)
    t = jnp.where(i == 0, 0, (i - 1) * TILE) + row

    def mean(sw, w):
        hi = jnp.minimum(t + (w // 2 - 1), seg_len - 1)
        lo = jnp.maximum(t - w // 2, 0)
        return sw / (hi - lo + 1).astype(F32)

    lane_p = lax.broadcasted_iota(I32, (1, D_POOL), 1)
    gdim = D_POOL // len(POOL_WINDOWS)
    mixed = jnp.where(lane_p < gdim, mean(s2, 2),
                      jnp.where(lane_p < 2 * gdim, mean(s4, 4),
                                jnp.where(lane_p < 3 * gdim, mean(s8, 8), mean(s16, 16)))) - centre
    pool = _dot(mixed.astype(BF16), pw_ref[...]) * ps_ref[...]

    d_attn = N_HEADS * V_HEAD
    attn = attn_ref[0]
    if actx_ref is not None:
        attn = jnp.where(i == 0, actx_ref[0], attn)
    mix = (_dot(attn, wout_ref[0:d_attn, :])
           + _dot(conv.astype(BF16), wout_ref[d_attn:d_attn + D_CONV, :])
           + _dot(pool.astype(BF16), wout_ref[d_attn + D_CONV:, :]))
    gate1 = mod_ref[0, 2:3, :]
    x_in = jnp.where(i == 0, xc_ref[0], xl_ref[0])
    x1 = _layer_norm_rows(DEEPNORM_ALPHA * x_in + gate1 * mix, g1_ref[...], b1_ref[...])
    x1_ref[0] = x1

    h2 = (x1 * (1.0 + mod_ref[0, 4:5, :]) + mod_ref[0, 3:4, :]).astype(BF16)
    logits = _dot(h2, wr_ref[...]) + br_ref[...]
    lane = lax.broadcasted_iota(I32, (TILE, LANE), 1)
    neg = jnp.float32(-jnp.inf)
    big = jnp.int32(1 << 20)
    is_g = (lane >= N_EXPERTS) & (lane < N_EXPERTS + N_GROUPS)
    gl = jnp.where(is_g, logits, neg)
    gmax = jnp.max(gl, axis=-1, keepdims=True)
    g_sel = jnp.min(jnp.where(gl == gmax, lane - N_EXPERTS, big), axis=-1, keepdims=True)
    p_sel = 1.0 / jnp.sum(jnp.exp(gl - gmax), axis=-1, keepdims=True)
    in_grp = (lane >= g_sel * EXPERTS_PER_GROUP) & (lane < (g_sel + 1) * EXPERTS_PER_GROUP)
    el = jnp.where(in_grp, logits, neg)
    v1 = jnp.max(el, axis=-1, keepdims=True)
    i1 = jnp.min(jnp.where(el == v1, lane, big), axis=-1, keepdims=True)
    el2 = jnp.where(lane == i1, neg, el)
    v2 = jnp.max(el2, axis=-1, keepdims=True)
    i2 = jnp.min(jnp.where(el2 == v2, lane, big), axis=-1, keepdims=True)
    e2 = jnp.exp(v2 - v1)
    w1 = p_sel / (1.0 + e2)
    w2 = p_sel * e2 / (1.0 + e2)

    oh1 = (lane == i1).astype(F32)
    oh2 = (lane == i2).astype(F32)
    both = oh1 + oh2
    n_e = jnp.sum(both, axis=0, keepdims=True)
    chunks_e = jnp.floor((n_e + (CHUNK - 1)) * (1.0 / CHUNK))
    chunks8 = jnp.broadcast_to(chunks_e, (SUBLANE, LANE))
    start_e = _dot(chunks8.astype(BF16), upper_ref[...])[0:1, :]
    slot_mat = _dot(tri_ref[...], both.astype(BF16)) + CHUNK * start_e
    s1 = jnp.sum(oh1 * slot_mat, axis=-1, keepdims=True)
    s2_ = jnp.sum(oh2 * slot_mat, axis=-1, keepdims=True)
    fields = (w1, w2, s1, s2_)
    rinfo = jnp.zeros((TILE, LANE), F32)
    for n, f in enumerate(fields):
        rinfo = jnp.where(lane == n, f, rinfo)
    rinfo_ref[...] = rinfo

    slot_lane = lax.broadcasted_iota(I32, (TILE, LOCAL_ROWS), 1).astype(F32)
    perm = ((slot_lane == s1).astype(F32) + (slot_lane == s2_).astype(F32)).astype(BF16)
    sorted_rows = lax.dot_general(perm, h2, (((0,), (0,)), ((), ())), preferred_element_type=F32)
    xloc_ref[...] = sorted_rows.astype(BF16)

    done_e = cnt[0:1, :]
    cidx = lax.broadcasted_iota(I32, (LANE, LANE), 0).astype(F32)
    lane_sq = lax.broadcasted_iota(I32, (LANE, LANE), 1)
    owns = ((cidx >= start_e) & (cidx < start_e + chunks_e)).astype(F32)
    e_of_c = jnp.sum(owns * lane_sq.astype(F32), axis=-1, keepdims=True)
    rel_c = jnp.sum(owns * (done_e + cidx - start_e), axis=-1, keepdims=True)
    used_c = jnp.sum(owns, axis=-1, keepdims=True)
    tab = jnp.where(lane_sq == 0, e_of_c, jnp.where(lane_sq == 1, rel_c,
                                                    jnp.where(lane_sq == 2, used_c, 0.0)))
    ctab_ref[0] = tab.T[0:SUBLANE, :]
    cnt[...] = cnt[...] + chunks_e
    tot_ref[...] = cnt[...]


def _mixout(attn, attn_ctx, u, poolin, xc, xl, lat_off, mods, cw, cb, cg, cbeta, pw, ps, wout, g1, b1, wr, br,
            tri, upper, first_tile):
    B = xl.shape[0]
    nt = N_TILES - first_tile
    nb = mods.shape[0] - 1
    hb = TILE // HALO
    n_hb = SEQ_ALL // HALO

    def mod_idx(b, i):
        return (jnp.where(i + first_tile == 0, nb, b), 0, 0)

    tok = lambda w: pl.BlockSpec((1, TILE, w), lambda b, i: (b, i + first_tile, 0))
    own = lambda w: pl.BlockSpec((1, TILE, w), lambda b, i: (b, i, 0))
    left = lambda w: pl.BlockSpec(
        (1, HALO, w), lambda b, i: (b, jnp.maximum((i + first_tile) * hb - 1, 0), 0))
    right = lambda w: pl.BlockSpec(
        (1, HALO, w), lambda b, i: (b, jnp.minimum((i + first_tile + 1) * hb, n_hb - 1), 0))
    full = lambda a: pl.BlockSpec(a.shape, lambda b, i: (0,) * a.ndim)
    d_attn = N_HEADS * V_HEAD
    attn_specs = [pl.BlockSpec((1, TILE, d_attn),
                               lambda b, i: (b, jnp.maximum(i + first_tile - 1, 0), 0))]
    attn_args = [attn]
    if first_tile == 0:
        attn_specs.append(pl.BlockSpec((1, TILE, d_attn), lambda b, i: (b, 0, 0)))
        attn_args.append(attn_ctx)
    n_tok_tiles = B * nt
    ph_rows = TILE + 2 * HALO - SUBLANE
    return pl.pallas_call(
        functools.partial(_mixout_kernel, first_tile=first_tile),
        out_shape=(
            jax.ShapeDtypeStruct((B, nt * TILE, D_MODEL), F32),
            jax.ShapeDtypeStruct((n_tok_tiles * LOCAL_ROWS, D_MODEL), BF16),
            jax.ShapeDtypeStruct((n_tok_tiles * TILE, LANE), F32),
            jax.ShapeDtypeStruct((n_tok_tiles, SUBLANE, LANE), F32),
            jax.ShapeDtypeStruct((SUBLANE, LANE), F32),
        ),
        grid=(B, nt),
        in_specs=attn_specs + [
            tok(D_CONV), left(D_CONV), right(D_CONV),
            tok(D_POOL), left(D_POOL), right(D_POOL),
            pl.BlockSpec((1, TILE, D_MODEL), lambda b, i: (b, 0, 0)),
            pl.BlockSpec((1, TILE, D_MODEL),
                         lambda b, i: (b, jnp.maximum(i + first_tile + lat_off, 0), 0)),
            pl.BlockSpec((1, 8, D_MODEL), mod_idx),
            full(cw), full(cb), full(cg), full(cbeta), full(pw), full(ps), full(wout),
            full(g1), full(b1), full(wr), full(br), full(tri), full(upper)],
        out_specs=(own(D_MODEL),
                   pl.BlockSpec((LOCAL_ROWS, D_MODEL), lambda b, i: (b * nt + i, 0)),
                   pl.BlockSpec((TILE, LANE), lambda b, i: (b * nt + i, 0)),
                   pl.BlockSpec((1, SUBLANE, LANE), lambda b, i: (b * nt + i, 0, 0)),
                   pl.BlockSpec((SUBLANE, LANE), lambda b, i: (0, 0))),
        scratch_shapes=[pltpu.VMEM((TILE + 2 * HALO, D_CONV), F32),
                        pltpu.VMEM((TILE + 2 * HALO, D_POOL), F32),
                        pltpu.VMEM((SUBLANE, ph_rows, D_CONV), F32),
                        pltpu.VMEM((SUBLANE, ph_rows, D_POOL), F32),
                        pltpu.VMEM((SUBLANE, LANE), F32)],
        compiler_params=pltpu.CompilerParams(
            dimension_semantics=("arbitrary", "arbitrary"), vmem_limit_bytes=VMEM_LIMIT),
        name="mixout",
    )(*attn_args, u, u, u, poolin, poolin, poolin, xc, xl, mods, cw, cb, cg, cbeta, pw, ps, wout,
      g1, b1, wr, br, tri, upper)


def _chunk_copy(src, src_chunk, buf, dst_chunk, sem):
    src_row = src_chunk * CHUNK
    if not isinstance(src_row, int):
        src_row = pl.multiple_of(src_row, CHUNK)
    return pltpu.make_async_copy(
        src.at[pl.ds(src_row, CHUNK)], buf.at[pl.ds(dst_chunk * CHUNK, CHUNK)], sem)


def _for_chunk_groups(n_chunks, count, fn):
    for g0 in range(0, n_chunks, GATHER_GROUP):
        def group(g0=g0):
            for c in range(g0, min(g0 + GATHER_GROUP, n_chunks)):
                fn(c)
        if count is None:
            group()
        else:
            pl.when(g0 < count)(group)


def _start_gather(src, table_ref, n_chunks, buf, sem, count=None):
    _for_chunk_groups(n_chunks, count,
                      lambda c: _chunk_copy(src, table_ref[0, 0, c], buf, c, sem).start())


def _wait_gather(src, n_chunks, buf, sem, count=None):
    _for_chunk_groups(n_chunks, count, lambda c: _chunk_copy(src, 0, buf, c, sem).wait())


def _double_buffered_gather(step, n_steps, src, cur_tab, nxt_tab, n_chunks, bufs, sems, compute,
                            counted=False):
    cur_count = cur_tab[0, 0, cur_tab.shape[-1] - 1] if counted else None
    nxt_count = nxt_tab[0, 0, nxt_tab.shape[-1] - 1] if counted else None

    @pl.when(step == 0)
    def _():
        if counted:
            for buf in bufs:
                buf[...] = jnp.zeros_like(buf)
        _start_gather(src, cur_tab, n_chunks, bufs[0], sems.at[0], cur_count)

    for par in range(2):
        @pl.when((lax.rem(step, 2) == par) & (step < n_steps))
        def _(par=par):
            _start_gather(src, nxt_tab, n_chunks, bufs[1 - par], sems.at[1 - par], nxt_count)
            _wait_gather(src, n_chunks, bufs[par], sems.at[par], cur_count)
            compute(bufs[par])

            @pl.when(step == n_steps - 1)
            def _():
                _wait_gather(src, n_chunks, bufs[1 - par], sems.at[1 - par], nxt_count)


def _experts_kernel(te_ref, ns_ref, cur_ref, nxt_ref, xloc_ref, wga_ref, wua_ref, wda_ref,
                    wgb_ref, wub_ref, wdb_ref, ys_ref,
                    xbuf_a, xbuf_b, cga, cua, cda, cgb, cub, cdb, sem):
    i = pl.program_id(0)
    w_refs = ((wga_ref, wua_ref, wda_ref), (wgb_ref, wub_ref, wdb_ref))
    caches = ((cga, cua, cda), (cgb, cub, cdb))

    for t in range(EXPERT_TILES):
        tile = EXPERT_TILES * i + t
        prev = jnp.maximum(tile - EXPERT_TILES, 0)

        @pl.when((i == 0) | (te_ref[tile] != te_ref[prev]))
        def _(t=t):
            for cache, w_ref in zip(caches[t], w_refs[t]):
                cache[...] = w_ref[0, 0].astype(BF16)

    def compute(xbuf):
        for t in range(EXPERT_TILES):
            wg, wu, wd = caches[t]
            xt = xbuf[t * TILE:(t + 1) * TILE, :]
            hid = (_silu(_dot(xt, wg[...])) * _dot(xt, wu[...])).astype(BF16)
            ys_ref[t * TILE:(t + 1) * TILE, :] = _dot(hid, wd[...]).astype(BF16)

    _double_buffered_gather(i, ns_ref[0], xloc_ref, cur_ref, nxt_ref, EXPERT_TILES * TILE_CHUNKS,
                            (xbuf_a, xbuf_b), sem, compute)

    @pl.when(i >= ns_ref[0])
    def _():
        ys_ref[...] = jnp.zeros_like(ys_ref)


def _experts(tile_expert, n_used_steps, src_tab, xloc, w_gate, w_up, w_down, layer):
    n_steps = src_tab.shape[0]
    rows = EXPERT_TILES * TILE

    def w_idx(t):
        return lambda i, te, ns: (layer, te[EXPERT_TILES * i + t], 0, 0)

    tab = lambda f: pl.BlockSpec((1, 1, EXPERT_TILES * TILE_CHUNKS), f, memory_space=pltpu.SMEM)
    w_specs, w_cache = [], []
    for t in range(EXPERT_TILES):
        w_specs += [pl.BlockSpec((1, 1, D_MODEL, D_EXPERT), w_idx(t)),
                    pl.BlockSpec((1, 1, D_MODEL, D_EXPERT), w_idx(t)),
                    pl.BlockSpec((1, 1, D_EXPERT, D_MODEL), w_idx(t))]
        w_cache += [pltpu.VMEM((D_MODEL, D_EXPERT), BF16), pltpu.VMEM((D_MODEL, D_EXPERT), BF16),
                    pltpu.VMEM((D_EXPERT, D_MODEL), BF16)]
    return pl.pallas_call(
        _experts_kernel,
        out_shape=jax.ShapeDtypeStruct((n_steps * rows, D_MODEL), BF16),
        grid_spec=pltpu.PrefetchScalarGridSpec(
            num_scalar_prefetch=2,
            grid=(n_steps,),
            in_specs=[tab(lambda i, te, ns: (i, 0, 0)),
                      tab(lambda i, te, ns: (jnp.minimum(i + 1, n_steps - 1), 0, 0)),
                      pl.BlockSpec(memory_space=pl.ANY)] + w_specs,
            out_specs=pl.BlockSpec((rows, D_MODEL), lambda i, te, ns: (i, 0)),
            scratch_shapes=[pltpu.VMEM((rows, D_MODEL), BF16), pltpu.VMEM((rows, D_MODEL), BF16)]
            + w_cache + [pltpu.SemaphoreType.DMA((2,))],
        ),
        compiler_params=pltpu.CompilerParams(
            dimension_semantics=("arbitrary",), vmem_limit_bytes=VMEM_LIMIT),
        name="experts",
    )(tile_expert, n_used_steps, src_tab, src_tab, xloc, *([w_gate, w_up, w_down] * EXPERT_TILES))


def _combine_kernel(cur_ref, nxt_ref, ys_ref, rinfo_ref, x_ref, mod_ref, g2_ref, b2_ref, o_ref,
                    ybuf_a, ybuf_b, sem, *, n_tok_tiles):
    g = pl.program_id(0) * pl.num_programs(1) + pl.program_id(1)

    def compute(ybuf):
        slot_lane = lax.broadcasted_iota(I32, (TILE, LOCAL_ROWS), 1).astype(F32)
        sel = (jnp.where(slot_lane == rinfo_ref[:, 2:3], rinfo_ref[:, 0:1], 0.0)
               + jnp.where(slot_lane == rinfo_ref[:, 3:4], rinfo_ref[:, 1:2], 0.0))
        y = _dot(sel.astype(BF16), ybuf[...])
        gate2 = mod_ref[0, 5:6, :]
        o_ref[0] = _layer_norm_rows(DEEPNORM_ALPHA * x_ref[0] + gate2 * y, g2_ref[...], b2_ref[...])

    _double_buffered_gather(g, n_tok_tiles, ys_ref, cur_ref, nxt_ref, LOCAL_CHUNKS,
                            (ybuf_a, ybuf_b), sem, compute, counted=True)


def _combine(dst_tab, ys, rinfo, x1, mods, g2, b2, first_tile):
    B = x1.shape[0]
    nt = N_TILES - first_tile
    nb = mods.shape[0] - 1
    n_tok_tiles = B * nt

    def mod_idx(b, i):
        return (jnp.where(i + first_tile == 0, nb, b), 0, 0)

    full = lambda a: pl.BlockSpec(a.shape, lambda b, i: (0,) * a.ndim)
    tab = lambda f: pl.BlockSpec((1, 1, LANE), f, memory_space=pltpu.SMEM)
    return pl.pallas_call(
        functools.partial(_combine_kernel, n_tok_tiles=n_tok_tiles),
        out_shape=jax.ShapeDtypeStruct((B, nt * TILE, D_MODEL), F32),
        grid=(B, nt),
        in_specs=[tab(lambda b, i: (b * nt + i, 0, 0)),
                  tab(lambda b, i: (jnp.minimum(b * nt + i + 1, n_tok_tiles - 1), 0, 0)),
                  pl.BlockSpec(memory_space=pl.ANY),
                  pl.BlockSpec((TILE, LANE), lambda b, i: (b * nt + i, 0)),
                  pl.BlockSpec((1, TILE, D_MODEL), lambda b, i: (b, i, 0)),
                  pl.BlockSpec((1, 8, D_MODEL), mod_idx),
                  full(g2), full(b2)],
        out_specs=pl.BlockSpec((1, TILE, D_MODEL), lambda b, i: (b, i, 0)),
        scratch_shapes=[pltpu.VMEM((LOCAL_ROWS, D_MODEL), BF16), pltpu.VMEM((LOCAL_ROWS, D_MODEL), BF16),
                        pltpu.SemaphoreType.DMA((2,))],
        compiler_params=pltpu.CompilerParams(
            dimension_semantics=("arbitrary", "arbitrary"), vmem_limit_bytes=VMEM_LIMIT),
        name="combine",
    )(dst_tab, dst_tab, ys, rinfo, x1, mods, g2, b2)


def _rot_cols(w):
    w4 = w.reshape(w.shape[:-1] + (2, 2, QK_ROPE // 4))
    return jnp.stack([-w4[..., 1, :], w4[..., 0, :]], axis=-2).reshape(w.shape)


def _rope_tables():
    rows = SEQ // GRID_W
    row = jnp.repeat(jnp.arange(rows), GRID_W)
    col = jnp.tile(jnp.arange(GRID_W), rows)
    d_axis = QK_ROPE // 2
    inv_freq = jnp.power(ROPE_THETA, -jnp.arange(0, d_axis, 2, dtype=F32) / d_axis)

    def axis_angles(p):
        a = p.astype(F32)[:, None] * inv_freq[None, :]
        return jnp.concatenate([a, a], -1)

    ang = jnp.concatenate([axis_angles(row), axis_angles(col)], -1)
    cos = jnp.concatenate([jnp.ones((CTX_LEN, QK_ROPE), F32), jnp.cos(ang)], 0)
    sin = jnp.concatenate([jnp.zeros((CTX_LEN, QK_ROPE), F32), jnp.sin(ang)], 0)
    scale = math.log2(math.e) / math.sqrt(QK_HEAD)
    z64 = jnp.zeros((SEQ_ALL, QK_NOPE), F32)
    z32 = jnp.zeros((SEQ_ALL, HEAD_PAD - QK_HEAD), F32)
    cq = jnp.concatenate([jnp.full((SEQ_ALL, QK_NOPE), scale, F32), scale * cos, z32], -1)
    sq = jnp.concatenate([z64, scale * sin, z32], -1)
    ck = jnp.concatenate([z64, cos, z32], -1)
    sk = jnp.concatenate([z64, sin, z32], -1)
    return cq, sq, ck, sk


def _prep_layer(w_in, w_uq, w_ukv, conv_w, pool_w, w_out, w_rg, b_rg, w_re, b_re):
    zc = lambda r, c: jnp.zeros((r, c), F32)
    kr = w_in[:, OFF_KR:OFF_CONV]
    pad_l, pad_r = QK_NOPE, HEAD_PAD - QK_HEAD
    kr_arr = jnp.concatenate([zc(D_MODEL, pad_l), kr, zc(D_MODEL, pad_r)], -1)
    krr_arr = jnp.concatenate([zc(D_MODEL, pad_l), _rot_cols(kr), zc(D_MODEL, pad_r)], -1)
    win_ext = jnp.concatenate(
        [w_in[:, :OFF_KR], kr_arr, krr_arr, w_in[:, OFF_CONV:]], -1).astype(BF16)

    wq3 = w_uq.reshape(Q_RANK, N_HEADS, QK_HEAD)
    nope, rope = wq3[..., :QK_NOPE], wq3[..., QK_NOPE:]
    z3 = lambda c: jnp.zeros((Q_RANK, N_HEADS, c), F32)
    q_arr = jnp.concatenate([nope, rope, z3(pad_r)], -1).reshape(Q_RANK, N_HEADS * HEAD_PAD)
    q_rot = jnp.concatenate([z3(pad_l), _rot_cols(rope), z3(pad_r)], -1).reshape(Q_RANK, N_HEADS * HEAD_PAD)
    wuq_ext = jnp.concatenate([q_arr, q_rot], -1).astype(BF16)

    wkv3 = w_ukv.reshape(KV_RANK, N_HEADS, QK_NOPE + V_HEAD)
    zkv = jnp.zeros((KV_RANK, N_HEADS, HEAD_PAD - QK_NOPE), F32)
    k_arr = jnp.concatenate([wkv3[..., :QK_NOPE], zkv], -1).reshape(KV_RANK, N_HEADS * HEAD_PAD)
    v_arr = jnp.concatenate([wkv3[..., QK_NOPE:], zkv], -1).reshape(KV_RANK, N_HEADS * HEAD_PAD)
    wuk, wvt = k_arr.astype(BF16), v_arr.T.astype(BF16)

    cw = jnp.concatenate([conv_w, jnp.zeros((1, D_CONV), F32)], 0)
    gdim = D_POOL // len(POOL_WINDOWS)
    pw = jnp.zeros((D_POOL, D_POOL), F32)
    for g in range(len(POOL_WINDOWS)):
        pw = pw.at[g * gdim:(g + 1) * gdim, g * gdim:(g + 1) * gdim].set(pool_w[g])
    wr = jnp.concatenate([w_re, w_rg, zc(D_MODEL, LANE - N_EXPERTS - N_GROUPS)], -1).astype(BF16)
    br = jnp.concatenate([b_re, b_rg, jnp.zeros((LANE - N_EXPERTS - N_GROUPS,), F32)])[None, :]
    return win_ext, wuq_ext, wuk, wvt, cw, pw.astype(BF16), w_out.astype(BF16), wr, br


def _routing_tables(ctab, totals, n_tok_tiles):
    n_tiles_max = -(-(MAX_USED_CHUNKS * n_tok_tiles + (TILE_CHUNKS - 1) * N_EXPERTS) // TILE_CHUNKS)
    n_tiles_max = -(-n_tiles_max // EXPERT_TILES) * EXPERT_TILES
    n_chunks_max = n_tiles_max * TILE_CHUNKS
    chunks = totals[0, :N_EXPERTS].astype(I32)
    padded = ((chunks + TILE_CHUNKS - 1) // TILE_CHUNKS) * TILE_CHUNKS
    ends = jnp.cumsum(padded)
    starts = ends - padded
    e_of_c = ctab[:, 0, :LOCAL_CHUNKS].astype(I32)
    rel_c = ctab[:, 1, :LOCAL_CHUNKS].astype(I32)
    used = ctab[:, 2, :LOCAL_CHUNKS] > 0.5
    owner = e_of_c[..., None] == jnp.arange(N_EXPERTS, dtype=I32)
    dst = jnp.sum(jnp.where(owner, starts, 0), axis=-1) + rel_c
    local_id = (jnp.arange(n_tok_tiles, dtype=I32)[:, None] * LOCAL_CHUNKS
                + jnp.arange(LOCAL_CHUNKS, dtype=I32)[None, :])
    src = jnp.full((n_chunks_max,), LOCAL_CHUNKS - 1, I32)
    src = src.at[jnp.where(used, dst, n_chunks_max).reshape(-1)].set(local_id.reshape(-1), mode="drop")
    src_tab = src.reshape(n_tiles_max // EXPERT_TILES, 1, EXPERT_TILES * TILE_CHUNKS)
    dst_tab = jnp.where(used, dst, dst[:, 0:1])
    n_used = jnp.sum(used.astype(I32), axis=-1, keepdims=True)
    dst_tab = jnp.concatenate(
        [dst_tab, jnp.zeros((n_tok_tiles, LANE - LOCAL_CHUNKS - 1), I32), n_used],
        -1).reshape(n_tok_tiles, 1, LANE)
    n_valid = ends[-1] // TILE_CHUNKS
    tile_start = jnp.minimum(jnp.arange(n_tiles_max, dtype=I32), n_valid - 1) * TILE_CHUNKS
    tile_expert = jnp.sum((tile_start[:, None] >= ends[None, :]).astype(I32), axis=-1)
    n_used_steps = (n_valid + EXPERT_TILES - 1) // EXPERT_TILES
    return src_tab, dst_tab, tile_expert.astype(I32), n_used_steps.astype(I32).reshape(1)


def kernel(x, c, ctx, c_ctx, w_ada, b_ada, w_in, g_q, w_uq, g_kv, w_ukv, conv_w, conv_b, conv_ln_g,
           conv_ln_b, pool_w, pool_scale, w_out, ln1_g, ln1_b, w_router_group, b_router_group,
           w_router_expert, b_router_expert, w_gate, w_up, w_down, ln2_g, ln2_b):
    B = x.shape[0]
    L = w_ada.shape[0]
    tabs = _rope_tables()
    tri = jnp.tril(jnp.ones((TILE, TILE), F32), -1).astype(BF16)
    upper = jnp.triu(jnp.ones((LANE, LANE), F32), 1).astype(BF16)

    cc = jnp.concatenate([c, c_ctx[None, :], jnp.zeros((16 - B - 1, D_MODEL), F32)], 0)
    ada = _ada(cc, w_ada, b_ada)
    mods_all = ada[:, :B + 1].reshape(L, B + 1, 6, D_MODEL)
    mods_all = jnp.concatenate([mods_all, jnp.zeros((L, B + 1, 2, D_MODEL), F32)], 2)

    xc, xl, lat_off = ctx, x, -1
    row = lambda a: a[None, :]
    for l in range(L):
        last = l == L - 1
        first_tile = 1 if last else 0
        mods = mods_all[l]
        win_ext, wuq_ext, wuk, wvt, cw, pw, wout, wr, br = _prep_layer(
            w_in[l], w_uq[l], w_ukv[l], conv_w[l], pool_w[l], w_out[l], w_router_group[l],
            b_router_group[l], w_router_expert[l], b_router_expert[l])

        q, k, v, u, poolin = _inproj(xc, xl, lat_off, mods, tabs, win_ext, row(g_q[l]), wuq_ext, row(g_kv[l]),
                                     wuk, wvt)
        attn = _attention(q, k, v)
        attn_ctx = None if last else _attention_ctx(q, k, v)
        x1, xloc, rinfo, ctab, totals = _mixout(
            attn, attn_ctx, u, poolin, xc, xl, lat_off, mods, cw, row(conv_b[l]), row(conv_ln_g[l]),
            row(conv_ln_b[l]), pw, row(pool_scale[l]), wout, row(ln1_g[l]), row(ln1_b[l]), wr, br,
            tri, upper, first_tile)

        n_tok_tiles = B * (N_TILES - first_tile)
        src_tab, dst_tab, tile_expert, n_used_steps = _routing_tables(ctab, totals, n_tok_tiles)
        ys = _experts(tile_expert, n_used_steps, src_tab, xloc, w_gate, w_up, w_down, l)
        xl = _combine(dst_tab, ys, rinfo, x1, mods, row(ln2_g[l]), row(ln2_b[l]), first_tile)
        xc, lat_off = xl, 0
    return xl
```

```python
import functools
import math

import jax
import jax.numpy as jnp
from jax import lax
from jax.experimental import pallas as pl
from jax.experimental.pallas import tpu as pltpu

D_MODEL = 1024
SEQ = 4096
CTX_LEN = 256
SEQ_ALL = CTX_LEN + SEQ
GRID_W = 64
N_HEADS = 8
QK_NOPE = 64
QK_ROPE = 32
QK_HEAD = QK_NOPE + QK_ROPE
V_HEAD = 64
Q_RANK = 384
KV_RANK = 256
D_CONV = 256
CONV_WIDTH = 31
D_POOL = 256
POOL_WINDOWS = (2, 4, 8, 16)
ROPE_THETA = 10000.0
OFF_KV = Q_RANK
OFF_KR = OFF_KV + KV_RANK
OFF_CONV = OFF_KR + QK_ROPE
OFF_POOL = OFF_CONV + 2 * D_CONV
N_GROUPS = 4
EXPERTS_PER_GROUP = 8
N_EXPERTS = 32
D_EXPERT = 256
LN_EPS = 1e-5
RMS_EPS = 1e-6
DEPTH = 2
DEEPNORM_ALPHA = (2 * DEPTH) ** 0.25

LANE = 128
SUBLANE = 8
TILE = 256
HALO = 16
N_TILES = SEQ_ALL // TILE
HEAD_PAD = 128
KEY_CHUNK = 256
VMEM_LIMIT = 56 * 1024 * 1024

CHUNK = 2 * SUBLANE
MAX_USED_CHUNKS = (2 * TILE + (CHUNK - 1) * N_EXPERTS) // CHUNK
LOCAL_ROWS = 1024
LOCAL_CHUNKS = LOCAL_ROWS // CHUNK
assert MAX_USED_CHUNKS < LOCAL_CHUNKS <= LANE
TILE_CHUNKS = TILE // CHUNK
EXPERT_TILES = 2
GATHER_GROUP = 4

XOFF_Q = 0
XOFF_KV = 384
XOFF_KR = 640
XOFF_KRR = 768
XOFF_CONV = 896
XOFF_POOL = 1408
D_IN_EXT = 1664

BF16 = jnp.bfloat16
F32 = jnp.float32
I32 = jnp.int32


def _dot(a, b):
    return jnp.dot(a, b, preferred_element_type=F32)


def _silu(x):
    return x * jax.nn.sigmoid(x)


def _layer_norm_rows(x, g, b):
    mu = jnp.mean(x, axis=-1, keepdims=True)
    xc = x - mu
    var = jnp.mean(xc * xc, axis=-1, keepdims=True)
    return xc * lax.rsqrt(var + LN_EPS) * g + b


def _ada_kernel(cc_ref, w_ref, b_ref, o_ref):
    a = _silu(cc_ref[...])
    o_ref[0] = jnp.dot(a, w_ref[0], preferred_element_type=F32,
                       precision=lax.Precision.HIGHEST) + b_ref[0]


def _ada(cc, w_ada, b_ada):
    L = w_ada.shape[0]
    n = w_ada.shape[2]
    bn = 512
    return pl.pallas_call(
        _ada_kernel,
        out_shape=jax.ShapeDtypeStruct((L, cc.shape[0], n), F32),
        grid=(L, n // bn),
        in_specs=[
            pl.BlockSpec(cc.shape, lambda l, j: (0, 0)),
            pl.BlockSpec((1, D_MODEL, bn), lambda l, j: (l, 0, j)),
            pl.BlockSpec((1, 1, bn), lambda l, j: (l, 0, j)),
        ],
        out_specs=pl.BlockSpec((1, cc.shape[0], bn), lambda l, j: (l, 0, j)),
        compiler_params=pltpu.CompilerParams(
            dimension_semantics=("parallel", "parallel"), vmem_limit_bytes=VMEM_LIMIT),
        name="ada",
    )(cc, w_ada, b_ada.reshape(L, 1, n))


def _inproj_kernel(xc_ref, xl_ref, mod_ref, cq_ref, sq_ref, ck_ref, sk_ref, win_ref, gq_ref, wuq_ref,
                   gkv_ref, wuk_ref, wvt_ref, q_ref, k_ref, vt_ref, u_ref, pool_ref):
    x = jnp.where(pl.program_id(1) == 0, xc_ref[0], xl_ref[0])
    sh = mod_ref[0, 0:1, :]
    sc = mod_ref[0, 1:2, :]
    h = (x * (1.0 + sc) + sh).astype(BF16)
    p = _dot(h, win_ref[...])

    pq = p[:, XOFF_Q:XOFF_KV]
    qn = pq * lax.rsqrt(jnp.mean(pq * pq, axis=-1, keepdims=True) + RMS_EPS) * gq_ref[...]
    q2 = _dot(qn.astype(BF16), wuq_ref[...])
    cq = cq_ref[...]
    sq = sq_ref[...]
    nq = N_HEADS * HEAD_PAD
    for hd in range(N_HEADS):
        lo = hd * HEAD_PAD
        qh = q2[:, lo:lo + HEAD_PAD] * cq + q2[:, nq + lo:nq + lo + HEAD_PAD] * sq
        q_ref[0, :, lo:lo + HEAD_PAD] = qh.astype(BF16)

    pkv = p[:, XOFF_KV:XOFF_KR]
    kvn = pkv * lax.rsqrt(jnp.mean(pkv * pkv, axis=-1, keepdims=True) + RMS_EPS) * gkv_ref[...]
    kvn = kvn.astype(BF16)
    kn = _dot(kvn, wuk_ref[...])
    kr = p[:, XOFF_KR:XOFF_KRR] * ck_ref[...] + p[:, XOFF_KRR:XOFF_CONV] * sk_ref[...]
    for hd in range(N_HEADS):
        lo = hd * HEAD_PAD
        k_ref[0, :, lo:lo + HEAD_PAD] = (kn[:, lo:lo + HEAD_PAD] + kr).astype(BF16)
    vt = lax.dot_general(wvt_ref[...], kvn, (((1,), (1,)), ((), ())), preferred_element_type=F32)
    vrow = lax.broadcasted_iota(I32, (N_HEADS * HEAD_PAD, TILE), 0)
    vt = jnp.where((vrow & (HEAD_PAD - 1)) == V_HEAD, 1.0, vt).astype(BF16)
    for hd in range(N_HEADS):
        vt_ref[0, hd] = vt[hd * HEAD_PAD:(hd + 1) * HEAD_PAD, :]

    a = p[:, XOFF_CONV:XOFF_CONV + D_CONV]
    gt = p[:, XOFF_CONV + D_CONV:XOFF_POOL]
    u_ref[0] = (a * jax.nn.sigmoid(gt)).astype(BF16)
    pool_ref[0] = p[:, XOFF_POOL:]


def _inproj(xc, xl, lat_off, mods, tabs, win_ext, gq, wuq_ext, gkv, wuk, wvt):
    B = xl.shape[0]
    nb = mods.shape[0] - 1

    def mod_idx(b, i):
        return (jnp.where(i == 0, nb, b), 0, 0)

    tab_spec = pl.BlockSpec((TILE, LANE), lambda b, i: (i, 0))
    full = lambda a: pl.BlockSpec(a.shape, lambda b, i: (0,) * a.ndim)
    tok = lambda w: pl.BlockSpec((1, TILE, w), lambda b, i: (b, i, 0))
    hw = N_HEADS * HEAD_PAD
    return pl.pallas_call(
        _inproj_kernel,
        out_shape=(
            jax.ShapeDtypeStruct((B, SEQ_ALL, hw), BF16),
            jax.ShapeDtypeStruct((B, SEQ_ALL, hw), BF16),
            jax.ShapeDtypeStruct((B, N_HEADS, HEAD_PAD, SEQ_ALL), BF16),
            jax.ShapeDtypeStruct((B, SEQ_ALL, D_CONV), BF16),
            jax.ShapeDtypeStruct((B, SEQ_ALL, D_POOL), F32),
        ),
        grid=(B, N_TILES),
        in_specs=[pl.BlockSpec((1, TILE, D_MODEL), lambda b, i: (b, 0, 0)),
                  pl.BlockSpec((1, TILE, D_MODEL), lambda b, i: (b, jnp.maximum(i + lat_off, 0), 0)),
                  pl.BlockSpec((1, 8, D_MODEL), mod_idx),
                  tab_spec, tab_spec, tab_spec, tab_spec,
                  full(win_ext), full(gq), full(wuq_ext), full(gkv), full(wuk), full(wvt)],
        out_specs=(tok(hw), tok(hw),
                   pl.BlockSpec((1, N_HEADS, HEAD_PAD, TILE), lambda b, i: (b, 0, 0, i)),
                   tok(D_CONV), tok(D_POOL)),
        compiler_params=pltpu.CompilerParams(
            dimension_semantics=("parallel", "parallel"), vmem_limit_bytes=VMEM_LIMIT),
        name="inproj",
    )(xc, xl, mods, *tabs, win_ext, gq, wuq_ext, gkv, wuk, wvt)


def _scores_t(q, k_ref, n_keys):
    lane = lax.broadcasted_iota(I32, q.shape, 1)
    qbd = jnp.concatenate([jnp.where(lane < HEAD_PAD, q, jnp.zeros_like(q)),
                           jnp.where(lane >= HEAD_PAD, q, jnp.zeros_like(q))], axis=0)
    return lax.dot_general(k_ref[0, 0:n_keys, :], qbd, (((1,), (1,)), ((), ())),
                           preferred_element_type=F32)


def _store_scores(s_ref, m_ref, st):
    s_ref[...] = st
    m_ref[...] = jnp.broadcast_to(jnp.max(st, axis=0, keepdims=True), m_ref.shape)


def _softmax_values_t(s_ref, m_ref, vt_ref, p_ref, n_keys):
    nq = s_ref.shape[1] // 2
    m = m_ref[0:1, :]
    for kc in range(0, n_keys, KEY_CHUNK):
        p_ref[kc:kc + KEY_CHUNK, :] = jnp.exp2(s_ref[kc:kc + KEY_CHUNK, :] - m).astype(BF16)
    outs = []
    for hd in range(2):
        ot = _dot(vt_ref[0, hd, :, 0:n_keys], p_ref[0:n_keys, hd * nq:(hd + 1) * nq])
        outs.append(ot[0:V_HEAD, :] / ot[V_HEAD:V_HEAD + 1, :])
    return jnp.concatenate(outs, axis=0).T.astype(BF16)


def _attn_kernel(qca_ref, qcb_ref, qna_ref, qnb_ref, k_ref, vt_ref, o_ref,
                 sa_ref, sb_ref, ma_ref, mb_ref, p_ref):
    i = pl.program_id(2)

    @pl.when(i == 0)
    def _():
        qc = jnp.concatenate([qca_ref[0], qcb_ref[0]], axis=0)
        _store_scores(sa_ref, ma_ref, _scores_t(qc, k_ref, SEQ_ALL))

    def step(cur_ref, cur_m_ref, nxt_ref, nxt_m_ref):
        qn = jnp.concatenate([qna_ref[0], qnb_ref[0]], axis=0)
        _store_scores(nxt_ref, nxt_m_ref, _scores_t(qn, k_ref, SEQ_ALL))
        o_ref[0] = _softmax_values_t(cur_ref, cur_m_ref, vt_ref, p_ref, SEQ_ALL)

    parity = lax.rem(i, 2)

    @pl.when(parity == 0)
    def _():
        step(sa_ref, ma_ref, sb_ref, mb_ref)

    @pl.when(parity == 1)
    def _():
        step(sb_ref, mb_ref, sa_ref, ma_ref)


def _attention(q, k, v):
    B = q.shape[0]
    n_pairs = N_HEADS // 2
    pw = 2 * HEAD_PAD
    n_steps = SEQ // (2 * TILE)
    assert n_steps % 2 == 0
    last = SEQ_ALL // TILE - 1
    qspec = lambda f: pl.BlockSpec((1, TILE, pw), f)
    cols = 4 * TILE
    return pl.pallas_call(
        _attn_kernel,
        out_shape=jax.ShapeDtypeStruct((B, SEQ, N_HEADS * V_HEAD), BF16),
        grid=(B, n_pairs, n_steps),
        in_specs=[
            qspec(lambda b, hp, i: (b, 2 * i + 1, hp)),
            qspec(lambda b, hp, i: (b, 2 * i + 2, hp)),
            qspec(lambda b, hp, i: (b, jnp.minimum(2 * i + 3, last), hp)),
            qspec(lambda b, hp, i: (b, jnp.minimum(2 * i + 4, last), hp)),
            pl.BlockSpec((1, SEQ_ALL, pw), lambda b, hp, i: (b, 0, hp)),
            pl.BlockSpec((1, 2, HEAD_PAD, SEQ_ALL), lambda b, hp, i: (b, hp, 0, 0)),
        ],
        out_specs=pl.BlockSpec((1, 2 * TILE, 2 * V_HEAD), lambda b, hp, i: (b, i, hp)),
        scratch_shapes=[pltpu.VMEM((SEQ_ALL, cols), F32), pltpu.VMEM((SEQ_ALL, cols), F32),
                        pltpu.VMEM((SUBLANE, cols), F32), pltpu.VMEM((SUBLANE, cols), F32),
                        pltpu.VMEM((SEQ_ALL, cols), BF16)],
        compiler_params=pltpu.CompilerParams(
            dimension_semantics=("arbitrary", "arbitrary", "arbitrary"),
            vmem_limit_bytes=VMEM_LIMIT),
        name="attn",
    )(q, q, q, q, k, v)


def _attn_ctx_kernel(q_ref, k_ref, vt_ref, o_ref, s_ref, m_ref, p_ref):
    _store_scores(s_ref, m_ref, _scores_t(q_ref[0], k_ref, CTX_LEN))
    o_ref[0] = _softmax_values_t(s_ref, m_ref, vt_ref, p_ref, CTX_LEN)


def _attention_ctx(q, k, v):
    B = q.shape[0]
    pw = 2 * HEAD_PAD
    blk = pl.BlockSpec((1, CTX_LEN, pw), lambda b, hp: (b, 0, hp))
    return pl.pallas_call(
        _attn_ctx_kernel,
        out_shape=jax.ShapeDtypeStruct((B, CTX_LEN, N_HEADS * V_HEAD), BF16),
        grid=(B, N_HEADS // 2),
        in_specs=[blk, blk, pl.BlockSpec((1, 2, HEAD_PAD, CTX_LEN), lambda b, hp: (b, hp, 0, 0))],
        out_specs=pl.BlockSpec((1, CTX_LEN, 2 * V_HEAD), lambda b, hp: (b, 0, hp)),
        scratch_shapes=[pltpu.VMEM((CTX_LEN, 2 * TILE), F32), pltpu.VMEM((SUBLANE, 2 * TILE), F32),
                        pltpu.VMEM((CTX_LEN, 2 * TILE), BF16)],
        compiler_params=pltpu.CompilerParams(
            dimension_semantics=("parallel", "parallel"), vmem_limit_bytes=VMEM_LIMIT),
        name="attn_ctx",
    )(q, k, v)


def _mixout_kernel(*refs, first_tile):
    if first_tile == 0:
        attn_ref, actx_ref = refs[0], refs[1]
        refs = refs[2:]
    else:
        attn_ref, actx_ref = refs[0], None
        refs = refs[1:]
    (u_ref, ul_ref, ur_ref, pm_ref, pl_ref, pr_ref, xc_ref, xl_ref, mod_ref,
     cw_ref, cb_ref, cg_ref, cbeta_ref, pw_ref, ps_ref, wout_ref, g1_ref, b1_ref,
     wr_ref, br_ref, tri_ref, upper_ref,
     x1_ref, xloc_ref, rinfo_ref, ctab_ref, tot_ref,
     winu, winp, shu, shp, cnt) = refs
    i = pl.program_id(1) + first_tile
    left_ok = i >= 2
    right_ok = (i >= 1) & (i <= N_TILES - 2)

    @pl.when((pl.program_id(0) == 0) & (pl.program_id(1) == 0))
    def _():
        cnt[...] = jnp.zeros_like(cnt)

    zu = jnp.zeros((HALO, D_CONV), F32)
    winu[0:HALO, :] = jnp.where(left_ok, ul_ref[0].astype(F32), zu)
    winu[HALO:HALO + TILE, :] = u_ref[0].astype(F32)
    winu[HALO + TILE:, :] = jnp.where(right_ok, ur_ref[0].astype(F32), zu)
    winp[0:HALO, :] = jnp.where(left_ok, pl_ref[0], zu)
    winp[HALO:HALO + TILE, :] = pm_ref[0]
    winp[HALO + TILE:, :] = jnp.where(right_ok, pr_ref[0], zu)
    ph_rows = TILE + 2 * HALO - SUBLANE
    for ph in range(SUBLANE):
        shu[ph] = winu[ph:ph + ph_rows, :]
        shp[ph] = winp[ph:ph + ph_rows, :]

    def shifted(buf, off):
        ph, base = off % SUBLANE, off - off % SUBLANE
        return buf[ph, base:base + TILE, :]

    acc = jnp.zeros((TILE, D_CONV), F32)
    for j in range(CONV_WIDTH):
        acc = acc + shifted(shu, HALO - CONV_WIDTH // 2 + j) * cw_ref[j:j + 1, :]
    conv = _silu(_layer_norm_rows(acc + cb_ref[...], cg_ref[...], cbeta_ref[...]))

    def pw(o):
        return shifted(shp, HALO + o)

    centre = pw(0)
    s2 = pw(-1) + centre
    s4 = s2 + pw(-2) + pw(1)
    s8 = s4 + pw(-4) + pw(-3) + pw(2) + pw(3)
    s16 = s8 + pw(-8) + pw(-7) + pw(-6) + pw(-5) + pw(4) + pw(5) + pw(6) + pw(7)
    row = lax.broadcasted_iota(I32, (TILE, 1), 0)
    seg_len = jnp.where(i == 0, CTX_LEN, SEQ)
    t = jnp.where(i == 0, 0, (i - 1) * TILE) + row

    def mean(sw, w):
        hi = jnp.minimum(t + (w // 2 - 1), seg_len - 1)
        lo = jnp.maximum(t - w // 2, 0)
        return sw / (hi - lo + 1).astype(F32)

    lane_p = lax.broadcasted_iota(I32, (1, D_POOL), 1)
    gdim = D_POOL // len(POOL_WINDOWS)
    mixed = jnp.where(lane_p < gdim, mean(s2, 2),
                      jnp.where(lane_p < 2 * gdim, mean(s4, 4),
                                jnp.where(lane_p < 3 * gdim, mean(s8, 8), mean(s16, 16)))) - centre
    pool = _dot(mixed.astype(BF16), pw_ref[...]) * ps_ref[...]

    d_attn = N_HEADS * V_HEAD
    attn = attn_ref[0]
    if actx_ref is not None:
        attn = jnp.where(i == 0, actx_ref[0], attn)
    mix = (_dot(attn, wout_ref[0:d_attn, :])
           + _dot(conv.astype(BF16), wout_ref[d_attn:d_attn + D_CONV, :])
           + _dot(pool.astype(BF16), wout_ref[d_attn + D_CONV:, :]))
    gate1 = mod_ref[0, 2:3, :]
    x_in = jnp.where(i == 0, xc_ref[0], xl_ref[0])
    x1 = _layer_norm_rows(DEEPNORM_ALPHA * x_in + gate1 * mix, g1_ref[...], b1_ref[...])
    x1_ref[0] = x1

    h2 = (x1 * (1.0 + mod_ref[0, 4:5, :]) + mod_ref[0, 3:4, :]).astype(BF16)
    logits = _dot(h2, wr_ref[...]) + br_ref[...]
    lane = lax.broadcasted_iota(I32, (TILE, LANE), 1)
    neg = jnp.float32(-jnp.inf)
    big = jnp.int32(1 << 20)
    is_g = (lane >= N_EXPERTS) & (lane < N_EXPERTS + N_GROUPS)
    gl = jnp.where(is_g, logits, neg)
    gmax = jnp.max(gl, axis=-1, keepdims=True)
    g_sel = jnp.min(jnp.where(gl == gmax, lane - N_EXPERTS, big), axis=-1, keepdims=True)
    p_sel = 1.0 / jnp.sum(jnp.exp(gl - gmax), axis=-1, keepdims=True)
    in_grp = (lane >= g_sel * EXPERTS_PER_GROUP) & (lane < (g_sel + 1) * EXPERTS_PER_GROUP)
    el = jnp.where(in_grp, logits, neg)
    v1 = jnp.max(el, axis=-1, keepdims=True)
    i1 = jnp.min(jnp.where(el == v1, lane, big), axis=-1, keepdims=True)
    el2 = jnp.where(lane == i1, neg, el)
    v2 = jnp.max(el2, axis=-1, keepdims=True)
    i2 = jnp.min(jnp.where(el2 == v2, lane, big), axis=-1, keepdims=True)
    e2 = jnp.exp(v2 - v1)
    w1 = p_sel / (1.0 + e2)
    w2 = p_sel * e2 / (1.0 + e2)

    oh1 = (lane == i1).astype(F32)
    oh2 = (lane == i2).astype(F32)
    both = oh1 + oh2
    n_e = jnp.sum(both, axis=0, keepdims=True)
    chunks_e = jnp.floor((n_e + (CHUNK - 1)) * (1.0 / CHUNK))
    chunks8 = jnp.broadcast_to(chunks_e, (SUBLANE, LANE))
    start_e = _dot(chunks8.astype(BF16), upper_ref[...])[0:1, :]
    slot_mat = _dot(tri_ref[...], both.astype(BF16)) + CHUNK * start_e
    s1 = jnp.sum(oh1 * slot_mat, axis=-1, keepdims=True)
    s2_ = jnp.sum(oh2 * slot_mat, axis=-1, keepdims=True)
    fields = (w1, w2, s1, s2_)
    rinfo = jnp.zeros((TILE, LANE), F32)
    for n, f in enumerate(fields):
        rinfo = jnp.where(lane == n, f, rinfo)
    rinfo_ref[...] = rinfo

    slot_lane = lax.broadcasted_iota(I32, (TILE, LOCAL_ROWS), 1).astype(F32)
    perm = ((slot_lane == s1).astype(F32) + (slot_lane == s2_).astype(F32)).astype(BF16)
    sorted_rows = lax.dot_general(perm, h2, (((0,), (0,)), ((), ())), preferred_element_type=F32)
    xloc_ref[...] = sorted_rows.astype(BF16)

    done_e = cnt[0:1, :]
    cidx = lax.broadcasted_iota(I32, (LANE, LANE), 0).astype(F32)
    lane_sq = lax.broadcasted_iota(I32, (LANE, LANE), 1)
    owns = ((cidx >= start_e) & (cidx < start_e + chunks_e)).astype(F32)
    e_of_c = jnp.sum(owns * lane_sq.astype(F32), axis=-1, keepdims=True)
    rel_c = jnp.sum(owns * (done_e + cidx - start_e), axis=-1, keepdims=True)
    used_c = jnp.sum(owns, axis=-1, keepdims=True)
    tab = jnp.where(lane_sq == 0, e_of_c, jnp.where(lane_sq == 1, rel_c,
                                                    jnp.where(lane_sq == 2, used_c, 0.0)))
    ctab_ref[0] = tab.T[0:SUBLANE, :]
    cnt[...] = cnt[...] + chunks_e
    tot_ref[...] = cnt[...]


def _mixout(attn, attn_ctx, u, poolin, xc, xl, lat_off, mods, cw, cb, cg, cbeta, pw, ps, wout, g1, b1, wr, br,
            tri, upper, first_tile):
    B = xl.shape[0]
    nt = N_TILES - first_tile
    nb = mods.shape[0] - 1
    hb = TILE // HALO
    n_hb = SEQ_ALL // HALO

    def mod_idx(b, i):
        return (jnp.where(i + first_tile == 0, nb, b), 0, 0)

    tok = lambda w: pl.BlockSpec((1, TILE, w), lambda b, i: (b, i + first_tile, 0))
    own = lambda w: pl.BlockSpec((1, TILE, w), lambda b, i: (b, i, 0))
    left = lambda w: pl.BlockSpec(
        (1, HALO, w), lambda b, i: (b, jnp.maximum((i + first_tile) * hb - 1, 0), 0))
    right = lambda w: pl.BlockSpec(
        (1, HALO, w), lambda b, i: (b, jnp.minimum((i + first_tile + 1) * hb, n_hb - 1), 0))
    full = lambda a: pl.BlockSpec(a.shape, lambda b, i: (0,) * a.ndim)
    d_attn = N_HEADS * V_HEAD
    attn_specs = [pl.BlockSpec((1, TILE, d_attn),
                               lambda b, i: (b, jnp.maximum(i + first_tile - 1, 0), 0))]
    attn_args = [attn]
    if first_tile == 0:
        attn_specs.append(pl.BlockSpec((1, TILE, d_attn), lambda b, i: (b, 0, 0)))
        attn_args.append(attn_ctx)
    n_tok_tiles = B * nt
    ph_rows = TILE + 2 * HALO - SUBLANE
    return pl.pallas_call(
        functools.partial(_mixout_kernel, first_tile=first_tile),
        out_shape=(
            jax.ShapeDtypeStruct((B, nt * TILE, D_MODEL), F32),
            jax.ShapeDtypeStruct((n_tok_tiles * LOCAL_ROWS, D_MODEL), BF16),
            jax.ShapeDtypeStruct((n_tok_tiles * TILE, LANE), F32),
            jax.ShapeDtypeStruct((n_tok_tiles, SUBLANE, LANE), F32),
            jax.ShapeDtypeStruct((SUBLANE, LANE), F32),
        ),
        grid=(B, nt),
        in_specs=attn_specs + [
            tok(D_CONV), left(D_CONV), right(D_CONV),
            tok(D_POOL), left(D_POOL), right(D_POOL),
            pl.BlockSpec((1, TILE, D_MODEL), lambda b, i: (b, 0, 0)),
            pl.BlockSpec((1, TILE, D_MODEL),
                         lambda b, i: (b, jnp.maximum(i + first_tile + lat_off, 0), 0)),
            pl.BlockSpec((1, 8, D_MODEL), mod_idx),
            full(cw), full(cb), full(cg), full(cbeta), full(pw), full(ps), full(wout),
            full(g1), full(b1), full(wr), full(br), full(tri), full(upper)],
        out_specs=(own(D_MODEL),
                   pl.BlockSpec((LOCAL_ROWS, D_MODEL), lambda b, i: (b * nt + i, 0)),
                   pl.BlockSpec((TILE, LANE), lambda b, i: (b * nt + i, 0)),
                   pl.BlockSpec((1, SUBLANE, LANE), lambda b, i: (b * nt + i, 0, 0)),
                   pl.BlockSpec((SUBLANE, LANE), lambda b, i: (0, 0))),
        scratch_shapes=[pltpu.VMEM((TILE + 2 * HALO, D_CONV), F32),
                        pltpu.VMEM((TILE + 2 * HALO, D_POOL), F32),
                        pltpu.VMEM((SUBLANE, ph_rows, D_CONV), F32),
                        pltpu.VMEM((SUBLANE, ph_rows, D_POOL), F32),
                        pltpu.VMEM((SUBLANE, LANE), F32)],
        compiler_params=pltpu.CompilerParams(
            dimension_semantics=("arbitrary", "arbitrary"), vmem_limit_bytes=VMEM_LIMIT),
        name="mixout",
    )(*attn_args, u, u, u, poolin, poolin, poolin, xc, xl, mods, cw, cb, cg, cbeta, pw, ps, wout,
      g1, b1, wr, br, tri, upper)


def _chunk_copy(src, src_chunk, buf, dst_chunk, sem):
    src_row = src_chunk * CHUNK
    if not isinstance(src_row, int):
        src_row = pl.multiple_of(src_row, CHUNK)
    return pltpu.make_async_copy(
        src.at[pl.ds(src_row, CHUNK)], buf.at[pl.ds(dst_chunk * CHUNK, CHUNK)], sem)


def _for_chunk_groups(n_chunks, count, fn):
    for g0 in range(0, n_chunks, GATHER_GROUP):
        def group(g0=g0):
            for c in range(g0, min(g0 + GATHER_GROUP, n_chunks)):
                fn(c)
        if count is None:
            group()
        else:
            pl.when(g0 < count)(group)


def _start_gather(src, table_ref, n_chunks, buf, sem, count=None):
    _for_chunk_groups(n_chunks, count,
                      lambda c: _chunk_copy(src, table_ref[0, 0, c], buf, c, sem).start())


def _wait_gather(src, n_chunks, buf, sem, count=None):
    _for_chunk_groups(n_chunks, count, lambda c: _chunk_copy(src, 0, buf, c, sem).wait())


def _double_buffered_gather(step, n_steps, src, cur_tab, nxt_tab, n_chunks, bufs, sems, compute,
                            counted=False):
    cur_count = cur_tab[0, 0, cur_tab.shape[-1] - 1] if counted else None
    nxt_count = nxt_tab[0, 0, nxt_tab.shape[-1] - 1] if counted else None

    @pl.when(step == 0)
    def _():
        if counted:
            for buf in bufs:
                buf[...] = jnp.zeros_like(buf)
        _start_gather(src, cur_tab, n_chunks, bufs[0], sems.at[0], cur_count)

    for par in range(2):
        @pl.when((lax.rem(step, 2) == par) & (step < n_steps))
        def _(par=par):
            _start_gather(src, nxt_tab, n_chunks, bufs[1 - par], sems.at[1 - par], nxt_count)
            _wait_gather(src, n_chunks, bufs[par], sems.at[par], cur_count)
            compute(bufs[par])

            @pl.when(step == n_steps - 1)
            def _():
                _wait_gather(src, n_chunks, bufs[1 - par], sems.at[1 - par], nxt_count)


def _experts_kernel(te_ref, ns_ref, cur_ref, nxt_ref, xloc_ref, wga_ref, wua_ref, wda_ref,
                    wgb_ref, wub_ref, wdb_ref, ys_ref,
                    xbuf_a, xbuf_b, cga, cua, cda, cgb, cub, cdb, sem):
    i = pl.program_id(0)
    w_refs = ((wga_ref, wua_ref, wda_ref), (wgb_ref, wub_ref, wdb_ref))
    caches = ((cga, cua, cda), (cgb, cub, cdb))

    for t in range(EXPERT_TILES):
        tile = EXPERT_TILES * i + t
        prev = jnp.maximum(tile - EXPERT_TILES, 0)

        @pl.when((i == 0) | (te_ref[tile] != te_ref[prev]))
        def _(t=t):
            for cache, w_ref in zip(caches[t], w_refs[t]):
                cache[...] = w_ref[0, 0].astype(BF16)

    def compute(xbuf):
        for t in range(EXPERT_TILES):
            wg, wu, wd = caches[t]
            xt = xbuf[t * TILE:(t + 1) * TILE, :]
            hid = (_silu(_dot(xt, wg[...])) * _dot(xt, wu[...])).astype(BF16)
            ys_ref[t * TILE:(t + 1) * TILE, :] = _dot(hid, wd[...]).astype(BF16)

    _double_buffered_gather(i, ns_ref[0], xloc_ref, cur_ref, nxt_ref, EXPERT_TILES * TILE_CHUNKS,
                            (xbuf_a, xbuf_b), sem, compute)

    @pl.when(i >= ns_ref[0])
    def _():
        ys_ref[...] = jnp.zeros_like(ys_ref)


def _experts(tile_expert, n_used_steps, src_tab, xloc, w_gate, w_up, w_down, layer):
    n_steps = src_tab.shape[0]
    rows = EXPERT_TILES * TILE

    def w_idx(t):
        return lambda i, te, ns: (layer, te[EXPERT_TILES * i + t], 0, 0)

    tab = lambda f: pl.BlockSpec((1, 1, EXPERT_TILES * TILE_CHUNKS), f, memory_space=pltpu.SMEM)
    w_specs, w_cache = [], []
    for t in range(EXPERT_TILES):
        w_specs += [pl.BlockSpec((1, 1, D_MODEL, D_EXPERT), w_idx(t)),
                    pl.BlockSpec((1, 1, D_MODEL, D_EXPERT), w_idx(t)),
                    pl.BlockSpec((1, 1, D_EXPERT, D_MODEL), w_idx(t))]
        w_cache += [pltpu.VMEM((D_MODEL, D_EXPERT), BF16), pltpu.VMEM((D_MODEL, D_EXPERT), BF16),
                    pltpu.VMEM((D_EXPERT, D_MODEL), BF16)]
    return pl.pallas_call(
        _experts_kernel,
        out_shape=jax.ShapeDtypeStruct((n_steps * rows, D_MODEL), BF16),
        grid_spec=pltpu.PrefetchScalarGridSpec(
            num_scalar_prefetch=2,
            grid=(n_steps,),
            in_specs=[tab(lambda i, te, ns: (i, 0, 0)),
                      tab(lambda i, te, ns: (jnp.minimum(i + 1, n_steps - 1), 0, 0)),
                      pl.BlockSpec(memory_space=pl.ANY)] + w_specs,
            out_specs=pl.BlockSpec((rows, D_MODEL), lambda i, te, ns: (i, 0)),
            scratch_shapes=[pltpu.VMEM((rows, D_MODEL), BF16), pltpu.VMEM((rows, D_MODEL), BF16)]
            + w_cache + [pltpu.SemaphoreType.DMA((2,))],
        ),
        compiler_params=pltpu.CompilerParams(
            dimension_semantics=("arbitrary",), vmem_limit_bytes=VMEM_LIMIT),
        name="experts",
    )(tile_expert, n_used_steps, src_tab, src_tab, xloc, *([w_gate, w_up, w_down] * EXPERT_TILES))


def _combine_kernel(cur_ref, nxt_ref, ys_ref, rinfo_ref, x_ref, mod_ref, g2_ref, b2_ref, o_ref,
                    ybuf_a, ybuf_b, sem, *, n_tok_tiles):
    g = pl.program_id(0) * pl.num_programs(1) + pl.program_id(1)

    def compute(ybuf):
        slot_lane = lax.broadcasted_iota(I32, (TILE, LOCAL_ROWS), 1).astype(F32)
        sel = (jnp.where(slot_lane == rinfo_ref[:, 2:3], rinfo_ref[:, 0:1], 0.0)
               + jnp.where(slot_lane == rinfo_ref[:, 3:4], rinfo_ref[:, 1:2], 0.0))
        y = _dot(sel.astype(BF16), ybuf[...])
        gate2 = mod_ref[0, 5:6, :]
        o_ref[0] = _layer_norm_rows(DEEPNORM_ALPHA * x_ref[0] + gate2 * y, g2_ref[...], b2_ref[...])

    _double_buffered_gather(g, n_tok_tiles, ys_ref, cur_ref, nxt_ref, LOCAL_CHUNKS,
                            (ybuf_a, ybuf_b), sem, compute, counted=True)


def _combine(dst_tab, ys, rinfo, x1, mods, g2, b2, first_tile):
    B = x1.shape[0]
    nt = N_TILES - first_tile
    nb = mods.shape[0] - 1
    n_tok_tiles = B * nt

    def mod_idx(b, i):
        return (jnp.where(i + first_tile == 0, nb, b), 0, 0)

    full = lambda a: pl.BlockSpec(a.shape, lambda b, i: (0,) * a.ndim)
    tab = lambda f: pl.BlockSpec((1, 1, LANE), f, memory_space=pltpu.SMEM)
    return pl.pallas_call(
        functools.partial(_combine_kernel, n_tok_tiles=n_tok_tiles),
        out_shape=jax.ShapeDtypeStruct((B, nt * TILE, D_MODEL), F32),
        grid=(B, nt),
        in_specs=[tab(lambda b, i: (b * nt + i, 0, 0)),
                  tab(lambda b, i: (jnp.minimum(b * nt + i + 1, n_tok_tiles - 1), 0, 0)),
                  pl.BlockSpec(memory_space=pl.ANY),
                  pl.BlockSpec((TILE, LANE), lambda b, i: (b * nt + i, 0)),
                  pl.BlockSpec((1, TILE, D_MODEL), lambda b, i: (b, i, 0)),
                  pl.BlockSpec((1, 8, D_MODEL), mod_idx),
                  full(g2), full(b2)],
        out_specs=pl.BlockSpec((1, TILE, D_MODEL), lambda b, i: (b, i, 0)),
        scratch_shapes=[pltpu.VMEM((LOCAL_ROWS, D_MODEL), BF16), pltpu.VMEM((LOCAL_ROWS, D_MODEL), BF16),
                        pltpu.SemaphoreType.DMA((2,))],
        compiler_params=pltpu.CompilerParams(
            dimension_semantics=("arbitrary", "arbitrary"), vmem_limit_bytes=VMEM_LIMIT),
        name="combine",
    )(dst_tab, dst_tab, ys, rinfo, x1, mods, g2, b2)


def _rot_cols(w):
    w4 = w.reshape(w.shape[:-1] + (2, 2, QK_ROPE // 4))
    return jnp.stack([-w4[..., 1, :], w4[..., 0, :]], axis=-2).reshape(w.shape)


def _rope_tables():
    rows = SEQ // GRID_W
    row = jnp.repeat(jnp.arange(rows), GRID_W)
    col = jnp.tile(jnp.arange(GRID_W), rows)
    d_axis = QK_ROPE // 2
    inv_freq = jnp.power(ROPE_THETA, -jnp.arange(0, d_axis, 2, dtype=F32) / d_axis)

    def axis_angles(p):
        a = p.astype(F32)[:, None] * inv_freq[None, :]
        return jnp.concatenate([a, a], -1)

    ang = jnp.concatenate([axis_angles(row), axis_angles(col)], -1)
    cos = jnp.concatenate([jnp.ones((CTX_LEN, QK_ROPE), F32), jnp.cos(ang)], 0)
    sin = jnp.concatenate([jnp.zeros((CTX_LEN, QK_ROPE), F32), jnp.sin(ang)], 0)
    scale = math.log2(math.e) / math.sqrt(QK_HEAD)
    z64 = jnp.zeros((SEQ_ALL, QK_NOPE), F32)
    z32 = jnp.zeros((SEQ_ALL, HEAD_PAD - QK_HEAD), F32)
    cq = jnp.concatenate([jnp.full((SEQ_ALL, QK_NOPE), scale, F32), scale * cos, z32], -1)
    sq = jnp.concatenate([z64, scale * sin, z32], -1)
    ck = jnp.concatenate([z64, cos, z32], -1)
    sk = jnp.concatenate([z64, sin, z32], -1)
    return cq, sq, ck, sk


def _prep_layer(w_in, w_uq, w_ukv, conv_w, pool_w, w_out, w_rg, b_rg, w_re, b_re):
    zc = lambda r, c: jnp.zeros((r, c), F32)
    kr = w_in[:, OFF_KR:OFF_CONV]
    pad_l, pad_r = QK_NOPE, HEAD_PAD - QK_HEAD
    kr_arr = jnp.concatenate([zc(D_MODEL, pad_l), kr, zc(D_MODEL, pad_r)], -1)
    krr_arr = jnp.concatenate([zc(D_MODEL, pad_l), _rot_cols(kr), zc(D_MODEL, pad_r)], -1)
    win_ext = jnp.concatenate(
        [w_in[:, :OFF_KR], kr_arr, krr_arr, w_in[:, OFF_CONV:]], -1).astype(BF16)

    wq3 = w_uq.reshape(Q_RANK, N_HEADS, QK_HEAD)
    nope, rope = wq3[..., :QK_NOPE], wq3[..., QK_NOPE:]
    z3 = lambda c: jnp.zeros((Q_RANK, N_HEADS, c), F32)
    q_arr = jnp.concatenate([nope, rope, z3(pad_r)], -1).reshape(Q_RANK, N_HEADS * HEAD_PAD)
    q_rot = jnp.concatenate([z3(pad_l), _rot_cols(rope), z3(pad_r)], -1).reshape(Q_RANK, N_HEADS * HEAD_PAD)
    wuq_ext = jnp.concatenate([q_arr, q_rot], -1).astype(BF16)

    wkv3 = w_ukv.reshape(KV_RANK, N_HEADS, QK_NOPE + V_HEAD)
    zkv = jnp.zeros((KV_RANK, N_HEADS, HEAD_PAD - QK_NOPE), F32)
    k_arr = jnp.concatenate([wkv3[..., :QK_NOPE], zkv], -1).reshape(KV_RANK, N_HEADS * HEAD_PAD)
    v_arr = jnp.concatenate([wkv3[..., QK_NOPE:], zkv], -1).reshape(KV_RANK, N_HEADS * HEAD_PAD)
    wuk, wvt = k_arr.astype(BF16), v_arr.T.astype(BF16)

    cw = jnp.concatenate([conv_w, jnp.zeros((1, D_CONV), F32)], 0)
    gdim = D_POOL // len(POOL_WINDOWS)
    pw = jnp.zeros((D_POOL, D_POOL), F32)
    for g in range(len(POOL_WINDOWS)):
        pw = pw.at[g * gdim:(g + 1) * gdim, g * gdim:(g + 1) * gdim].set(pool_w[g])
    wr = jnp.concatenate([w_re, w_rg, zc(D_MODEL, LANE - N_EXPERTS - N_GROUPS)], -1).astype(BF16)
    br = jnp.concatenate([b_re, b_rg, jnp.zeros((LANE - N_EXPERTS - N_GROUPS,), F32)])[None, :]
    return win_ext, wuq_ext, wuk, wvt, cw, pw.astype(BF16), w_out.astype(BF16), wr, br


def _routing_tables(ctab, totals, n_tok_tiles):
    n_tiles_max = -(-(MAX_USED_CHUNKS * n_tok_tiles + (TILE_CHUNKS - 1) * N_EXPERTS) // TILE_CHUNKS)
    n_tiles_max = -(-n_tiles_max // EXPERT_TILES) * EXPERT_TILES
    n_chunks_max = n_tiles_max * TILE_CHUNKS
    chunks = totals[0, :N_EXPERTS].astype(I32)
    padded = ((chunks + TILE_CHUNKS - 1) // TILE_CHUNKS) * TILE_CHUNKS
    ends = jnp.cumsum(padded)
    starts = ends - padded
    e_of_c = ctab[:, 0, :LOCAL_CHUNKS].astype(I32)
    rel_c = ctab[:, 1, :LOCAL_CHUNKS].astype(I32)
    used = ctab[:, 2, :LOCAL_CHUNKS] > 0.5
    owner = e_of_c[..., None] == jnp.arange(N_EXPERTS, dtype=I32)
    dst = jnp.sum(jnp.where(owner, starts, 0), axis=-1) + rel_c
    local_id = (jnp.arange(n_tok_tiles, dtype=I32)[:, None] * LOCAL_CHUNKS
                + jnp.arange(LOCAL_CHUNKS, dtype=I32)[None, :])
    src = jnp.full((n_chunks_max,), LOCAL_CHUNKS - 1, I32)
    src = src.at[jnp.where(used, dst, n_chunks_max).reshape(-1)].set(local_id.reshape(-1), mode="drop")
    src_tab = src.reshape(n_tiles_max // EXPERT_TILES, 1, EXPERT_TILES * TILE_CHUNKS)
    dst_tab = jnp.where(used, dst, dst[:, 0:1])
    n_used = jnp.sum(used.astype(I32), axis=-1, keepdims=True)
    dst_tab = jnp.concatenate(
        [dst_tab, jnp.zeros((n_tok_tiles, LANE - LOCAL_CHUNKS - 1), I32), n_used],
        -1).reshape(n_tok_tiles, 1, LANE)
    n_valid = ends[-1] // TILE_CHUNKS
    tile_start = jnp.minimum(jnp.arange(n_tiles_max, dtype=I32), n_valid - 1) * TILE_CHUNKS
    tile_expert = jnp.sum((tile_start[:, None] >= ends[None, :]).astype(I32), axis=-1)
    n_used_steps = (n_valid + EXPERT_TILES - 1) // EXPERT_TILES
    return src_tab, dst_tab, tile_expert.astype(I32), n_used_steps.astype(I32).reshape(1)


def kernel(x, c, ctx, c_ctx, w_ada, b_ada, w_in, g_q, w_uq, g_kv, w_ukv, conv_w, conv_b, conv_ln_g,
           conv_ln_b, pool_w, pool_scale, w_out, ln1_g, ln1_b, w_router_group, b_router_group,
           w_router_expert, b_router_expert, w_gate, w_up, w_down, ln2_g, ln2_b):
    B = x.shape[0]
    L = w_ada.shape[0]
    tabs = _rope_tables()
    tri = jnp.tril(jnp.ones((TILE, TILE), F32), -1).astype(BF16)
    upper = jnp.triu(jnp.ones((LANE, LANE), F32), 1).astype(BF16)

    cc = jnp.concatenate([c, c_ctx[None, :], jnp.zeros((16 - B - 1, D_MODEL), F32)], 0)
    ada = _ada(cc, w_ada, b_ada)
    mods_all = ada[:, :B + 1].reshape(L, B + 1, 6, D_MODEL)
    mods_all = jnp.concatenate([mods_all, jnp.zeros((L, B + 1, 2, D_MODEL), F32)], 2)

    xc, xl, lat_off = ctx, x, -1
    row = lambda a: a[None, :]
    for l in range(L):
        last = l == L - 1
        first_tile = 1 if last else 0
        mods = mods_all[l]
        win_ext, wuq_ext, wuk, wvt, cw, pw, wout, wr, br = _prep_layer(
            w_in[l], w_uq[l], w_ukv[l], conv_w[l], pool_w[l], w_out[l], w_router_group[l],
            b_router_group[l], w_router_expert[l], b_router_expert[l])

        q, k, v, u, poolin = _inproj(xc, xl, lat_off, mods, tabs, win_ext, row(g_q[l]), wuq_ext, row(g_kv[l]),
                                     wuk, wvt)
        attn = _attention(q, k, v)
        attn_ctx = None if last else _attention_ctx(q, k, v)
        x1, xloc, rinfo, ctab, totals = _mixout(
            attn, attn_ctx, u, poolin, xc, xl, lat_off, mods, cw, row(conv_b[l]), row(conv_ln_g[l]),
            row(conv_ln_b[l]), pw, row(pool_scale[l]), wout, row(ln1_g[l]), row(ln1_b[l]), wr, br,
            tri, upper, first_tile)

        n_tok_tiles = B * (N_TILES - first_tile)
        src_tab, dst_tab, tile_expert, n_used_steps = _routing_tables(ctab, totals, n_tok_tiles)
        ys = _experts(tile_expert, n_used_steps, src_tab, xloc, w_gate, w_up, w_down, l)
        xl = _combine(dst_tab, ys, rinfo, x1, mods, row(ln2_g[l]), row(ln2_b[l]), first_tile)
        xc, lat_off = xl, 0
    return xl
```
